```python
import math
import jax
import jax.numpy as jnp
from jax import lax
import numpy as np

D_MODEL = 1024
BATCH = 1
SEQ = 16384
DEPTH = 4
DEC_BATCH = 32
DEC_SEQ = 32
PAST_LEN = 2048

CHUNK = 64
N_META = 16
N_EVEN = (DEPTH + 1) // 2
N_ODD = DEPTH // 2
EPS = 1e-6
MIX_HALF = D_MODEL // 2

A_HEADS = 4
A_DH = MIX_HALF // (2 * A_HEADS)
A_DV = 2 * A_DH
A_WIDTH = A_HEADS * A_DV
A_QK = A_HEADS * 2 * A_DH
N_BUCKETS = 32
MAX_DIST = 128
Q_BLOCK = 128
B_HEADS = 4
B_DK = MIX_HALF // B_HEADS
B_DV = B_DK
B_WIDTH = B_HEADS * B_DV
B_CONV = 4
C_GROUP = 16
C_WIDTH = MIX_HALF
C_GROUPS = C_WIDTH // C_GROUP
C_STATE = 64
D_HEAD = 64
D_WIDTH = MIX_HALF
D_HEADS = D_WIDTH // D_HEAD
D_LORA_W = 64
D_LORA_A = 64
D_LORA_G = 128
D_COLS = 3 * D_WIDTH + D_LORA_W + D_LORA_A + D_LORA_G
D_GN_EPS = 64e-5
FFN_HIDDEN = -(-8 * D_MODEL // (3 * 256)) * 256

EV_SIZES = (A_QK, A_QK, A_WIDTH, 2 * B_WIDTH, B_WIDTH, B_WIDTH, B_HEADS, B_HEADS)
EV_COLS = sum(EV_SIZES)
EV_SPLITS = tuple(int(s) for s in np.cumsum(EV_SIZES)[:-1])
D_SIZES = (D_WIDTH, D_WIDTH, D_WIDTH, D_LORA_W, D_LORA_A, D_LORA_G)
D_SPLITS = tuple(int(s) for s in np.cumsum(D_SIZES)[:-1])
OD_COLS = C_WIDTH + D_COLS

kernel_name = 'hybrid_chunk_streaming_encoder_step'


def rmsnorm(x, g):
    xf = x.astype(jnp.float32)
    y = xf * lax.rsqrt(jnp.mean(xf * xf, axis=-1, keepdims=True) + EPS)
    return (y * g.astype(jnp.float32)).astype(x.dtype)


def swiglu(h, w1, w3, w2):
    return (jax.nn.silu(h @ w1) * (h @ w3)) @ w2


def t5_bucket(rel):
    nb = N_BUCKETS // 2
    max_exact = nb // 2
    ret = jnp.where(rel > 0, nb, 0)
    n = jnp.abs(rel)
    nf = jnp.maximum(n, 1).astype(jnp.float32)
    large = max_exact + (jnp.log(nf / max_exact) / math.log(MAX_DIST / max_exact) * (nb - max_exact)).astype(jnp.int32)
    large = jnp.minimum(large, nb - 1)
    return ret + jnp.where(n < max_exact, n, large)


def diff_attention(q, k, v, q_pos, q_cid, k_pos, k_cid, rel_bias, lam):
    f = jnp.float32
    bsz, tq = q.shape[0], q.shape[1]
    qb = min(Q_BLOCK, tq)
    nblk = -(-tq // qb)
    padq = nblk * qb - tq
    q = jnp.pad(q, ((0, 0), (0, padq), (0, 0), (0, 0), (0, 0)))
    q_pos = jnp.pad(q_pos, (0, padq))
    q_cid = jnp.pad(q_cid, (0, padq))
    q_blocks = jnp.moveaxis(q.reshape(bsz, nblk, qb, A_HEADS, 2, A_DH), 1, 0)
    kf = k.astype(f) * (A_DH ** -0.5)
    vf = v.astype(f)

    def one_block(args):
        qblk, pblk, cblk = args
        s = jnp.einsum('bqhmd,bkhmd->bhmqk', qblk.astype(f), kf)
        bias = jnp.transpose(rel_bias.astype(f)[t5_bucket(k_pos[None, :] - pblk[:, None])], (2, 0, 1))
        visible = k_cid[None, :] <= cblk[:, None]
        s = jnp.where(visible, s + bias[None, :, None], -jnp.inf)
        prob = jax.nn.softmax(s, axis=-1)
        w = prob[:, :, 0] - lam * prob[:, :, 1]
        return jnp.einsum('bhqk,bkhe->bqhe', w, vf)

    o = lax.map(one_block, (q_blocks, q_pos.reshape(nblk, qb), q_cid.reshape(nblk, qb)))
    o = jnp.moveaxis(o, 0, 1).reshape(bsz, nblk * qb, A_HEADS, A_DV)
    return o[:, :tq]


def mlstm_chunkwise(q, k, v, logi, logf, c0, n0, m0):
    f = jnp.float32
    bsz, t, nh, _ = q.shape
    ln = min(CHUNK, t)
    pad = (-t) % ln
    pt = ((0, 0), (pad, 0), (0, 0), (0, 0))
    q = jnp.pad(q.astype(f), pt)
    k = jnp.pad(k.astype(f), pt)
    v = jnp.pad(v.astype(f), pt)
    logi = jnp.pad(logi.astype(f), pt[:3], constant_values=-jnp.inf)
    logf = jnp.pad(logf.astype(f), pt[:3])
    nc = (t + pad) // ln

    def to_chunks(a):
        a = a.reshape((bsz, nc, ln) + a.shape[2:])
        return jnp.swapaxes(jnp.moveaxis(a, 1, 0), 2, 3)

    tril = jnp.tril(jnp.ones((ln, ln), dtype=bool))

    def step(carry, inp):
        c, n, m = carry
        qc, kc, vc, li, lf = inp
        b = jnp.cumsum(lf, axis=-1)
        inter = b + m[..., None]
        dmat = jnp.where(tril, b[..., :, None] - b[..., None, :] + li[..., None, :], -jnp.inf)
        mt = jnp.maximum(inter, jnp.max(dmat, axis=-1))
        w_inter = jnp.exp(inter - mt)
        s = jnp.einsum('bhtd,bhsd->bhts', qc, kc) * jnp.exp(dmat - mt[..., None])
        num = w_inter[..., None] * jnp.einsum('bhtd,bhde->bhte', qc, c) + jnp.einsum('bhts,bhse->bhte', s, vc)
        den = w_inter * jnp.einsum('bhtd,bhd->bht', qc, n) + jnp.sum(s, axis=-1)
        h = num / jnp.maximum(jnp.abs(den), jnp.exp(-mt))[..., None]
        b_last = b[..., -1]
        g = b_last[..., None] - b + li
        m_new = jnp.maximum(b_last + m, jnp.max(g, axis=-1))
        decay = jnp.exp(b_last + m - m_new)
        wk = jnp.exp(g - m_new[..., None])
        c_new = decay[..., None, None] * c + jnp.einsum('bhs,bhsd,bhse->bhde', wk, kc, vc)
        n_new = decay[..., None] * n + jnp.einsum('bhs,bhsd->bhd', wk, kc)
        return (c_new, n_new, m_new), h

    (c1, n1, m1), h = lax.scan(step, (c0.astype(f), n0.astype(f), m0.astype(f)),
                               (to_chunks(q), to_chunks(k), to_chunks(v), to_chunks(logi), to_chunks(logf)))
    h = jnp.moveaxis(jnp.swapaxes(h, 2, 3), 0, 1).reshape(bsz, t + pad, nh, v.shape[-1])
    return h[:, pad:], c1, n1, m1


def _complex_affine_combine(e1, e2):
    a1r, a1i, b1r, b1i = e1
    a2r, a2i, b2r, b2i = e2
    return (a2r * a1r - a2i * a1i, a2r * a1i + a2i * a1r,
            a2r * b1r - a2i * b1i + b2r, a2r * b1i + a2i * b1r + b2i)


def s5_ssm(u, lam_re, lam_im, log_dt, b_re, b_im, c_re, c_im, d_skip, x0_re, x0_im):
    f = jnp.float32
    bsz, t, _ = u.shape
    uf = u.astype(f)
    ug = uf.reshape(bsz, t, C_GROUPS, C_GROUP)
    lr, li = lam_re.astype(f), lam_im.astype(f)
    dt = jnp.exp(log_dt.astype(f))[:, None]
    mag = jnp.exp(lr * dt)
    ar, ai = mag * jnp.cos(li * dt), mag * jnp.sin(li * dt)
    den = lr * lr + li * li
    cr = ((ar - 1.0) * lr + ai * li) / den
    ci = (ai * lr - (ar - 1.0) * li) / den
    br, bi = b_re.astype(f), b_im.astype(f)
    bbr = cr[..., None] * br - ci[..., None] * bi
    bbi = cr[..., None] * bi + ci[..., None] * br
    bur = jnp.einsum('btgc,gpc->btgp', ug, bbr)
    bui = jnp.einsum('btgc,gpc->btgp', ug, bbi)
    x0r, x0i = x0_re.astype(f), x0_im.astype(f)
    bur = bur.at[:, 0].add(ar * x0r - ai * x0i)
    bui = bui.at[:, 0].add(ar * x0i + ai * x0r)
    a_r = jnp.broadcast_to(ar, bur.shape)
    a_i = jnp.broadcast_to(ai, bui.shape)
    _, _, xr, xi = lax.associative_scan(_complex_affine_combine, (a_r, a_i, bur, bui), axis=1)
    y = jnp.einsum('btgp,gcp->btgc', xr, c_re.astype(f)) - jnp.einsum('btgp,gcp->btgc', xi, c_im.astype(f))
    y = y.reshape(bsz, t, C_WIDTH) + d_skip.astype(f) * uf
    return y, xr[:, -1], xi[:, -1]


def rwkv7_mix(dcols, shift0, s0, mu, w0, w_w2, a0, w_a2, w_g2, k_k, k_a, r_k, ln_g, ln_b):
    f = jnp.float32
    bsz, t, _ = dcols.shape
    prev = jnp.concatenate([shift0.astype(dcols.dtype), dcols[:, :-1]], axis=1)
    xm = dcols + mu * (prev - dcols)
    r, k, v, wlo, alo, glo = jnp.split(xm, D_SPLITS, axis=-1)
    w_raw = (w0 + jnp.tanh(wlo) @ w_w2).astype(f)
    decay = jnp.exp(-jnp.exp(-jax.nn.softplus(-w_raw) - 0.5))
    a = jax.nn.sigmoid((a0 + alo @ w_a2).astype(f))
    g = (jax.nn.sigmoid(glo) @ w_g2).astype(f)

    def heads(z):
        return z.astype(f).reshape(bsz, t, D_HEADS, D_HEAD)

    r, k, v, a, decay = heads(r), heads(k), heads(v), heads(a), heads(decay)
    kk = k * k_k.astype(f).reshape(D_HEADS, D_HEAD)
    kk = kk / jnp.maximum(jnp.sqrt(jnp.sum(kk * kk, axis=-1, keepdims=True)), 1e-12)
    k = k * (1.0 + (a - 1.0) * k_a.astype(f).reshape(D_HEADS, D_HEAD))

    def step(s, inp):
        rt, wt, kt, vt, kkt, at = inp
        sa = jnp.einsum('bhvk,bhk->bhv', s, -kkt)
        s = s * wt[:, :, None, :] + sa[..., None] * (kkt * at)[:, :, None, :] + vt[..., None] * kt[:, :, None, :]
        return s, jnp.einsum('bhvk,bhk->bhv', s, rt)

    def tm(z):
        return jnp.moveaxis(z, 1, 0)

    s1, y = lax.scan(step, s0.astype(f), (tm(r), tm(decay), tm(k), tm(v), tm(kk), tm(a)))
    y = jnp.moveaxis(y, 0, 1)
    mean = jnp.mean(y, axis=-1, keepdims=True)
    var = jnp.mean(jnp.square(y - mean), axis=-1, keepdims=True)
    y = (y - mean) * lax.rsqrt(var + D_GN_EPS) * ln_g.astype(f).reshape(D_HEADS, D_HEAD) + ln_b.astype(f).reshape(D_HEADS, D_HEAD)
    y = y + jnp.sum(r * k * r_k.astype(f).reshape(D_HEADS, D_HEAD), axis=-1, keepdims=True) * v
    y = y.reshape(bsz, t, D_WIDTH) * g
    return y.astype(dcols.dtype), s1, dcols[:, -1:]


def even_layer(h, e, layer, q_pos, q_cid, k_pos, k_cid, past_k, past_v, bc0, bn0, bm0, bconv0, p):
    f = jnp.float32
    bsz, t, _ = h.shape
    proj = h @ p['ev_w_in'][e]
    aq, ak, av, bqk, bv, bo, bi, bfg = jnp.split(proj, EV_SPLITS, axis=-1)
    new_k = ak.reshape(bsz, t, A_HEADS, 2 * A_DH)
    new_v = av.reshape(bsz, t, A_HEADS, A_DV)
    if past_k is None:
        keys, vals = new_k, new_v
    else:
        keys = jnp.concatenate([past_k.astype(h.dtype), new_k], axis=1)
        vals = jnp.concatenate([past_v.astype(h.dtype), new_v], axis=1)
    lam_init = 0.8 - 0.6 * math.exp(-0.3 * layer)
    lam = (jnp.exp(jnp.sum(p['a_lq1'][e].astype(f) * p['a_lk1'][e].astype(f)))
           - jnp.exp(jnp.sum(p['a_lq2'][e].astype(f) * p['a_lk2'][e].astype(f))) + lam_init)
    oa = diff_attention(aq.reshape(bsz, t, A_HEADS, 2, A_DH), keys.reshape(bsz, -1, A_HEADS, 2, A_DH), vals,
                        q_pos, q_cid, k_pos, k_cid, p['rel_bias'], lam)
    oa = (rmsnorm(oa, p['a_subln'][e]) * (1.0 - lam_init)).reshape(bsz, t, A_WIDTH).astype(h.dtype)
    xc = jnp.concatenate([bconv0.astype(h.dtype), bqk], axis=1)
    cw = p['b_conv_w'][e]
    conv = p['b_conv_b'][e] + cw[0] * xc[:, 0:t]
    for j in range(1, B_CONV):
        conv = conv + cw[j] * xc[:, j:j + t]
    conv = jax.nn.silu(conv)
    bq = conv[..., :B_WIDTH].reshape(bsz, t, B_HEADS, B_DK) * (B_DK ** -0.5)
    bk = conv[..., B_WIDTH:].reshape(bsz, t, B_HEADS, B_DK)
    logi = (bi + p['b_ig_bias'][e]).astype(f)
    logf = jax.nn.log_sigmoid((bfg + p['b_fg_bias'][e]).astype(f))
    hb, bc, bn, bm = mlstm_chunkwise(bq, bk, bv.reshape(bsz, t, B_HEADS, B_DV), logi, logf, bc0, bn0, bm0)
    hb = rmsnorm(hb, p['b_norm'][e].reshape(B_HEADS, B_DV)) * jax.nn.sigmoid(bo.astype(f)).reshape(bsz, t, B_HEADS, B_DV)
    hb = hb.reshape(bsz, t, B_WIDTH).astype(h.dtype)
    out = jnp.concatenate([oa, hb], axis=-1) @ p['ev_w_out'][e]
    return out, new_k, new_v, bc, bn, bm, xc[:, -(B_CONV - 1):]


def odd_layer(h, o, cre0, cim0, ds0, dsh0, p):
    proj = h @ p['od_w_in'][o]
    cu, dcols = proj[..., :C_WIDTH], proj[..., C_WIDTH:]
    y, cre, cim = s5_ssm(cu, p['c_lam_re'][o], p['c_lam_im'][o], p['c_log_dt'][o], p['c_b_re'][o], p['c_b_im'][o],
                         p['c_c_re'][o], p['c_c_im'][o], p['c_d'][o], cre0, cim0)
    yg = jax.nn.gelu(y)
    oc = (yg * jax.nn.sigmoid(yg @ p['c_w_glu'][o].astype(jnp.float32))).astype(h.dtype)
    od, ds, dsh = rwkv7_mix(dcols, dsh0, ds0, p['d_mu'][o], p['d_w0'][o], p['d_w_w2'][o], p['d_a0'][o], p['d_w_a2'][o],
                            p['d_w_g2'][o], p['d_k_k'][o], p['d_k_a'][o], p['d_r_k'][o], p['d_ln_g'][o], p['d_ln_b'][o])
    out = jnp.concatenate([oc, od], axis=-1) @ p['od_w_out'][o]
    return out, cre, cim, ds, dsh


def trunk(x, q_pos, q_cid, k_pos, k_cid, past_k, past_v, st, p):
    dt = x.dtype
    names_even = ('a_k', 'a_v', 'b_c', 'b_n', 'b_m', 'b_conv')
    names_odd = ('c_re', 'c_im', 'd_s', 'd_shift')
    new = {name: [] for name in names_even + names_odd}
    for layer in range(DEPTH):
        h = rmsnorm(x, p['norm_mix'][layer])
        if layer % 2 == 0:
            e = layer // 2
            pk = None if past_k is None else past_k[e]
            pv = None if past_v is None else past_v[e]
            mix, *vals = even_layer(h, e, layer, q_pos, q_cid, k_pos, k_cid, pk, pv,
                                    st['b_c'][e], st['b_n'][e], st['b_m'][e], st['b_conv'][e], p)
            for name, val in zip(names_even, vals):
                new[name].append(val.astype(dt))
        else:
            o = layer // 2
            mix, *vals = odd_layer(h, o, st['c_re'][o], st['c_im'][o], st['d_s'][o], st['d_shift'][o], p)
            for name, val in zip(names_odd, vals):
                new[name].append(val.astype(dt))
        x = x + mix
        x = x + swiglu(rmsnorm(x, p['norm_ffn'][layer]), p['ffn_w1'][layer], p['ffn_w3'][layer], p['ffn_w2'][layer])
    y = rmsnorm(x, p['norm_final'])
    s = {name: jnp.stack(vals, axis=0) for name, vals in new.items()}
    return (y, s['a_k'], s['a_v'], s['b_c'], s['b_n'], s['b_m'], s['b_conv'], s['c_re'], s['c_im'], s['d_s'], s['d_shift'])


def setup_inputs(seed: int = 0) -> dict:
    key = jax.random.key(seed)
    keys = jax.random.split(key, 64)
    ctr = [0]
    f = jnp.float32

    def nk():
        ctr[0] += 1
        return keys[ctr[0] - 1]

    def nrm(shape, scale=1.0):
        return scale * jax.random.normal(nk(), shape, f)

    def gain(shape):
        return 1.0 + nrm(shape, 0.02)

    d = {}
    d['x_prompt'] = nrm((BATCH, SEQ, D_MODEL))
    d['x_sample'] = nrm((DEC_BATCH, DEC_SEQ, D_MODEL))
    d['cache_a_k'] = nrm((N_EVEN, DEC_BATCH, N_META + PAST_LEN, A_HEADS, 2 * A_DH))
    d['cache_a_v'] = nrm((N_EVEN, DEC_BATCH, N_META + PAST_LEN, A_HEADS, A_DV))
    d['state_b_c'] = nrm((N_EVEN, DEC_BATCH, B_HEADS, B_DK, B_DV), 0.1)
    d['state_b_n'] = nrm((N_EVEN, DEC_BATCH, B_HEADS, B_DK), 0.1)
    d['state_b_m'] = nrm((N_EVEN, DEC_BATCH, B_HEADS))
    d['state_b_conv'] = nrm((N_EVEN, DEC_BATCH, B_CONV - 1, 2 * B_WIDTH))
    d['state_c_re'] = nrm((N_ODD, DEC_BATCH, C_GROUPS, C_STATE), 0.1)
    d['state_c_im'] = nrm((N_ODD, DEC_BATCH, C_GROUPS, C_STATE), 0.1)
    d['state_d_s'] = nrm((N_ODD, DEC_BATCH, D_HEADS, D_HEAD, D_HEAD), 0.1)
    d['state_d_shift'] = nrm((N_ODD, DEC_BATCH, 1, D_COLS))
    d['meta_tokens'] = nrm((N_META, D_MODEL))
    d['rel_bias'] = nrm((N_BUCKETS, A_HEADS), 0.5)
    d['norm_mix'] = gain((DEPTH, D_MODEL))
    d['norm_ffn'] = gain((DEPTH, D_MODEL))
    d['norm_final'] = gain((D_MODEL,))
    d['ev_w_in'] = nrm((N_EVEN, D_MODEL, EV_COLS), D_MODEL ** -0.5)
    d['ev_w_out'] = nrm((N_EVEN, A_WIDTH + B_WIDTH, D_MODEL), (A_WIDTH + B_WIDTH) ** -0.5)
    d['a_lq1'] = nrm((N_EVEN, A_DH), 0.1)
    d['a_lk1'] = nrm((N_EVEN, A_DH), 0.1)
    d['a_lq2'] = nrm((N_EVEN, A_DH), 0.1)
    d['a_lk2'] = nrm((N_EVEN, A_DH), 0.1)
    d['a_subln'] = gain((N_EVEN, A_DV))
    d['b_conv_w'] = nrm((N_EVEN, B_CONV, 2 * B_WIDTH), B_CONV ** -0.5)
    d['b_conv_b'] = nrm((N_EVEN, 2 * B_WIDTH), 0.01)
    d['b_ig_bias'] = nrm((N_EVEN, B_HEADS), 0.1)
    d['b_fg_bias'] = jnp.linspace(3.0, 6.0, B_HEADS, dtype=f)[None] + nrm((N_EVEN, B_HEADS), 0.1)
    d['b_norm'] = gain((N_EVEN, B_WIDTH))
    d['od_w_in'] = nrm((N_ODD, D_MODEL, OD_COLS), D_MODEL ** -0.5)
    d['od_w_out'] = nrm((N_ODD, C_WIDTH + D_WIDTH, D_MODEL), (C_WIDTH + D_WIDTH) ** -0.5)
    d['c_lam_re'] = -0.5 + nrm((N_ODD, C_GROUPS, C_STATE), 0.01)
    d['c_lam_im'] = math.pi * jnp.arange(C_STATE, dtype=f)[None, None] + nrm((N_ODD, C_GROUPS, C_STATE), 0.01)
    d['c_log_dt'] = jax.random.uniform(nk(), (N_ODD, C_GROUPS), f, math.log(1e-3), math.log(1e-1))
    d['c_b_re'] = nrm((N_ODD, C_GROUPS, C_STATE, C_GROUP), (2 * C_GROUP) ** -0.5)
    d['c_b_im'] = nrm((N_ODD, C_GROUPS, C_STATE, C_GROUP), (2 * C_GROUP) ** -0.5)
    d['c_c_re'] = nrm((N_ODD, C_GROUPS, C_GROUP, C_STATE), (2 * C_STATE) ** -0.5)
    d['c_c_im'] = nrm((N_ODD, C_GROUPS, C_GROUP, C_STATE), (2 * C_STATE) ** -0.5)
    d['c_d'] = nrm((N_ODD, C_WIDTH))
    d['c_w_glu'] = nrm((N_ODD, C_WIDTH, C_WIDTH), C_WIDTH ** -0.5)
    d['d_mu'] = jax.random.uniform(nk(), (N_ODD, D_COLS), f, 0.0, 1.0)
    d['d_w0'] = jnp.tile(jnp.linspace(-6.0, -1.0, D_HEAD, dtype=f), D_HEADS)[None] + nrm((N_ODD, D_WIDTH), 0.1)
    d['d_w_w2'] = nrm((N_ODD, D_LORA_W, D_WIDTH), 0.1)
    d['d_a0'] = nrm((N_ODD, D_WIDTH), 0.1)
    d['d_w_a2'] = nrm((N_ODD, D_LORA_A, D_WIDTH), 0.1)
    d['d_w_g2'] = nrm((N_ODD, D_LORA_G, D_WIDTH), D_LORA_G ** -0.5)
    d['d_k_k'] = 0.85 + nrm((N_ODD, D_WIDTH), 0.05)
    d['d_k_a'] = 1.0 + nrm((N_ODD, D_WIDTH), 0.05)
    d['d_r_k'] = nrm((N_ODD, D_WIDTH), 0.1)
    d['d_ln_g'] = gain((N_ODD, D_WIDTH))
    d['d_ln_b'] = nrm((N_ODD, D_WIDTH), 0.01)
    d['ffn_w1'] = nrm((DEPTH, D_MODEL, FFN_HIDDEN), D_MODEL ** -0.5)
    d['ffn_w3'] = nrm((DEPTH, D_MODEL, FFN_HIDDEN), D_MODEL ** -0.5)
    d['ffn_w2'] = nrm((DEPTH, FFN_HIDDEN, D_MODEL), FFN_HIDDEN ** -0.5)
    return d


def reference(x_prompt, x_sample, cache_a_k, cache_a_v, state_b_c, state_b_n, state_b_m, state_b_conv,
              state_c_re, state_c_im, state_d_s, state_d_shift, meta_tokens, rel_bias, norm_mix, norm_ffn,
              norm_final, ev_w_in, ev_w_out, a_lq1, a_lk1, a_lq2, a_lk2, a_subln, b_conv_w, b_conv_b,
              b_ig_bias, b_fg_bias, b_norm, od_w_in, od_w_out, c_lam_re, c_lam_im, c_log_dt, c_b_re, c_b_im,
              c_c_re, c_c_im, c_d, c_w_glu, d_mu, d_w0, d_w_w2, d_a0, d_w_a2, d_w_g2, d_k_k, d_k_a, d_r_k,
              d_ln_g, d_ln_b, ffn_w1, ffn_w3, ffn_w2):
    p = dict(rel_bias=rel_bias, norm_mix=norm_mix, norm_ffn=norm_ffn, norm_final=norm_final,
             ev_w_in=ev_w_in, ev_w_out=ev_w_out, a_lq1=a_lq1, a_lk1=a_lk1, a_lq2=a_lq2, a_lk2=a_lk2,
             a_subln=a_subln, b_conv_w=b_conv_w, b_conv_b=b_conv_b, b_ig_bias=b_ig_bias, b_fg_bias=b_fg_bias,
             b_norm=b_norm, od_w_in=od_w_in, od_w_out=od_w_out, c_lam_re=c_lam_re, c_lam_im=c_lam_im,
             c_log_dt=c_log_dt, c_b_re=c_b_re, c_b_im=c_b_im, c_c_re=c_c_re, c_c_im=c_c_im, c_d=c_d,
             c_w_glu=c_w_glu, d_mu=d_mu, d_w0=d_w0, d_w_w2=d_w_w2, d_a0=d_a0, d_w_a2=d_w_a2, d_w_g2=d_w_g2,
             d_k_k=d_k_k, d_k_a=d_k_a, d_r_k=d_r_k, d_ln_g=d_ln_g, d_ln_b=d_ln_b,
             ffn_w1=ffn_w1, ffn_w3=ffn_w3, ffn_w2=ffn_w2)
    f = jnp.float32
    i32 = jnp.int32
    bp, sp = x_prompt.shape[0], x_prompt.shape[1]
    meta = jnp.broadcast_to(meta_tokens.astype(x_prompt.dtype)[None], (bp, N_META, D_MODEL))
    xp = jnp.concatenate([meta, x_prompt], axis=1)
    pos_p = jnp.arange(N_META + sp, dtype=i32)
    cid_p = jnp.concatenate([jnp.zeros((N_META,), i32), 1 + jnp.arange(sp, dtype=i32) // CHUNK])
    st_p = dict(b_c=jnp.zeros((N_EVEN, bp, B_HEADS, B_DK, B_DV), f), b_n=jnp.zeros((N_EVEN, bp, B_HEADS, B_DK), f),
                b_m=jnp.zeros((N_EVEN, bp, B_HEADS), f), b_conv=jnp.zeros((N_EVEN, bp, B_CONV - 1, 2 * B_WIDTH), xp.dtype),
                c_re=jnp.zeros((N_ODD, bp, C_GROUPS, C_STATE), f), c_im=jnp.zeros((N_ODD, bp, C_GROUPS, C_STATE), f),
                d_s=jnp.zeros((N_ODD, bp, D_HEADS, D_HEAD, D_HEAD), f), d_shift=jnp.zeros((N_ODD, bp, 1, D_COLS), xp.dtype))
    y_p, ak_p, av_p, bc_p, bn_p, bm_p, bconv_p, cre_p, cim_p, ds_p, dsh_p = trunk(
        xp, pos_p, cid_p, pos_p, cid_p, None, None, st_p, p)
    ts = x_sample.shape[1]
    past_len = cache_a_k.shape[2] - N_META
    new_idx = past_len + jnp.arange(ts, dtype=i32)
    q_pos_s = N_META + new_idx
    q_cid_s = 1 + new_idx // CHUNK
    past_pos = jnp.arange(N_META + past_len, dtype=i32)
    past_cid = jnp.concatenate([jnp.zeros((N_META,), i32), 1 + jnp.arange(past_len, dtype=i32) // CHUNK])
    k_pos_s = jnp.concatenate([past_pos, q_pos_s])
    k_cid_s = jnp.concatenate([past_cid, q_cid_s])
    st_s = dict(b_c=state_b_c, b_n=state_b_n, b_m=state_b_m, b_conv=state_b_conv, c_re=state_c_re,
                c_im=state_c_im, d_s=state_d_s, d_shift=state_d_shift)
    y_s, ak_s, av_s, bc_s, bn_s, bm_s, bconv_s, cre_s, cim_s, ds_s, dsh_s = trunk(
        x_sample, q_pos_s, q_cid_s, k_pos_s, k_cid_s, cache_a_k, cache_a_v, st_s, p)
    return (y_p[:, N_META:], y_s,
            ak_p, av_p, bc_p, bn_p, bm_p, bconv_p, cre_p, cim_p, ds_p, dsh_p,
            ak_s, av_s, bc_s, bn_s, bm_s, bconv_s, cre_s, cim_s, ds_s, dsh_s)
```

```python
import functools
import math

import numpy as np
import jax
import jax.numpy as jnp
from jax import lax
from jax.experimental import pallas as pl
from jax.experimental.pallas import tpu as pltpu

F32 = jnp.float32
BF16 = jnp.bfloat16
I32 = jnp.int32

CHUNK = 64
N_META = 16
EPS = 1e-6
A_HEADS = 4
A_DH = 64
A_DV = 128
N_BUCKETS = 32
MAX_DIST = 128
B_HEADS = 4
B_DK = 128
B_CONV = 4
C_GROUP = 16
C_GROUPS = 32
C_STATE = 64
C_WIDTH = C_GROUP * C_GROUPS
D_HEAD = 64
D_HEADS = 8
D_WIDTH = D_HEAD * D_HEADS
D_GN_EPS = 64e-5
HALF = 512

NEG = -1e30
LANES = 128
VMEM_LIMIT = 56 * 1024 * 1024

ROW_TILE = 512
ATT_BLK = 512
SEQ_BLK = 128
FFN_TH = 1408


def _cparams(sem):
    return pltpu.CompilerParams(dimension_semantics=sem, vmem_limit_bytes=VMEM_LIMIT)


def _dot(a, b):
    return jnp.dot(a, b, preferred_element_type=F32)


def _dot_nt(a, b):
    return lax.dot_general(a, b, (((1,), (1,)), ((), ())), preferred_element_type=F32)


def _dot_tn(a, b):
    return lax.dot_general(a, b, (((0,), (0,)), ((), ())), preferred_element_type=F32)


def _sigmoid(x):
    return 1.0 / (1.0 + jnp.exp(-x))


def _softplus(x):
    return jnp.maximum(x, 0.0) + jnp.log1p(jnp.exp(-jnp.abs(x)))


def _rms(x, g):
    return x * lax.rsqrt(jnp.mean(x * x, axis=-1, keepdims=True) + EPS) * g


def _split3(x):
    hi = x.astype(BF16)
    r1 = x - hi.astype(F32)
    mid = r1.astype(BF16)
    lo = (r1 - mid.astype(F32)).astype(BF16)
    return hi, mid, lo


def _segsum(x, ones):
    hi, mid, lo = _split3(x)
    return _dot(hi, ones) + _dot(mid, ones) + _dot(lo, ones)


def _block_ones(n, seg):
    r = lax.broadcasted_iota(I32, (n, n), 0) // seg
    c = lax.broadcasted_iota(I32, (n, n), 1) // seg
    return jnp.where(r == c, 1.0, 0.0).astype(BF16)


def _rms_mm_kernel(x_ref, g_ref, w_ref, o_ref):
    h = _rms(x_ref[...], g_ref[...])
    o_ref[...] = _dot(h.astype(BF16), w_ref[...])


def rms_matmul(x, g, w, tm=ROW_TILE):
    m, d = x.shape
    tm = min(tm, m)
    n = w.shape[1]
    return pl.pallas_call(
        _rms_mm_kernel,
        grid=(m // tm,),
        in_specs=[pl.BlockSpec((tm, d), lambda i: (i, 0)),
                  pl.BlockSpec((1, d), lambda i: (0, 0)),
                  pl.BlockSpec((d, n), lambda i: (0, 0))],
        out_specs=pl.BlockSpec((tm, n), lambda i: (i, 0)),
        out_shape=jax.ShapeDtypeStruct((m, n), F32),
        compiler_params=_cparams(("parallel",)),
        name="rms_matmul",
    )(x, g, w)


def _outproj_kernel(x_ref, a_ref, b_ref, wa_ref, wb_ref, o_ref):
    acc = _dot(a_ref[...].astype(BF16), wa_ref[...]) + _dot(b_ref[...].astype(BF16), wb_ref[...])
    o_ref[...] = x_ref[...] + acc


def outproj(x, a, b, wa, wb, tm=ROW_TILE):
    m, d = x.shape
    tm = min(tm, m)
    k = a.shape[1]
    row = lambda i: (i, 0)
    fix = lambda i: (0, 0)
    return pl.pallas_call(
        _outproj_kernel,
        grid=(m // tm,),
        in_specs=[pl.BlockSpec((tm, d), row), pl.BlockSpec((tm, k), row), pl.BlockSpec((tm, k), row),
                  pl.BlockSpec((k, d), fix), pl.BlockSpec((k, d), fix)],
        out_specs=pl.BlockSpec((tm, d), row),
        out_shape=jax.ShapeDtypeStruct((m, d), F32),
        compiler_params=_cparams(("parallel",)),
        name="outproj",
    )(x, a, b, wa, wb)


def _ffn_kernel(x_ref, g_ref, gf_ref, w1_ref, w3_ref, w2_ref, o_ref, h_scr, *, final_norm):
    j = pl.program_id(1)

    @pl.when(j == 0)
    def _():
        x = x_ref[...]
        h_scr[...] = _rms(x, g_ref[...]).astype(BF16)
        o_ref[...] = x

    h = h_scr[...]
    a = _dot(h, w1_ref[...])
    b = _dot(h, w3_ref[...])
    u = (a * _sigmoid(a)) * b
    o_ref[...] += _dot(u.astype(BF16), w2_ref[...])

    if final_norm:
        @pl.when(j == pl.num_programs(1) - 1)
        def _():
            o_ref[...] = _rms(o_ref[...], gf_ref[...])


def ffn(x, g, w1, w3, w2, gf=None, tm=ROW_TILE, th=FFN_TH):
    m, d = x.shape
    tm = min(tm, m)
    hid = w1.shape[1]
    final_norm = gf is not None
    if gf is None:
        gf = g
    return pl.pallas_call(
        functools.partial(_ffn_kernel, final_norm=final_norm),
        grid=(m // tm, hid // th),
        in_specs=[pl.BlockSpec((tm, d), lambda i, j: (i, 0)),
                  pl.BlockSpec((1, d), lambda i, j: (0, 0)),
                  pl.BlockSpec((1, d), lambda i, j: (0, 0)),
                  pl.BlockSpec((d, th), lambda i, j: (0, j)),
                  pl.BlockSpec((d, th), lambda i, j: (0, j)),
                  pl.BlockSpec((th, d), lambda i, j: (j, 0))],
        out_specs=pl.BlockSpec((tm, d), lambda i, j: (i, 0)),
        out_shape=jax.ShapeDtypeStruct((m, d), F32),
        scratch_shapes=[pltpu.VMEM((tm, d), BF16)],
        compiler_params=_cparams(("parallel", "arbitrary")),
        name="ffn",
    )(x, g, gf, w1, w3, w2)


def _bias_kernel(rb_ref, o_ref, *, rel0, causal):
    nq, nk = o_ref.shape[1], o_ref.shape[2]
    a = lax.broadcasted_iota(I32, (nq, nk), 0)
    b = lax.broadcasted_iota(I32, (nq, nk), 1)
    rel = b - a + rel0
    nb = N_BUCKETS // 2
    max_exact = nb // 2
    ret = jnp.where(rel > 0, nb, 0)
    n = jnp.abs(rel)
    nf = jnp.maximum(n, 1).astype(F32)
    large = max_exact + (jnp.log(nf / max_exact) / math.log(MAX_DIST / max_exact) * (nb - max_exact)).astype(I32)
    large = jnp.minimum(large, nb - 1)
    bucket = ret + jnp.where(n < max_exact, n, large)
    for h in range(A_HEADS):
        acc = jnp.zeros((nq, nk), F32)
        for bk in range(N_BUCKETS):
            acc = jnp.where(bucket == bk, rb_ref[bk, h], acc)
        if causal:
            acc = jnp.where((b // CHUNK) <= (a // CHUNK), acc, NEG)
        o_ref[h] = acc


def bias_tile(rel_bias, nq, nk, rel0, causal=False):
    return pl.pallas_call(
        functools.partial(_bias_kernel, rel0=rel0, causal=causal),
        in_specs=[pl.BlockSpec(memory_space=pltpu.SMEM)],
        out_specs=pl.BlockSpec(memory_space=pltpu.VMEM),
        out_shape=jax.ShapeDtypeStruct((A_HEADS, nq, nk), F32),
        compiler_params=pltpu.CompilerParams(vmem_limit_bytes=VMEM_LIMIT),
        name="bias_tile",
    )(rel_bias)


def _lambda(lq1_ref, lk1_ref, lq2_ref, lk2_ref, lam_init):
    s1 = jnp.sum(lq1_ref[...] * lk1_ref[...], axis=-1, keepdims=True)
    s2 = jnp.sum(lq2_ref[...] * lk2_ref[...], axis=-1, keepdims=True)
    return jnp.exp(s1) - jnp.exp(s2) + lam_init


def _attn_finish(acc0, l0, acc1, l1, lam, g, lam_init):
    o = acc0 / l0 - lam * (acc1 / l1)
    return _rms(o, g) * (1.0 - lam_init)


def _attn_prompt_kernel(qi_ref, kj_ref, rb_ref, q_ref, k_ref, v_ref, bias_ref,
                        lq1_ref, lk1_ref, lq2_ref, lk2_ref, g_ref, o_ref,
                        m_scr, l_scr, acc_scr, *, blk, n_pad, lam_init):
    s = pl.program_id(0)
    i = qi_ref[s]
    j = kj_ref[s]
    d = i - j

    @pl.when(j == 0)
    def _():
        m_scr[...] = jnp.full(m_scr.shape, NEG, F32)
        l_scr[...] = jnp.zeros(l_scr.shape, F32)
        acc_scr[...] = jnp.zeros(acc_scr.shape, F32)

    kpos = j * blk + lax.broadcasted_iota(I32, (1, blk), 1)
    pen = jnp.where(kpos < n_pad, NEG, 0.0)

    def update(bias_fn):
        for h in range(A_HEADS):
            vb = v_ref[:, h * A_DV:(h + 1) * A_DV].astype(BF16)
            bias = bias_fn(h)
            for mm in range(2):
                idx = 2 * h + mm
                c0 = h * 2 * A_DH + mm * A_DH
                qb = (q_ref[:, c0:c0 + A_DH] * (A_DH ** -0.5)).astype(BF16)
                kb = k_ref[:, c0:c0 + A_DH].astype(BF16)
                sc = _dot_nt(qb, kb) + bias
                m_old = m_scr[idx]
                m_new = jnp.maximum(m_old, jnp.max(sc, axis=-1, keepdims=True))
                alpha = jnp.exp(m_old - m_new)
                p = jnp.exp(sc - m_new)
                l_scr[idx] = alpha * l_scr[idx] + jnp.sum(p, axis=-1, keepdims=True)
                acc_scr[idx] = alpha * acc_scr[idx] + _dot(p.astype(BF16), vb)
                m_scr[idx] = m_new

    @pl.when(d >= 2)
    def _():
        update(lambda h: rb_ref[N_BUCKETS // 2 - 1, h] + pen)

    @pl.when(d < 2)
    def _():
        update(lambda h: bias_ref[d, h] + pen)

    @pl.when(d == 0)
    def _():
        lam = _lambda(lq1_ref, lk1_ref, lq2_ref, lk2_ref, lam_init)
        rows = i * blk + lax.broadcasted_iota(I32, (blk, 1), 0)
        valid = rows >= n_pad
        for h in range(A_HEADS):
            y = _attn_finish(acc_scr[2 * h], l_scr[2 * h], acc_scr[2 * h + 1], l_scr[2 * h + 1],
                             lam, g_ref[...], lam_init)
            o_ref[:, h * A_DV:(h + 1) * A_DV] = jnp.where(valid, y, 0.0)


def attn_prompt(proj, bias2, rel_bias, lq1, lk1, lq2, lk2, g, *, n_pad, lam_init, blk=ATT_BLK):
    t = proj.shape[0]
    nb = t // blk
    qi = np.array([i for i in range(nb) for _ in range(i + 1)], np.int32)
    kj = np.array([j for i in range(nb) for j in range(i + 1)], np.int32)
    vec = lambda n: pl.BlockSpec((1, n), lambda s, qi, kj: (0, 0))
    grid_spec = pltpu.PrefetchScalarGridSpec(
        num_scalar_prefetch=2,
        grid=(len(qi),),
        in_specs=[pl.BlockSpec(memory_space=pltpu.SMEM),
                  pl.BlockSpec((blk, HALF), lambda s, qi, kj: (qi[s], 0)),
                  pl.BlockSpec((blk, HALF), lambda s, qi, kj: (kj[s], 1)),
                  pl.BlockSpec((blk, HALF), lambda s, qi, kj: (kj[s], 2)),
                  pl.BlockSpec((2, A_HEADS, blk, blk), lambda s, qi, kj: (0, 0, 0, 0)),
                  vec(A_DH), vec(A_DH), vec(A_DH), vec(A_DH), vec(A_DV)],
        out_specs=pl.BlockSpec((blk, HALF), lambda s, qi, kj: (qi[s], 0)),
        scratch_shapes=[pltpu.VMEM((2 * A_HEADS, blk, 1), F32),
                        pltpu.VMEM((2 * A_HEADS, blk, 1), F32),
                        pltpu.VMEM((2 * A_HEADS, blk, A_DV), F32)],
    )
    return pl.pallas_call(
        functools.partial(_attn_prompt_kernel, blk=blk, n_pad=n_pad, lam_init=lam_init),
        grid_spec=grid_spec,
        out_shape=jax.ShapeDtypeStruct((t, HALF), F32),
        compiler_params=_cparams(("arbitrary",)),
        name="attn_prompt",
    )(jnp.asarray(qi), jnp.asarray(kj), rel_bias, proj, proj, proj, bias2, lq1, lk1, lq2, lk2, g)


def _attn_sample_kernel(q_ref, kn_ref, vn_ref, kc_ref, vc_ref, bp_ref, bn_ref,
                        lq1_ref, lk1_ref, lq2_ref, lk2_ref, g_ref, o_ref, *, lam_init):
    lam = _lambda(lq1_ref, lk1_ref, lq2_ref, lk2_ref, lam_init)
    for h in range(A_HEADS):
        vsl = slice(h * A_DV, (h + 1) * A_DV)
        vp = vc_ref[0, :, vsl].astype(BF16)
        vn = vn_ref[:, vsl].astype(BF16)
        outs = []
        for mm in range(2):
            c0 = h * 2 * A_DH + mm * A_DH
            qb = (q_ref[:, c0:c0 + A_DH] * (A_DH ** -0.5)).astype(BF16)
            sp = _dot_nt(qb, kc_ref[0, :, c0:c0 + A_DH].astype(BF16)) + bp_ref[h]
            sn = _dot_nt(qb, kn_ref[:, c0:c0 + A_DH].astype(BF16)) + bn_ref[h]
            mx = jnp.maximum(jnp.max(sp, axis=-1, keepdims=True), jnp.max(sn, axis=-1, keepdims=True))
            pp = jnp.exp(sp - mx)
            pn = jnp.exp(sn - mx)
            l = jnp.sum(pp, axis=-1, keepdims=True) + jnp.sum(pn, axis=-1, keepdims=True)
            acc = _dot(pp.astype(BF16), vp) + _dot(pn.astype(BF16), vn)
            outs.append((acc, l))
        o_ref[:, vsl] = _attn_finish(outs[0][0], outs[0][1], outs[1][0], outs[1][1], lam, g_ref[...], lam_init)


def attn_sample(proj, kc, vc, bias_past, bias_new, lq1, lk1, lq2, lk2, g, *, ts, lam_init):
    bsz, tk, _ = kc.shape
    vec = lambda n: pl.BlockSpec((1, n), lambda b: (0, 0))
    return pl.pallas_call(
        functools.partial(_attn_sample_kernel, lam_init=lam_init),
        grid=(bsz,),
        in_specs=[pl.BlockSpec((ts, HALF), lambda b: (b, 0)),
                  pl.BlockSpec((ts, HALF), lambda b: (b, 1)),
                  pl.BlockSpec((ts, HALF), lambda b: (b, 2)),
                  pl.BlockSpec((1, tk, HALF), lambda b: (b, 0, 0)),
                  pl.BlockSpec((1, tk, HALF), lambda b: (b, 0, 0)),
                  pl.BlockSpec((A_HEADS, ts, tk), lambda b: (0, 0, 0)),
                  pl.BlockSpec((A_HEADS, ts, ts), lambda b: (0, 0, 0)),
                  vec(A_DH), vec(A_DH), vec(A_DH), vec(A_DH), vec(A_DV)],
        out_specs=pl.BlockSpec((ts, HALF), lambda b: (b, 0)),
        out_shape=jax.ShapeDtypeStruct((bsz * ts, HALF), F32),
        compiler_params=_cparams(("parallel",)),
        name="attn_sample",
    )(proj, proj, proj, kc, vc, bias_past, bias_new, lq1, lk1, lq2, lk2, g)


def _mlstm_kernel(bq_ref, bk_ref, bv_ref, bo_ref, gc_ref, gr_ref, c0_ref, n0_ref, m0_ref, conv0_ref,
                  cw_ref, cb_ref, gbc_ref, gbr_ref, bn_ref,
                  h_ref, c_out, n_out, m_out,
                  xbuf, c_scr, n_scr, m_scr, *, ln, n_pad):
    tb = pl.program_id(1)
    nt = pl.num_programs(1)
    halo = B_CONV - 1
    base = 8 - halo

    @pl.when(tb == 0)
    def _():
        xbuf[base:8, :] = conv0_ref[0]
        c_scr[...] = c0_ref[0]
        n_scr[...] = n0_ref[0]
        for h in range(B_HEADS):
            m_scr[h] = m0_ref[0, :, h:h + 1]

    xbuf[8:8 + ln, 0:HALF] = bq_ref[...]
    xbuf[8:8 + ln, HALF:2 * HALF] = bk_ref[...]
    conv = cb_ref[...] + cw_ref[0:1, :] * xbuf[base:base + ln, :]
    for jj in range(1, B_CONV):
        conv = conv + cw_ref[jj:jj + 1, :] * xbuf[base + jj:base + jj + ln, :]
    xbuf[base:8, :] = xbuf[8 + ln - halo:8 + ln, :]
    conv = conv * _sigmoid(conv)
    q_all = conv[:, 0:HALF] * (B_DK ** -0.5)
    k_all = conv[:, HALF:2 * HALF]

    rows = tb * ln + lax.broadcasted_iota(I32, (ln, 1), 0)
    cols = tb * ln + lax.broadcasted_iota(I32, (1, ln), 1)
    valid_c = rows >= n_pad
    valid_r = cols >= n_pad

    gc = gc_ref[...] + gbc_ref[...]
    gr = gr_ref[0] + gbr_ref[...]
    li_c = jnp.where(valid_c, gc, NEG)
    li_r = jnp.where(valid_r, gr, NEG)
    lf_c = jnp.where(valid_c, -_softplus(-gc), 0.0)
    lf_r = jnp.where(valid_r, -_softplus(-gr), 0.0)

    ri = lax.broadcasted_iota(I32, (ln, ln), 0)
    ci = lax.broadcasted_iota(I32, (ln, ln), 1)
    tril = ri >= ci
    tril_f = jnp.where(tril, 1.0, 0.0)
    triu_f = jnp.where(ri <= ci, 1.0, 0.0)
    b_c = jnp.dot(tril_f, lf_c, preferred_element_type=F32, precision=lax.Precision.HIGHEST)
    b_r = jnp.dot(lf_r, triu_f, preferred_element_type=F32, precision=lax.Precision.HIGHEST)

    for h in range(B_HEADS):
        sl = slice(h * B_DK, (h + 1) * B_DK)
        qh = q_all[:, sl].astype(BF16)
        kh = k_all[:, sl]
        vh = bv_ref[:, sl].astype(BF16)
        c = c_scr[h]
        n = n_scr[h]
        m = m_scr[h]
        bc = b_c[:, B_HEADS + h:B_HEADS + h + 1]
        br = b_r[B_HEADS + h:B_HEADS + h + 1, :]
        inter = bc + m
        dmat = jnp.where(tril, bc - br + li_r[h:h + 1, :], NEG)
        mt = jnp.maximum(inter, jnp.max(dmat, axis=-1, keepdims=True))
        w_inter = jnp.exp(inter - mt)
        s = _dot_nt(qh, kh.astype(BF16)) * jnp.exp(dmat - mt)
        num = w_inter * _dot(qh, c.astype(BF16)) + _dot(s.astype(BF16), vh)
        qn = jnp.sum(qh.astype(F32) * n, axis=-1, keepdims=True)
        den = w_inter * qn + jnp.sum(s, axis=-1, keepdims=True)
        hh = num / jnp.maximum(jnp.abs(den), jnp.exp(-mt))
        b_last = bc[ln - 1:ln, :]
        g = b_last - bc + li_c[:, h:h + 1]
        m_new = jnp.maximum(b_last + m, jnp.max(g, axis=0, keepdims=True))
        decay = jnp.exp(b_last + m - m_new)
        wk = (jnp.exp(g - m_new) * kh)
        c_scr[h] = decay * c + _dot_tn(wk.astype(BF16), vh)
        n_scr[h] = decay * n + jnp.sum(wk, axis=0, keepdims=True)
        m_scr[h] = m_new
        hn = _rms(hh, bn_ref[:, sl]) * _sigmoid(bo_ref[:, sl])
        h_ref[:, sl] = jnp.where(valid_c, hn, 0.0)

    @pl.when(tb == nt - 1)
    def _():
        c_out[0] = c_scr[...]
        n_out[0] = n_scr[...]
        lane = lax.broadcasted_iota(I32, (1, B_HEADS), 1)
        mrow = jnp.zeros((1, B_HEADS), F32)
        for h in range(B_HEADS):
            mrow = jnp.where(lane == h, m_scr[h], mrow)
        m_out[0] = mrow


def mlstm(proj, gates_r, c0, n0, m0, conv0, cw, cb, igb, fgb, bnorm, *, bsz, t, ln, n_pad):
    nt = t // ln
    gb = jnp.concatenate([igb, fgb])
    gw = proj.shape[1] // LANES - 1
    row = lambda c: (lambda b, i: (b * nt + i, c))
    fix2 = lambda b, i: (0, 0)
    out_shapes = (jax.ShapeDtypeStruct((bsz * t, HALF), F32),
                  jax.ShapeDtypeStruct((bsz, B_HEADS, B_DK, B_DK), F32),
                  jax.ShapeDtypeStruct((bsz, B_HEADS, 1, B_DK), F32),
                  jax.ShapeDtypeStruct((bsz, 1, B_HEADS), F32))
    return pl.pallas_call(
        functools.partial(_mlstm_kernel, ln=ln, n_pad=n_pad),
        grid=(bsz, nt),
        in_specs=[pl.BlockSpec((ln, HALF), row(3)), pl.BlockSpec((ln, HALF), row(4)),
                  pl.BlockSpec((ln, HALF), row(5)), pl.BlockSpec((ln, HALF), row(6)),
                  pl.BlockSpec((ln, LANES), row(gw)),
                  pl.BlockSpec((1, 2 * B_HEADS, ln), lambda b, i: (b * nt + i, 0, 0)),
                  pl.BlockSpec((1, B_HEADS, B_DK, B_DK), lambda b, i: (b, 0, 0, 0)),
                  pl.BlockSpec((1, B_HEADS, 1, B_DK), lambda b, i: (b, 0, 0, 0)),
                  pl.BlockSpec((1, 1, B_HEADS), lambda b, i: (b, 0, 0)),
                  pl.BlockSpec((1, B_CONV - 1, 2 * HALF), lambda b, i: (b, 0, 0)),
                  pl.BlockSpec((B_CONV, 2 * HALF), fix2), pl.BlockSpec((1, 2 * HALF), fix2),
                  pl.BlockSpec((1, LANES), fix2), pl.BlockSpec((2 * B_HEADS, 1), fix2),
                  pl.BlockSpec((1, HALF), fix2)],
        out_specs=(pl.BlockSpec((ln, HALF), lambda b, i: (b * nt + i, 0)),
                   pl.BlockSpec((1, B_HEADS, B_DK, B_DK), lambda b, i: (b, 0, 0, 0)),
                   pl.BlockSpec((1, B_HEADS, 1, B_DK), lambda b, i: (b, 0, 0, 0)),
                   pl.BlockSpec((1, 1, B_HEADS), lambda b, i: (b, 0, 0))),
        out_shape=out_shapes,
        scratch_shapes=[pltpu.VMEM((ln + 8, 2 * HALF), F32),
                        pltpu.VMEM((B_HEADS, B_DK, B_DK), F32),
                        pltpu.VMEM((B_HEADS, 1, B_DK), F32),
                        pltpu.VMEM((B_HEADS, 1, 1), F32)],
        compiler_params=_cparams(("parallel", "arbitrary")),
        name="mlstm",
    )(proj, proj, proj, proj, proj, gates_r, c0, n0, m0, conv0, cw, cb,
      jnp.pad(gb, (0, LANES - 2 * B_HEADS))[None], gb[:, None], bnorm)


def _s5_param_kernel(lr_ref, li_ref, ldt_ref, bre_ref, bim_ref, ar_ref, ai_ref, bbr_ref, bbi_ref):
    lr = lr_ref[...]
    li = li_ref[...]
    dt = jnp.exp(ldt_ref[...])
    mag = jnp.exp(lr * dt)
    ar = mag * jnp.cos(li * dt)
    ai = mag * jnp.sin(li * dt)
    den = lr * lr + li * li
    cr = ((ar - 1.0) * lr + ai * li) / den
    ci = (ai * lr - (ar - 1.0) * li) / den
    ar_ref[...] = ar
    ai_ref[...] = ai
    nr, nc = bre_ref.shape
    same = (lax.broadcasted_iota(I32, (nr, nc), 0) // C_GROUP) == (lax.broadcasted_iota(I32, (nr, nc), 1) // C_STATE)
    br = bre_ref[...]
    bi = bim_ref[...]
    bbr_ref[...] = jnp.where(same, cr * br - ci * bi, 0.0).astype(BF16)
    bbi_ref[...] = jnp.where(same, cr * bi + ci * br, 0.0).astype(BF16)


def s5_params(lr, li, ldt, bre_rep, bim_rep):
    ns = lr.shape[1]
    vm = pl.BlockSpec(memory_space=pltpu.VMEM)
    return pl.pallas_call(
        _s5_param_kernel,
        in_specs=[vm] * 5,
        out_specs=(vm, vm, vm, vm),
        out_shape=(jax.ShapeDtypeStruct((1, ns), F32), jax.ShapeDtypeStruct((1, ns), F32),
                   jax.ShapeDtypeStruct(bre_rep.shape, BF16), jax.ShapeDtypeStruct(bre_rep.shape, BF16)),
        compiler_params=pltpu.CompilerParams(vmem_limit_bytes=VMEM_LIMIT),
        name="s5_params",
    )(lr, li, ldt, bre_rep, bim_rep)


def _s5_kernel(u_ref, ar_ref, ai_ref, bbr_ref, bbi_ref, cre_ref, cim_ref, d_ref, wg_ref, x0r_ref, x0i_ref,
               o_ref, xr_out, xi_out, sr_scr, si_scr, bur, bui, xra, xia, *, tb_len, n_pad):
    tb = pl.program_id(1)
    nt = pl.num_programs(1)

    @pl.when(tb == 0)
    def _():
        sr_scr[...] = x0r_ref[0]
        si_scr[...] = x0i_ref[0]

    u = u_ref[...]
    ub = u.astype(BF16)
    bur[...] = _dot(ub, bbr_ref[...])
    bui[...] = _dot(ub, bbi_ref[...])
    ar = ar_ref[...]
    ai = ai_ref[...]

    sub = lax.broadcasted_iota(I32, (8, ar.shape[1]), 0)

    def body(grp, carry):
        xr, xi = carry
        base = pl.multiple_of(grp * 8, 8)
        br8 = bur[pl.ds(base, 8), :]
        bi8 = bui[pl.ds(base, 8), :]
        xr8 = jnp.zeros(br8.shape, F32)
        xi8 = jnp.zeros(br8.shape, F32)
        for i in range(8):
            nxr = ar * xr - ai * xi + br8[i:i + 1, :]
            nxi = ar * xi + ai * xr + bi8[i:i + 1, :]
            xr, xi = nxr, nxi
            xr8 = jnp.where(sub == i, xr, xr8)
            xi8 = jnp.where(sub == i, xi, xi8)
        xra[pl.ds(base, 8), :] = xr8
        xia[pl.ds(base, 8), :] = xi8
        return xr, xi

    xr, xi = lax.fori_loop(0, tb_len // 8, body, (sr_scr[...], si_scr[...]))
    sr_scr[...] = xr
    si_scr[...] = xi

    y = _dot(xra[...].astype(BF16), cre_ref[...]) - _dot(xia[...].astype(BF16), cim_ref[...]) + d_ref[...] * u
    yg = 0.5 * y * (1.0 + jnp.tanh(math.sqrt(2.0 / math.pi) * (y + 0.044715 * (y * y * y))))
    oc = yg * _sigmoid(_dot(yg.astype(BF16), wg_ref[...]))
    rows = tb * tb_len + lax.broadcasted_iota(I32, (tb_len, 1), 0)
    o_ref[...] = jnp.where(rows >= n_pad, oc, 0.0)

    @pl.when(tb == nt - 1)
    def _():
        xr_out[0] = xr
        xi_out[0] = xi


def s5(proj, ar, ai, bbr, bbi, cre, cim, dskip, wglu, x0r, x0i, *, bsz, t, tb_len, n_pad):
    nt = t // tb_len
    ns = ar.shape[1]
    fix2 = lambda b, i: (0, 0)
    st = pl.BlockSpec((1, 1, ns), lambda b, i: (b, 0, 0))
    return pl.pallas_call(
        functools.partial(_s5_kernel, tb_len=tb_len, n_pad=n_pad),
        grid=(bsz, nt),
        in_specs=[pl.BlockSpec((tb_len, HALF), lambda b, i: (b * nt + i, 0)),
                  pl.BlockSpec((1, ns), fix2), pl.BlockSpec((1, ns), fix2),
                  pl.BlockSpec((HALF, ns), fix2), pl.BlockSpec((HALF, ns), fix2),
                  pl.BlockSpec((ns, HALF), fix2), pl.BlockSpec((ns, HALF), fix2),
                  pl.BlockSpec((1, HALF), fix2), pl.BlockSpec((HALF, HALF), fix2), st, st],
        out_specs=(pl.BlockSpec((tb_len, HALF), lambda b, i: (b * nt + i, 0)), st, st),
        out_shape=(jax.ShapeDtypeStruct((bsz * t, HALF), F32),
                   jax.ShapeDtypeStruct((bsz, 1, ns), F32), jax.ShapeDtypeStruct((bsz, 1, ns), F32)),
        scratch_shapes=[pltpu.VMEM((1, ns), F32), pltpu.VMEM((1, ns), F32),
                        pltpu.VMEM((tb_len, ns), F32), pltpu.VMEM((tb_len, ns), F32),
                        pltpu.VMEM((tb_len, ns), F32), pltpu.VMEM((tb_len, ns), F32)],
        compiler_params=_cparams(("parallel", "arbitrary")),
        name="s5",
    )(proj, ar, ai, bbr, bbi, cre, cim, dskip, wglu, x0r, x0i)


def _rwkv_kernel(r_ref, k_ref, v_ref, lo_ref, sh0_ref, s0_ref,
                 mu_ref, w0_ref, ww2_ref, a0_ref, wa2_ref, wg2_ref, kk_ref, ka_ref, rk_ref, lng_ref, lnb_ref,
                 o_ref, s_out,
                 xbuf, s_scr, w_scr, kk_scr, b_scr, k_scr, r_scr, v_scr, y_scr, *, tb_len, n_pad):
    tb = pl.program_id(1)
    nt = pl.num_programs(1)
    npair = D_HEADS // 2
    ncol = xbuf.shape[1]

    @pl.when(tb == 0)
    def _():
        xbuf[7:8, :] = sh0_ref[0]
        s_scr[...] = s0_ref[0]

    xbuf[8:8 + tb_len, 0:D_WIDTH] = r_ref[...]
    xbuf[8:8 + tb_len, D_WIDTH:2 * D_WIDTH] = k_ref[...]
    xbuf[8:8 + tb_len, 2 * D_WIDTH:3 * D_WIDTH] = v_ref[...]
    xbuf[8:8 + tb_len, 3 * D_WIDTH:ncol] = lo_ref[...]
    cur = xbuf[8:8 + tb_len, :]
    prev = xbuf[7:7 + tb_len, :]
    xbuf[7:8, :] = xbuf[7 + tb_len:8 + tb_len, :]
    xm = cur + mu_ref[...] * (prev - cur)
    r = xm[:, 0:D_WIDTH]
    k = xm[:, D_WIDTH:2 * D_WIDTH]
    v = xm[:, 2 * D_WIDTH:3 * D_WIDTH]
    c0 = 3 * D_WIDTH
    wlo = xm[:, c0:c0 + 64]
    alo = xm[:, c0 + 64:c0 + 128]
    glo = xm[:, c0 + 128:c0 + 256]

    w_raw = w0_ref[...] + _dot(jnp.tanh(wlo).astype(BF16), ww2_ref[...])
    decay = jnp.exp(-jnp.exp(-_softplus(-w_raw) - 0.5))
    a = _sigmoid(a0_ref[...] + _dot(alo.astype(BF16), wa2_ref[...]))
    g = _dot(_sigmoid(glo).astype(BF16), wg2_ref[...])

    ones_h = _block_ones(D_WIDTH, D_HEAD)
    kk = k * kk_ref[...]
    kk = kk / jnp.maximum(jnp.sqrt(_segsum(kk * kk, ones_h)), 1e-12)
    k2 = k * (1.0 + (a - 1.0) * ka_ref[...])

    w_scr[...] = decay
    kk_scr[...] = kk
    b_scr[...] = kk * a
    k_scr[...] = k2
    r_scr[...] = r
    v_scr[...] = v

    ones_p = _block_ones(LANES, D_HEAD)
    eye2 = jnp.where((lax.broadcasted_iota(I32, (D_HEAD, LANES), 1) % D_HEAD)
                     == lax.broadcasted_iota(I32, (D_HEAD, LANES), 0), 1.0, 0.0)

    sub = lax.broadcasted_iota(I32, (8, LANES), 0)

    def body(grp, state):
        base = pl.multiple_of(grp * 8, 8)
        new = []
        for p in range(npair):
            sl = slice(p * LANES, (p + 1) * LANES)
            sp = state[p]
            kk8 = kk_scr[pl.ds(base, 8), sl]
            v8 = v_scr[pl.ds(base, 8), sl]
            w8 = w_scr[pl.ds(base, 8), sl]
            b8 = b_scr[pl.ds(base, 8), sl]
            k8 = k_scr[pl.ds(base, 8), sl]
            r8 = r_scr[pl.ds(base, 8), sl]
            y8 = jnp.zeros((8, LANES), F32)
            for i in range(8):
                sa = _segsum(sp * (-kk8[i:i + 1, :]), ones_p)
                vcol = _segsum(v8[i:i + 1, :] * eye2, ones_p)
                sp = sp * w8[i:i + 1, :] + sa * b8[i:i + 1, :] + vcol * k8[i:i + 1, :]
                ycol = _segsum(sp * r8[i:i + 1, :], ones_p)
                y8 = jnp.where(sub == i, jnp.sum(ycol * eye2, axis=0, keepdims=True), y8)
            y_scr[pl.ds(base, 8), sl] = y8
            new.append(sp)
        return tuple(new)

    state = lax.fori_loop(0, tb_len // 8, body, tuple(s_scr[p] for p in range(npair)))
    for p in range(npair):
        s_scr[p] = state[p]

    y = y_scr[...]
    inv = 1.0 / D_HEAD
    mean = _segsum(y, ones_h) * inv
    yc = y - mean
    var = _segsum(yc * yc, ones_h) * inv
    y = yc * lax.rsqrt(var + D_GN_EPS) * lng_ref[...] + lnb_ref[...]
    y = y + _segsum(r * k2 * rk_ref[...], ones_h) * v
    rows = tb * tb_len + lax.broadcasted_iota(I32, (tb_len, 1), 0)
    o_ref[...] = jnp.where(rows >= n_pad, y * g, 0.0)

    @pl.when(tb == nt - 1)
    def _():
        s_out[0] = s_scr[...]


def rwkv(proj, sh0, s0, mu, w0, ww2, a0, wa2, wg2, kkp, kap, rkp, lng, lnb, *, bsz, t, tb_len, n_pad):
    nt = t // tb_len
    npair = D_HEADS // 2
    ncol = mu.shape[1]
    nlo = ncol - 3 * D_WIDTH
    row = lambda c: (lambda b, i: (b * nt + i, c))
    fix2 = lambda b, i: (0, 0)
    vec = pl.BlockSpec((1, D_WIDTH), fix2)
    st = pl.BlockSpec((1, npair, D_HEAD, LANES), lambda b, i: (b, 0, 0, 0))
    big = lambda: pltpu.VMEM((tb_len, D_WIDTH), F32)
    return pl.pallas_call(
        functools.partial(_rwkv_kernel, tb_len=tb_len, n_pad=n_pad),
        grid=(bsz, nt),
        in_specs=[pl.BlockSpec((tb_len, D_WIDTH), row(1)), pl.BlockSpec((tb_len, D_WIDTH), row(2)),
                  pl.BlockSpec((tb_len, D_WIDTH), row(3)),
                  pl.BlockSpec((tb_len, nlo), row(4 * D_WIDTH // nlo)),
                  pl.BlockSpec((1, 1, ncol), lambda b, i: (b, 0, 0)), st,
                  pl.BlockSpec((1, ncol), fix2), vec,
                  pl.BlockSpec(ww2.shape, fix2), vec, pl.BlockSpec(wa2.shape, fix2), pl.BlockSpec(wg2.shape, fix2),
                  vec, vec, vec, vec, vec],
        out_specs=(pl.BlockSpec((tb_len, D_WIDTH), lambda b, i: (b * nt + i, 0)), st),
        out_shape=(jax.ShapeDtypeStruct((bsz * t, D_WIDTH), F32),
                   jax.ShapeDtypeStruct((bsz, npair, D_HEAD, LANES), F32)),
        scratch_shapes=[pltpu.VMEM((tb_len + 8, ncol), F32), pltpu.VMEM((npair, D_HEAD, LANES), F32),
                        big(), big(), big(), big(), big(), big(), big()],
        compiler_params=_cparams(("parallel", "arbitrary")),
        name="rwkv7",
    )(proj, proj, proj, proj, sh0, s0, mu, w0, ww2, a0, wa2, wg2, kkp, kap, rkp, lng, lnb)


def _pad_cols(w, mult=LANES):
    n = w.shape[1]
    return jnp.pad(w, ((0, 0), (0, (-n) % mult)))


def _pairs_from_heads(s):
    b = s.shape[0]
    return s.reshape(b, D_HEADS // 2, 2, D_HEAD, D_HEAD).transpose(0, 1, 3, 2, 4).reshape(b, D_HEADS // 2, D_HEAD, LANES)


def _heads_from_pairs(s):
    b = s.shape[0]
    return s.reshape(b, D_HEADS // 2, D_HEAD, 2, D_HEAD).transpose(0, 1, 3, 2, 4).reshape(b, D_HEADS, D_HEAD, D_HEAD)


def _trunk(x, p, *, bsz, t, seq_blk, n_pad, attn_fn, st):
    depth = p['norm_mix'].shape[0]
    new = {k: [] for k in ('a_k', 'a_v', 'b_c', 'b_n', 'b_m', 'b_conv', 'c_re', 'c_im', 'd_s', 'd_shift')}
    nt = t // seq_blk
    y = None
    for layer in range(depth):
        g_mix = p['norm_mix'][layer][None]
        if layer % 2 == 0:
            e = layer // 2
            proj = rms_matmul(x, g_mix, p['ev_w_in'][e])
            lam_init = 0.8 - 0.6 * math.exp(-0.3 * layer)
            oa = attn_fn(proj, e, lam_init)
            gcols = proj[:, 7 * HALF:7 * HALF + 2 * B_HEADS]
            gates_r = gcols.reshape(bsz * nt, seq_blk, 2 * B_HEADS).transpose(0, 2, 1)
            hb, bc, bn, bm = mlstm(proj, gates_r, st['b_c'][e], st['b_n'][e][:, :, None, :], st['b_m'][e][:, None, :],
                                   st['b_conv'][e], p['b_conv_w'][e], p['b_conv_b'][e][None],
                                   p['b_ig_bias'][e], p['b_fg_bias'][e], p['b_norm'][e][None],
                                   bsz=bsz, t=t, ln=seq_blk, n_pad=n_pad)
            p3 = proj.reshape(bsz, t, -1)
            new['a_k'].append(p3[:, :, HALF:2 * HALF])
            new['a_v'].append(p3[:, :, 2 * HALF:3 * HALF])
            new['b_c'].append(bc)
            new['b_n'].append(bn[:, :, 0, :])
            new['b_m'].append(bm[:, 0, :])
            new['b_conv'].append(p3[:, t - (B_CONV - 1):, 3 * HALF:5 * HALF])
            x = outproj(x, oa, hb, p['ev_w_out'][e][:HALF], p['ev_w_out'][e][HALF:])
        else:
            o = layer // 2
            proj = rms_matmul(x, g_mix, p['od_w_in'][o])
            sp = p['s5'][o]
            oc, cre, cim = s5(proj, sp['ar'], sp['ai'], sp['bbr'], sp['bbi'], sp['cre'], sp['cim'],
                              p['c_d'][o][None], p['c_w_glu'][o],
                              st['c_re'][o].reshape(bsz, 1, -1), st['c_im'][o].reshape(bsz, 1, -1),
                              bsz=bsz, t=t, tb_len=seq_blk, n_pad=n_pad)
            od, ds = rwkv(proj, st['d_shift'][o], _pairs_from_heads(st['d_s'][o]),
                          p['d_mu'][o][None], p['d_w0'][o][None], p['d_w_w2'][o], p['d_a0'][o][None],
                          p['d_w_a2'][o], p['d_w_g2'][o], p['d_k_k'][o][None], p['d_k_a'][o][None],
                          p['d_r_k'][o][None], p['d_ln_g'][o][None], p['d_ln_b'][o][None],
                          bsz=bsz, t=t, tb_len=seq_blk, n_pad=n_pad)
            new['c_re'].append(cre.reshape(bsz, C_GROUPS, C_STATE))
            new['c_im'].append(cim.reshape(bsz, C_GROUPS, C_STATE))
            new['d_s'].append(_heads_from_pairs(ds))
            new['d_shift'].append(proj.reshape(bsz, t, -1)[:, -1:, HALF:])
            x = outproj(x, oc, od, p['od_w_out'][o][:HALF], p['od_w_out'][o][HALF:])
        gf = p['norm_final'][None] if layer == depth - 1 else None
        x = ffn(x, p['norm_ffn'][layer][None], p['ffn_w1'][layer], p['ffn_w3'][layer], p['ffn_w2'][layer], gf=gf)
    return x, new


def kernel(x_prompt, x_sample, cache_a_k, cache_a_v, state_b_c, state_b_n, state_b_m, state_b_conv, state_c_re, state_c_im, state_d_s, state_d_shift, meta_tokens, rel_bias, norm_mix, norm_ffn, norm_final, ev_w_in, ev_w_out, a_lq1, a_lk1, a_lq2, a_lk2, a_subln, b_conv_w, b_conv_b, b_ig_bias, b_fg_bias, b_norm, od_w_in, od_w_out, c_lam_re, c_lam_im, c_log_dt, c_b_re, c_b_im, c_c_re, c_c_im, c_d, c_w_glu, d_mu, d_w0, d_w_w2, d_a0, d_w_a2, d_w_g2, d_k_k, d_k_a, d_r_k, d_ln_g, d_ln_b, ffn_w1, ffn_w3, ffn_w2):
    bp, sp_len, dm = x_prompt.shape
    bs, ts, _ = x_sample.shape
    n_even, n_odd = ev_w_in.shape[0], od_w_in.shape[0]
    assert bp == 1 and sp_len % CHUNK == 0 and ts % 8 == 0 and ts >= B_CONV - 1
    dt = x_prompt.dtype

    ns = C_GROUPS * C_STATE
    s5p = []
    for o in range(n_odd):
        bre = jnp.tile(c_b_re[o].transpose(2, 0, 1).reshape(C_GROUP, ns), (C_GROUPS, 1))
        bim = jnp.tile(c_b_im[o].transpose(2, 0, 1).reshape(C_GROUP, ns), (C_GROUPS, 1))
        ar, ai, bbr, bbi = s5_params(c_lam_re[o].reshape(1, ns), c_lam_im[o].reshape(1, ns),
                                     jnp.repeat(c_log_dt[o], C_STATE)[None], bre, bim)
        eye = jnp.eye(C_GROUPS, dtype=F32)
        blk = lambda c: (eye[:, None, :, None] * c.transpose(0, 2, 1)[:, :, None, :]).reshape(ns, C_WIDTH).astype(BF16)
        s5p.append(dict(ar=ar, ai=ai, bbr=bbr, bbi=bbi, cre=blk(c_c_re[o]), cim=blk(c_c_im[o])))
    p = dict(norm_mix=norm_mix, norm_ffn=norm_ffn, norm_final=norm_final,
             ev_w_in=[_pad_cols(ev_w_in[e]).astype(BF16) for e in range(n_even)],
             ev_w_out=ev_w_out.astype(BF16),
             od_w_in=[od_w_in[o].astype(BF16) for o in range(n_odd)], od_w_out=od_w_out.astype(BF16),
             b_conv_w=b_conv_w, b_conv_b=b_conv_b, b_ig_bias=b_ig_bias, b_fg_bias=b_fg_bias, b_norm=b_norm,
             s5=s5p, c_d=c_d, c_w_glu=c_w_glu.astype(BF16),
             d_mu=d_mu, d_w0=d_w0, d_w_w2=d_w_w2.astype(BF16), d_a0=d_a0, d_w_a2=d_w_a2.astype(BF16),
             d_w_g2=d_w_g2.astype(BF16), d_k_k=d_k_k, d_k_a=d_k_a, d_r_k=d_r_k, d_ln_g=d_ln_g, d_ln_b=d_ln_b,
             ffn_w1=ffn_w1.astype(BF16), ffn_w3=ffn_w3.astype(BF16), ffn_w2=ffn_w2.astype(BF16))
    lam_vecs = lambda e: (a_lq1[e][None], a_lk1[e][None], a_lq2[e][None], a_lk2[e][None], a_subln[e][None])

    tp = -(-(sp_len + CHUNK) // ATT_BLK) * ATT_BLK
    n_pad = tp - sp_len - N_META
    xp = jnp.concatenate([jnp.zeros((n_pad, dm), dt), meta_tokens.astype(dt), x_prompt[0]], axis=0)
    bias2 = jnp.stack([bias_tile(rel_bias, ATT_BLK, ATT_BLK, 0, causal=True),
                       bias_tile(rel_bias, ATT_BLK, ATT_BLK, -ATT_BLK)], axis=0)

    def attn_p(proj, e, lam_init):
        return attn_prompt(proj, bias2, rel_bias, *lam_vecs(e), n_pad=n_pad, lam_init=lam_init)

    zeros = lambda *s: jnp.zeros(s, F32)
    st_p = dict(b_c=zeros(n_even, bp, B_HEADS, B_DK, B_DK), b_n=zeros(n_even, bp, B_HEADS, B_DK),
                b_m=zeros(n_even, bp, B_HEADS), b_conv=zeros(n_even, bp, B_CONV - 1, 2 * HALF),
                c_re=zeros(n_odd, bp, C_GROUPS, C_STATE), c_im=zeros(n_odd, bp, C_GROUPS, C_STATE),
                d_s=zeros(n_odd, bp, D_HEADS, D_HEAD, D_HEAD), d_shift=zeros(n_odd, bp, 1, d_mu.shape[1]))
    y_p, new_p = _trunk(xp, p, bsz=bp, t=tp, seq_blk=SEQ_BLK, n_pad=n_pad, attn_fn=attn_p, st=st_p)

    tk = cache_a_k.shape[2]
    past_len = tk - N_META
    cid = lambda pos: np.where(pos < N_META, 0, 1 + (pos - N_META) // CHUNK)
    q_pos = N_META + past_len + np.arange(ts)
    k_pos = np.arange(tk + ts)
    assert (cid(k_pos)[None, :] <= cid(q_pos)[:, None]).all()
    bias_past = bias_tile(rel_bias, ts, tk, -(N_META + past_len))
    bias_new = bias_tile(rel_bias, ts, ts, 0)

    def attn_s(proj, e, lam_init):
        return attn_sample(proj, cache_a_k[e].reshape(bs, tk, HALF), cache_a_v[e].reshape(bs, tk, HALF),
                           bias_past, bias_new, *lam_vecs(e), ts=ts, lam_init=lam_init)

    st_s = dict(b_c=state_b_c, b_n=state_b_n, b_m=state_b_m, b_conv=state_b_conv, c_re=state_c_re,
                c_im=state_c_im, d_s=state_d_s, d_shift=state_d_shift)
    y_s, new_s = _trunk(x_sample.reshape(bs * ts, dm), p, bsz=bs, t=ts, seq_blk=ts, n_pad=0, attn_fn=attn_s, st=st_s)

    def pack(new, bsz, t, drop):
        kv = lambda a: a[:, drop:].reshape(bsz, t - drop, A_HEADS, A_DV)
        return (jnp.stack([kv(a) for a in new['a_k']]), jnp.stack([kv(a) for a in new['a_v']]),
                jnp.stack(new['b_c']), jnp.stack(new['b_n']), jnp.stack(new['b_m']), jnp.stack(new['b_conv']),
                jnp.stack(new['c_re']), jnp.stack(new['c_im']), jnp.stack(new['d_s']), jnp.stack(new['d_shift']))

    out_p = pack(new_p, bp, tp, n_pad)
    out_s = pack(new_s, bs, ts, 0)
    return (y_p[n_pad + N_META:][None], y_s.reshape(bs, ts, dm)) + out_p + out_s
```

```python
import functools
import math

import numpy as np
import jax
import jax.numpy as jnp
from jax import lax
from jax.experimental import pallas as pl
from jax.experimental.pallas import tpu as pltpu

F32 = jnp.float32
BF16 = jnp.bfloat16
I32 = jnp.int32

CHUNK = 64
N_META = 16
EPS = 1e-6
A_HEADS = 4
A_DH = 64
A_DV = 128
N_BUCKETS = 32
MAX_DIST = 128
B_HEADS = 4
B_DK = 128
B_CONV = 4
C_GROUP = 16
C_GROUPS = 32
C_STATE = 64
C_WIDTH = C_GROUP * C_GROUPS
D_HEAD = 64
D_HEADS = 8
D_WIDTH = D_HEAD * D_HEADS
D_GN_EPS = 64e-5
HALF = 512

NEG = -1e30
LANES = 128
VMEM_LIMIT = 56 * 1024 * 1024

ROW_TILE = 512
ATT_BLK = 512
SEQ_BLK = 128
FFN_TH = 1408


def _cparams(sem):
    return pltpu.CompilerParams(dimension_semantics=sem, vmem_limit_bytes=VMEM_LIMIT)


def _dot(a, b):
    return jnp.dot(a, b, preferred_element_type=F32)


def _dot_nt(a, b):
    return lax.dot_general(a, b, (((1,), (1,)), ((), ())), preferred_element_type=F32)


def _dot_tn(a, b):
    return lax.dot_general(a, b, (((0,), (0,)), ((), ())), preferred_element_type=F32)


def _sigmoid(x):
    return 1.0 / (1.0 + jnp.exp(-x))


def _softplus(x):
    return jnp.maximum(x, 0.0) + jnp.log1p(jnp.exp(-jnp.abs(x)))


def _rms(x, g):
    return x * lax.rsqrt(jnp.mean(x * x, axis=-1, keepdims=True) + EPS) * g


def _split3(x):
    hi = x.astype(BF16)
    r1 = x - hi.astype(F32)
    mid = r1.astype(BF16)
    lo = (r1 - mid.astype(F32)).astype(BF16)
    return hi, mid, lo


def _segsum(x, ones):
    hi, mid, lo = _split3(x)
    return _dot(hi, ones) + _dot(mid, ones) + _dot(lo, ones)


def _block_ones(n, seg):
    r = lax.broadcasted_iota(I32, (n, n), 0) // seg
    c = lax.broadcasted_iota(I32, (n, n), 1) // seg
    return jnp.where(r == c, 1.0, 0.0).astype(BF16)


def _rms_mm_kernel(x_ref, g_ref, w_ref, o_ref):
    h = _rms(x_ref[...], g_ref[...])
    o_ref[...] = _dot(h.astype(BF16), w_ref[...])


def rms_matmul(x, g, w, tm=ROW_TILE):
    m, d = x.shape
    tm = min(tm, m)
    n = w.shape[1]
    return pl.pallas_call(
        _rms_mm_kernel,
        grid=(m // tm,),
        in_specs=[pl.BlockSpec((tm, d), lambda i: (i, 0)),
                  pl.BlockSpec((1, d), lambda i: (0, 0)),
                  pl.BlockSpec((d, n), lambda i: (0, 0))],
        out_specs=pl.BlockSpec((tm, n), lambda i: (i, 0)),
        out_shape=jax.ShapeDtypeStruct((m, n), F32),
        compiler_params=_cparams(("parallel",)),
        name="rms_matmul",
    )(x, g, w)


def _outproj_kernel(x_ref, a_ref, b_ref, wa_ref, wb_ref, o_ref):
    acc = _dot(a_ref[...].astype(BF16), wa_ref[...]) + _dot(b_ref[...].astype(BF16), wb_ref[...])
    o_ref[...] = x_ref[...] + acc


def outproj(x, a, b, wa, wb, tm=ROW_TILE):
    m, d = x.shape
    tm = min(tm, m)
    k = a.shape[1]
    row = lambda i: (i, 0)
    fix = lambda i: (0, 0)
    return pl.pallas_call(
        _outproj_kernel,
        grid=(m // tm,),
        in_specs=[pl.BlockSpec((tm, d), row), pl.BlockSpec((tm, k), row), pl.BlockSpec((tm, k), row),
                  pl.BlockSpec((k, d), fix), pl.BlockSpec((k, d), fix)],
        out_specs=pl.BlockSpec((tm, d), row),
        out_shape=jax.ShapeDtypeStruct((m, d), F32),
        compiler_params=_cparams(("parallel",)),
        name="outproj",
    )(x, a, b, wa, wb)


def _ffn_kernel(x_ref, g_ref, gf_ref, w1_ref, w3_ref, w2_ref, o_ref, h_scr, *, final_norm):
    j = pl.program_id(1)

    @pl.when(j == 0)
    def _():
        x = x_ref[...]
        h_scr[...] = _rms(x, g_ref[...]).astype(BF16)
        o_ref[...] = x

    h = h_scr[...]
    a = _dot(h, w1_ref[...])
    b = _dot(h, w3_ref[...])
    u = (a * _sigmoid(a)) * b
    o_ref[...] += _dot(u.astype(BF16), w2_ref[...])

    if final_norm:
        @pl.when(j == pl.num_programs(1) - 1)
        def _():
            o_ref[...] = _rms(o_ref[...], gf_ref[...])


def ffn(x, g, w1, w3, w2, gf=None, tm=ROW_TILE, th=FFN_TH):
    m, d = x.shape
    tm = min(tm, m)
    hid = w1.shape[1]
    final_norm = gf is not None
    if gf is None:
        gf = g
    return pl.pallas_call(
        functools.partial(_ffn_kernel, final_norm=final_norm),
        grid=(m // tm, hid // th),
        in_specs=[pl.BlockSpec((tm, d), lambda i, j: (i, 0)),
                  pl.BlockSpec((1, d), lambda i, j: (0, 0)),
                  pl.BlockSpec((1, d), lambda i, j: (0, 0)),
                  pl.BlockSpec((d, th), lambda i, j: (0, j)),
                  pl.BlockSpec((d, th), lambda i, j: (0, j)),
                  pl.BlockSpec((th, d), lambda i, j: (j, 0))],
        out_specs=pl.BlockSpec((tm, d), lambda i, j: (i, 0)),
        out_shape=jax.ShapeDtypeStruct((m, d), F32),
        scratch_shapes=[pltpu.VMEM((tm, d), BF16)],
        compiler_params=_cparams(("parallel", "arbitrary")),
        name="ffn",
    )(x, g, gf, w1, w3, w2)


def _bias_kernel(rb_ref, o_ref, *, rel0, causal):
    nq, nk = o_ref.shape[1], o_ref.shape[2]
    a = lax.broadcasted_iota(I32, (nq, nk), 0)
    b = lax.broadcasted_iota(I32, (nq, nk), 1)
    rel = b - a + rel0
    nb = N_BUCKETS // 2
    max_exact = nb // 2
    ret = jnp.where(rel > 0, nb, 0)
    n = jnp.abs(rel)
    nf = jnp.maximum(n, 1).astype(F32)
    large = max_exact + (jnp.log(nf / max_exact) / math.log(MAX_DIST / max_exact) * (nb - max_exact)).astype(I32)
    large = jnp.minimum(large, nb - 1)
    bucket = ret + jnp.where(n < max_exact, n, large)
    for h in range(A_HEADS):
        acc = jnp.zeros((nq, nk), F32)
        for bk in range(N_BUCKETS):
            acc = jnp.where(bucket == bk, rb_ref[bk, h], acc)
        if causal:
            acc = jnp.where((b // CHUNK) <= (a // CHUNK), acc, NEG)
        o_ref[h] = acc


def bias_tile(rel_bias, nq, nk, rel0, causal=False):
    return pl.pallas_call(
        functools.partial(_bias_kernel, rel0=rel0, causal=causal),
        in_specs=[pl.BlockSpec(memory_space=pltpu.SMEM)],
        out_specs=pl.BlockSpec(memory_space=pltpu.VMEM),
        out_shape=jax.ShapeDtypeStruct((A_HEADS, nq, nk), F32),
        compiler_params=pltpu.CompilerParams(vmem_limit_bytes=VMEM_LIMIT),
        name="bias_tile",
    )(rel_bias)


def _lambda(lq1_ref, lk1_ref, lq2_ref, lk2_ref, lam_init):
    s1 = jnp.sum(lq1_ref[...] * lk1_ref[...], axis=-1, keepdims=True)
    s2 = jnp.sum(lq2_ref[...] * lk2_ref[...], axis=-1, keepdims=True)
    return jnp.exp(s1) - jnp.exp(s2) + lam_init


def _attn_finish(acc0, l0, acc1, l1, lam, g, lam_init):
    o = acc0 / l0 - lam * (acc1 / l1)
    return _rms(o, g) * (1.0 - lam_init)


def _attn_prompt_kernel(qi_ref, kj_ref, rb_ref, q_ref, k_ref, v_ref, bias_ref,
                        lq1_ref, lk1_ref, lq2_ref, lk2_ref, g_ref, o_ref,
                        m_scr, l_scr, acc_scr, *, blk, n_pad, lam_init):
    s = pl.program_id(0)
    i = qi_ref[s]
    j = kj_ref[s]
    d = i - j

    @pl.when(j == 0)
    def _():
        m_scr[...] = jnp.full(m_scr.shape, NEG, F32)
        l_scr[...] = jnp.zeros(l_scr.shape, F32)
        acc_scr[...] = jnp.zeros(acc_scr.shape, F32)

    kpos = j * blk + lax.broadcasted_iota(I32, (1, blk), 1)
    pen = jnp.where(kpos < n_pad, NEG, 0.0)

    def update(bias_fn):
        for h in range(A_HEADS):
            vb = v_ref[:, h * A_DV:(h + 1) * A_DV].astype(BF16)
            bias = bias_fn(h)
            for mm in range(2):
                idx = 2 * h + mm
                c0 = h * 2 * A_DH + mm * A_DH
                qb = (q_ref[:, c0:c0 + A_DH] * (A_DH ** -0.5)).astype(BF16)
                kb = k_ref[:, c0:c0 + A_DH].astype(BF16)
                sc = _dot_nt(qb, kb) + bias
                m_old = m_scr[idx]
                m_new = jnp.maximum(m_old, jnp.max(sc, axis=-1, keepdims=True))
                alpha = jnp.exp(m_old - m_new)
                p = jnp.exp(sc - m_new)
                l_scr[idx] = alpha * l_scr[idx] + jnp.sum(p, axis=-1, keepdims=True)
                acc_scr[idx] = alpha * acc_scr[idx] + _dot(p.astype(BF16), vb)
                m_scr[idx] = m_new

    @pl.when(d >= 2)
    def _():
        update(lambda h: rb_ref[N_BUCKETS // 2 - 1, h] + pen)

    @pl.when(d < 2)
    def _():
        update(lambda h: bias_ref[d, h] + pen)

    @pl.when(d == 0)
    def _():
        lam = _lambda(lq1_ref, lk1_ref, lq2_ref, lk2_ref, lam_init)
        rows = i * blk + lax.broadcasted_iota(I32, (blk, 1), 0)
        valid = rows >= n_pad
        for h in range(A_HEADS):
            y = _attn_finish(acc_scr[2 * h], l_scr[2 * h], acc_scr[2 * h + 1], l_scr[2 * h + 1],
                             lam, g_ref[...], lam_init)
            o_ref[:, h * A_DV:(h + 1) * A_DV] = jnp.where(valid, y, 0.0)


def attn_prompt(proj, bias2, rel_bias, lq1, lk1, lq2, lk2, g, *, n_pad, lam_init, blk=ATT_BLK):
    t = proj.shape[0]
    nb = t // blk
    qi = np.array([i for i in range(nb) for _ in range(i + 1)], np.int32)
    kj = np.array([j for i in range(nb) for j in range(i + 1)], np.int32)
    vec = lambda n: pl.BlockSpec((1, n), lambda s, qi, kj: (0, 0))
    grid_spec = pltpu.PrefetchScalarGridSpec(
        num_scalar_prefetch=2,
        grid=(len(qi),),
        in_specs=[pl.BlockSpec(memory_space=pltpu.SMEM),
                  pl.BlockSpec((blk, HALF), lambda s, qi, kj: (qi[s], 0)),
                  pl.BlockSpec((blk, HALF), lambda s, qi, kj: (kj[s], 1)),
                  pl.BlockSpec((blk, HALF), lambda s, qi, kj: (kj[s], 2)),
                  pl.BlockSpec((2, A_HEADS, blk, blk), lambda s, qi, kj: (0, 0, 0, 0)),
                  vec(A_DH), vec(A_DH), vec(A_DH), vec(A_DH), vec(A_DV)],
        out_specs=pl.BlockSpec((blk, HALF), lambda s, qi, kj: (qi[s], 0)),
        scratch_shapes=[pltpu.VMEM((2 * A_HEADS, blk, 1), F32),
                        pltpu.VMEM((2 * A_HEADS, blk, 1), F32),
                        pltpu.VMEM((2 * A_HEADS, blk, A_DV), F32)],
    )
    return pl.pallas_call(
        functools.partial(_attn_prompt_kernel, blk=blk, n_pad=n_pad, lam_init=lam_init),
        grid_spec=grid_spec,
        out_shape=jax.ShapeDtypeStruct((t, HALF), F32),
        compiler_params=_cparams(("arbitrary",)),
        name="attn_prompt",
    )(jnp.asarray(qi), jnp.asarray(kj), rel_bias, proj, proj, proj, bias2, lq1, lk1, lq2, lk2, g)


def _attn_sample_kernel(q_ref, kn_ref, vn_ref, kc_ref, vc_ref, bp_ref, bn_ref,
                        lq1_ref, lk1_ref, lq2_ref, lk2_ref, g_ref, o_ref, *, lam_init):
    lam = _lambda(lq1_ref, lk1_ref, lq2_ref, lk2_ref, lam_init)
    for h in range(A_HEADS):
        vsl = slice(h * A_DV, (h + 1) * A_DV)
        vp = vc_ref[0, :, vsl].astype(BF16)
        vn = vn_ref[:, vsl].astype(BF16)
        outs = []
        for mm in range(2):
            c0 = h * 2 * A_DH + mm * A_DH
            qb = (q_ref[:, c0:c0 + A_DH] * (A_DH ** -0.5)).astype(BF16)
            sp = _dot_nt(qb, kc_ref[0, :, c0:c0 + A_DH].astype(BF16)) + bp_ref[h]
            sn = _dot_nt(qb, kn_ref[:, c0:c0 + A_DH].astype(BF16)) + bn_ref[h]
            mx = jnp.maximum(jnp.max(sp, axis=-1, keepdims=True), jnp.max(sn, axis=-1, keepdims=True))
            pp = jnp.exp(sp - mx)
            pn = jnp.exp(sn - mx)
            l = jnp.sum(pp, axis=-1, keepdims=True) + jnp.sum(pn, axis=-1, keepdims=True)
            acc = _dot(pp.astype(BF16), vp) + _dot(pn.astype(BF16), vn)
            outs.append((acc, l))
        o_ref[:, vsl] = _attn_finish(outs[0][0], outs[0][1], outs[1][0], outs[1][1], lam, g_ref[...], lam_init)


def attn_sample(proj, kc, vc, bias_past, bias_new, lq1, lk1, lq2, lk2, g, *, ts, lam_init):
    bsz, tk, _ = kc.shape
    vec = lambda n: pl.BlockSpec((1, n), lambda b: (0, 0))
    return pl.pallas_call(
        functools.partial(_attn_sample_kernel, lam_init=lam_init),
        grid=(bsz,),
        in_specs=[pl.BlockSpec((ts, HALF), lambda b: (b, 0)),
                  pl.BlockSpec((ts, HALF), lambda b: (b, 1)),
                  pl.BlockSpec((ts, HALF), lambda b: (b, 2)),
                  pl.BlockSpec((1, tk, HALF), lambda b: (b, 0, 0)),
                  pl.BlockSpec((1, tk, HALF), lambda b: (b, 0, 0)),
                  pl.BlockSpec((A_HEADS, ts, tk), lambda b: (0, 0, 0)),
                  pl.BlockSpec((A_HEADS, ts, ts), lambda b: (0, 0, 0)),
                  vec(A_DH), vec(A_DH), vec(A_DH), vec(A_DH), vec(A_DV)],
        out_specs=pl.BlockSpec((ts, HALF), lambda b: (b, 0)),
        out_shape=jax.ShapeDtypeStruct((bsz * ts, HALF), F32),
        compiler_params=_cparams(("parallel",)),
        name="attn_sample",
    )(proj, proj, proj, kc, vc, bias_past, bias_new, lq1, lk1, lq2, lk2, g)


def _mlstm_kernel(bq_ref, bk_ref, bv_ref, bo_ref, gc_ref, gr_ref, c0_ref, n0_ref, m0_ref, conv0_ref,
                  cw_ref, cb_ref, gbc_ref, gbr_ref, bn_ref,
                  h_ref, c_out, n_out, m_out,
                  xbuf, c_scr, n_scr, m_scr, *, ln, n_pad):
    tb = pl.program_id(1)
    nt = pl.num_programs(1)
    halo = B_CONV - 1
    base = 8 - halo

    @pl.when(tb == 0)
    def _():
        xbuf[base:8, :] = conv0_ref[0]
        c_scr[...] = c0_ref[0]
        n_scr[...] = n0_ref[0]
        for h in range(B_HEADS):
            m_scr[h] = m0_ref[0, :, h:h + 1]

    xbuf[8:8 + ln, 0:HALF] = bq_ref[...]
    xbuf[8:8 + ln, HALF:2 * HALF] = bk_ref[...]
    conv = cb_ref[...] + cw_ref[0:1, :] * xbuf[base:base + ln, :]
    for jj in range(1, B_CONV):
        conv = conv + cw_ref[jj:jj + 1, :] * xbuf[base + jj:base + jj + ln, :]
    xbuf[base:8, :] = xbuf[8 + ln - halo:8 + ln, :]
    conv = conv * _sigmoid(conv)
    q_all = conv[:, 0:HALF] * (B_DK ** -0.5)
    k_all = conv[:, HALF:2 * HALF]

    rows = tb * ln + lax.broadcasted_iota(I32, (ln, 1), 0)
    cols = tb * ln + lax.broadcasted_iota(I32, (1, ln), 1)
    valid_c = rows >= n_pad
    valid_r = cols >= n_pad

    gc = gc_ref[...] + gbc_ref[...]
    gr = gr_ref[0] + gbr_ref[...]
    li_c = jnp.where(valid_c, gc, NEG)
    li_r = jnp.where(valid_r, gr, NEG)
    lf_c = jnp.where(valid_c, -_softplus(-gc), 0.0)
    lf_r = jnp.where(valid_r, -_softplus(-gr), 0.0)

    ri = lax.broadcasted_iota(I32, (ln, ln), 0)
    ci = lax.broadcasted_iota(I32, (ln, ln), 1)
    tril = ri >= ci
    tril_f = jnp.where(tril, 1.0, 0.0)
    triu_f = jnp.where(ri <= ci, 1.0, 0.0)
    b_c = jnp.dot(tril_f, lf_c, preferred_element_type=F32, precision=lax.Precision.HIGHEST)
    b_r = jnp.dot(lf_r, triu_f, preferred_element_type=F32, precision=lax.Precision.HIGHEST)

    for h in range(B_HEADS):
        sl = slice(h * B_DK, (h + 1) * B_DK)
        qh = q_all[:, sl].astype(BF16)
        kh = k_all[:, sl]
        vh = bv_ref[:, sl].astype(BF16)
        c = c_scr[h]
        n = n_scr[h]
        m = m_scr[h]
        bc = b_c[:, B_HEADS + h:B_HEADS + h + 1]
        br = b_r[B_HEADS + h:B_HEADS + h + 1, :]
        inter = bc + m
        dmat = jnp.where(tril, bc - br + li_r[h:h + 1, :], NEG)
        mt = jnp.maximum(inter, jnp.max(dmat, axis=-1, keepdims=True))
        w_inter = jnp.exp(inter - mt)
        s = _dot_nt(qh, kh.astype(BF16)) * jnp.exp(dmat - mt)
        num = w_inter * _dot(qh, c.astype(BF16)) + _dot(s.astype(BF16), vh)
        qn = jnp.sum(qh.astype(F32) * n, axis=-1, keepdims=True)
        den = w_inter * qn + jnp.sum(s, axis=-1, keepdims=True)
        hh = num / jnp.maximum(jnp.abs(den), jnp.exp(-mt))
        b_last = bc[ln - 1:ln, :]
        g = b_last - bc + li_c[:, h:h + 1]
        m_new = jnp.maximum(b_last + m, jnp.max(g, axis=0, keepdims=True))
        decay = jnp.exp(b_last + m - m_new)
        wk = (jnp.exp(g - m_new) * kh)
        c_scr[h] = decay * c + _dot_tn(wk.astype(BF16), vh)
        n_scr[h] = decay * n + jnp.sum(wk, axis=0, keepdims=True)
        m_scr[h] = m_new
        hn = _rms(hh, bn_ref[:, sl]) * _sigmoid(bo_ref[:, sl])
        h_ref[:, sl] = jnp.where(valid_c, hn, 0.0)

    @pl.when(tb == nt - 1)
    def _():
        c_out[0] = c_scr[...]
        n_out[0] = n_scr[...]
        lane = lax.broadcasted_iota(I32, (1, B_HEADS), 1)
        mrow = jnp.zeros((1, B_HEADS), F32)
        for h in range(B_HEADS):
            mrow = jnp.where(lane == h, m_scr[h], mrow)
        m_out[0] = mrow


def mlstm(proj, gates_r, c0, n0, m0, conv0, cw, cb, igb, fgb, bnorm, *, bsz, t, ln, n_pad):
    nt = t // ln
    gb = jnp.concatenate([igb, fgb])
    gw = proj.shape[1] // LANES - 1
    row = lambda c: (lambda b, i: (b * nt + i, c))
    fix2 = lambda b, i: (0, 0)
    out_shapes = (jax.ShapeDtypeStruct((bsz * t, HALF), F32),
                  jax.ShapeDtypeStruct((bsz, B_HEADS, B_DK, B_DK), F32),
                  jax.ShapeDtypeStruct((bsz, B_HEADS, 1, B_DK), F32),
                  jax.ShapeDtypeStruct((bsz, 1, B_HEADS), F32))
    return pl.pallas_call(
        functools.partial(_mlstm_kernel, ln=ln, n_pad=n_pad),
        grid=(bsz, nt),
        in_specs=[pl.BlockSpec((ln, HALF), row(3)), pl.BlockSpec((ln, HALF), row(4)),
                  pl.BlockSpec((ln, HALF), row(5)), pl.BlockSpec((ln, HALF), row(6)),
                  pl.BlockSpec((ln, LANES), row(gw)),
                  pl.BlockSpec((1, 2 * B_HEADS, ln), lambda b, i: (b * nt + i, 0, 0)),
                  pl.BlockSpec((1, B_HEADS, B_DK, B_DK), lambda b, i: (b, 0, 0, 0)),
                  pl.BlockSpec((1, B_HEADS, 1, B_DK), lambda b, i: (b, 0, 0, 0)),
                  pl.BlockSpec((1, 1, B_HEADS), lambda b, i: (b, 0, 0)),
                  pl.BlockSpec((1, B_CONV - 1, 2 * HALF), lambda b, i: (b, 0, 0)),
                  pl.BlockSpec((B_CONV, 2 * HALF), fix2), pl.BlockSpec((1, 2 * HALF), fix2),
                  pl.BlockSpec((1, LANES), fix2), pl.BlockSpec((2 * B_HEADS, 1), fix2),
                  pl.BlockSpec((1, HALF), fix2)],
        out_specs=(pl.BlockSpec((ln, HALF), lambda b, i: (b * nt + i, 0)),
                   pl.BlockSpec((1, B_HEADS, B_DK, B_DK), lambda b, i: (b, 0, 0, 0)),
                   pl.BlockSpec((1, B_HEADS, 1, B_DK), lambda b, i: (b, 0, 0, 0)),
                   pl.BlockSpec((1, 1, B_HEADS), lambda b, i: (b, 0, 0))),
        out_shape=out_shapes,
        scratch_shapes=[pltpu.VMEM((ln + 8, 2 * HALF), F32),
                        pltpu.VMEM((B_HEADS, B_DK, B_DK), F32),
                        pltpu.VMEM((B_HEADS, 1, B_DK), F32),
                        pltpu.VMEM((B_HEADS, 1, 1), F32)],
        compiler_params=_cparams(("parallel", "arbitrary")),
        name="mlstm",
    )(proj, proj, proj, proj, proj, gates_r, c0, n0, m0, conv0, cw, cb,
      jnp.pad(gb, (0, LANES - 2 * B_HEADS))[None], gb[:, None], bnorm)


def _s5_param_kernel(lr_ref, li_ref, ldt_ref, bre_ref, bim_ref, ar_ref, ai_ref, bbr_ref, bbi_ref):
    lr = lr_ref[...]
    li = li_ref[...]
    dt = jnp.exp(ldt_ref[...])
    mag = jnp.exp(lr * dt)
    ar = mag * jnp.cos(li * dt)
    ai = mag * jnp.sin(li * dt)
    den = lr * lr + li * li
    cr = ((ar - 1.0) * lr + ai * li) / den
    ci = (ai * lr - (ar - 1.0) * li) / den
    ar_ref[...] = ar
    ai_ref[...] = ai
    nr, nc = bre_ref.shape
    same = (lax.broadcasted_iota(I32, (nr, nc), 0) // C_GROUP) == (lax.broadcasted_iota(I32, (nr, nc), 1) // C_STATE)
    br = bre_ref[...]
    bi = bim_ref[...]
    bbr_ref[...] = jnp.where(same, cr * br - ci * bi, 0.0).astype(BF16)
    bbi_ref[...] = jnp.where(same, cr * bi + ci * br, 0.0).astype(BF16)


def s5_params(lr, li, ldt, bre_rep, bim_rep):
    ns = lr.shape[1]
    vm = pl.BlockSpec(memory_space=pltpu.VMEM)
    return pl.pallas_call(
        _s5_param_kernel,
        in_specs=[vm] * 5,
        out_specs=(vm, vm, vm, vm),
        out_shape=(jax.ShapeDtypeStruct((1, ns), F32), jax.ShapeDtypeStruct((1, ns), F32),
                   jax.ShapeDtypeStruct(bre_rep.shape, BF16), jax.ShapeDtypeStruct(bre_rep.shape, BF16)),
        compiler_params=pltpu.CompilerParams(vmem_limit_bytes=VMEM_LIMIT),
        name="s5_params",
    )(lr, li, ldt, bre_rep, bim_rep)


def _s5_kernel(u_ref, ar_ref, ai_ref, bbr_ref, bbi_ref, cre_ref, cim_ref, d_ref, wg_ref, x0r_ref, x0i_ref,
               o_ref, xr_out, xi_out, sr_scr, si_scr, bur, bui, xra, xia, *, tb_len, n_pad):
    tb = pl.program_id(1)
    nt = pl.num_programs(1)

    @pl.when(tb == 0)
    def _():
        sr_scr[...] = x0r_ref[0]
        si_scr[...] = x0i_ref[0]

    u = u_ref[...]
    ub = u.astype(BF16)
    bur[...] = _dot(ub, bbr_ref[...])
    bui[...] = _dot(ub, bbi_ref[...])
    ar = ar_ref[...]
    ai = ai_ref[...]

    sub = lax.broadcasted_iota(I32, (8, ar.shape[1]), 0)

    def body(grp, carry):
        xr, xi = carry
        base = pl.multiple_of(grp * 8, 8)
        br8 = bur[pl.ds(base, 8), :]
        bi8 = bui[pl.ds(base, 8), :]
        xr8 = jnp.zeros(br8.shape, F32)
        xi8 = jnp.zeros(br8.shape, F32)
        for i in range(8):
            nxr = ar * xr - ai * xi + br8[i:i + 1, :]
            nxi = ar * xi + ai * xr + bi8[i:i + 1, :]
            xr, xi = nxr, nxi
            xr8 = jnp.where(sub == i, xr, xr8)
            xi8 = jnp.where(sub == i, xi, xi8)
        xra[pl.ds(base, 8), :] = xr8
        xia[pl.ds(base, 8), :] = xi8
        return xr, xi

    xr, xi = lax.fori_loop(0, tb_len // 8, body, (sr_scr[...], si_scr[...]))
    sr_scr[...] = xr
    si_scr[...] = xi

    y = _dot(xra[...].astype(BF16), cre_ref[...]) - _dot(xia[...].astype(BF16), cim_ref[...]) + d_ref[...] * u
    yg = 0.5 * y * (1.0 + jnp.tanh(math.sqrt(2.0 / math.pi) * (y + 0.044715 * (y * y * y))))
    oc = yg * _sigmoid(_dot(yg.astype(BF16), wg_ref[...]))
    rows = tb * tb_len + lax.broadcasted_iota(I32, (tb_len, 1), 0)
    o_ref[...] = jnp.where(rows >= n_pad, oc, 0.0)

    @pl.when(tb == nt - 1)
    def _():
        xr_out[0] = xr
        xi_out[0] = xi


def s5(proj, ar, ai, bbr, bbi, cre, cim, dskip, wglu, x0r, x0i, *, bsz, t, tb_len, n_pad):
    nt = t // tb_len
    ns = ar.shape[1]
    fix2 = lambda b, i: (0, 0)
    st = pl.BlockSpec((1, 1, ns), lambda b, i: (b, 0, 0))
    return pl.pallas_call(
        functools.partial(_s5_kernel, tb_len=tb_len, n_pad=n_pad),
        grid=(bsz, nt),
        in_specs=[pl.BlockSpec((tb_len, HALF), lambda b, i: (b * nt + i, 0)),
                  pl.BlockSpec((1, ns), fix2), pl.BlockSpec((1, ns), fix2),
                  pl.BlockSpec((HALF, ns), fix2), pl.BlockSpec((HALF, ns), fix2),
                  pl.BlockSpec((ns, HALF), fix2), pl.BlockSpec((ns, HALF), fix2),
                  pl.BlockSpec((1, HALF), fix2), pl.BlockSpec((HALF, HALF), fix2), st, st],
        out_specs=(pl.BlockSpec((tb_len, HALF), lambda b, i: (b * nt + i, 0)), st, st),
        out_shape=(jax.ShapeDtypeStruct((bsz * t, HALF), F32),
                   jax.ShapeDtypeStruct((bsz, 1, ns), F32), jax.ShapeDtypeStruct((bsz, 1, ns), F32)),
        scratch_shapes=[pltpu.VMEM((1, ns), F32), pltpu.VMEM((1, ns), F32),
                        pltpu.VMEM((tb_len, ns), F32), pltpu.VMEM((tb_len, ns), F32),
                        pltpu.VMEM((tb_len, ns), F32), pltpu.VMEM((tb_len, ns), F32)],
        compiler_params=_cparams(("parallel", "arbitrary")),
        name="s5",
    )(proj, ar, ai, bbr, bbi, cre, cim, dskip, wglu, x0r, x0i)


def _rwkv_kernel(r_ref, k_ref, v_ref, lo_ref, sh0_ref, s0_ref,
                 mu_ref, w0_ref, ww2_ref, a0_ref, wa2_ref, wg2_ref, kk_ref, ka_ref, rk_ref, lng_ref, lnb_ref,
                 o_ref, s_out,
                 xbuf, s_scr, w_scr, kk_scr, b_scr, k_scr, r_scr, v_scr, y_scr, *, tb_len, n_pad):
    tb = pl.program_id(1)
    nt = pl.num_programs(1)
    npair = D_HEADS // 2
    ncol = xbuf.shape[1]

    @pl.when(tb == 0)
    def _():
        xbuf[7:8, :] = sh0_ref[0]
        s_scr[...] = s0_ref[0]

    xbuf[8:8 + tb_len, 0:D_WIDTH] = r_ref[...]
    xbuf[8:8 + tb_len, D_WIDTH:2 * D_WIDTH] = k_ref[...]
    xbuf[8:8 + tb_len, 2 * D_WIDTH:3 * D_WIDTH] = v_ref[...]
    xbuf[8:8 + tb_len, 3 * D_WIDTH:ncol] = lo_ref[...]
    cur = xbuf[8:8 + tb_len, :]
    prev = xbuf[7:7 + tb_len, :]
    xbuf[7:8, :] = xbuf[7 + tb_len:8 + tb_len, :]
    xm = cur + mu_ref[...] * (prev - cur)
    r = xm[:, 0:D_WIDTH]
    k = xm[:, D_WIDTH:2 * D_WIDTH]
    v = xm[:, 2 * D_WIDTH:3 * D_WIDTH]
    c0 = 3 * D_WIDTH
    wlo = xm[:, c0:c0 + 64]
    alo = xm[:, c0 + 64:c0 + 128]
    glo = xm[:, c0 + 128:c0 + 256]

    w_raw = w0_ref[...] + _dot(jnp.tanh(wlo).astype(BF16), ww2_ref[...])
    decay = jnp.exp(-jnp.exp(-_softplus(-w_raw) - 0.5))
    a = _sigmoid(a0_ref[...] + _dot(alo.astype(BF16), wa2_ref[...]))
    g = _dot(_sigmoid(glo).astype(BF16), wg2_ref[...])

    ones_h = _block_ones(D_WIDTH, D_HEAD)
    kk = k * kk_ref[...]
    kk = kk / jnp.maximum(jnp.sqrt(_segsum(kk * kk, ones_h)), 1e-12)
    k2 = k * (1.0 + (a - 1.0) * ka_ref[...])

    w_scr[...] = decay
    kk_scr[...] = kk
    b_scr[...] = kk * a
    k_scr[...] = k2
    r_scr[...] = r
    v_scr[...] = v

    ones_p = _block_ones(LANES, D_HEAD)
    eye2 = jnp.where((lax.broadcasted_iota(I32, (D_HEAD, LANES), 1) % D_HEAD)
                     == lax.broadcasted_iota(I32, (D_HEAD, LANES), 0), 1.0, 0.0)

    sub = lax.broadcasted_iota(I32, (8, LANES), 0)
    rows_p = [slice(p * D_HEAD, (p + 1) * D_HEAD) for p in range(npair)]
    nrow = npair * D_HEAD

    def split2(x):
        hi = x.astype(BF16)
        return hi, (x - hi.astype(F32)).astype(BF16)

    def stack(parts):
        return jnp.concatenate(parts, axis=0)

    def y_rows(y8, ycol, i):
        return [jnp.where(sub == i, jnp.sum(ycol[rows_p[p]] * eye2, axis=0, keepdims=True), y8[p])
                for p in range(npair)]

    def body(grp, state):
        base = pl.multiple_of(grp * 8, 8)
        tile = lambda scr: [scr[pl.ds(base, 8), p * LANES:(p + 1) * LANES] for p in range(npair)]
        kk8, w8, b8, k8, r8, v8 = tile(kk_scr), tile(w_scr), tile(b_scr), tile(k_scr), tile(r_scr), tile(v_scr)
        vh8 = [x.astype(BF16).astype(F32) for x in v8]
        vl8 = [x - h for x, h in zip(v8, vh8)]
        y8 = [jnp.zeros((8, LANES), F32) for _ in range(npair)]
        sp = list(state)
        yprod = None
        for i in range(8):
            row = slice(i, i + 1)
            lhs = list(split2(stack([sp[p] * (-kk8[p][row]) for p in range(npair)])))
            lhs.append(stack([vh8[p][row] * eye2 for p in range(npair)]).astype(BF16))
            lhs.append(stack([vl8[p][row] * eye2 for p in range(npair)]).astype(BF16))
            if yprod is not None:
                lhs.extend(split2(yprod))
            res = _dot(stack(lhs), ones_p)
            sa = res[0:nrow] + res[nrow:2 * nrow]
            vcol = res[2 * nrow:3 * nrow] + res[3 * nrow:4 * nrow]
            if yprod is not None:
                y8 = y_rows(y8, res[4 * nrow:5 * nrow] + res[5 * nrow:6 * nrow], i - 1)
            sp = [sp[p] * w8[p][row] + sa[rows_p[p]] * b8[p][row] + vcol[rows_p[p]] * k8[p][row]
                  for p in range(npair)]
            yprod = stack([sp[p] * r8[p][row] for p in range(npair)])
        res = _dot(stack(list(split2(yprod))), ones_p)
        y8 = y_rows(y8, res[0:nrow] + res[nrow:2 * nrow], 7)
        for p in range(npair):
            y_scr[pl.ds(base, 8), p * LANES:(p + 1) * LANES] = y8[p]
        return tuple(sp)

    state = lax.fori_loop(0, tb_len // 8, body, tuple(s_scr[p] for p in range(npair)))
    for p in range(npair):
        s_scr[p] = state[p]

    y = y_scr[...]
    inv = 1.0 / D_HEAD
    mean = _segsum(y, ones_h) * inv
    yc = y - mean
    var = _segsum(yc * yc, ones_h) * inv
    y = yc * lax.rsqrt(var + D_GN_EPS) * lng_ref[...] + lnb_ref[...]
    y = y + _segsum(r * k2 * rk_ref[...], ones_h) * v
    rows = tb * tb_len + lax.broadcasted_iota(I32, (tb_len, 1), 0)
    o_ref[...] = jnp.where(rows >= n_pad, y * g, 0.0)

    @pl.when(tb == nt - 1)
    def _():
        s_out[0] = s_scr[...]


def rwkv(proj, sh0, s0, mu, w0, ww2, a0, wa2, wg2, kkp, kap, rkp, lng, lnb, *, bsz, t, tb_len, n_pad):
    nt = t // tb_len
    npair = D_HEADS // 2
    ncol = mu.shape[1]
    nlo = ncol - 3 * D_WIDTH
    row = lambda c: (lambda b, i: (b * nt + i, c))
    fix2 = lambda b, i: (0, 0)
    vec = pl.BlockSpec((1, D_WIDTH), fix2)
    st = pl.BlockSpec((1, npair, D_HEAD, LANES), lambda b, i: (b, 0, 0, 0))
    big = lambda: pltpu.VMEM((tb_len, D_WIDTH), F32)
    return pl.pallas_call(
        functools.partial(_rwkv_kernel, tb_len=tb_len, n_pad=n_pad),
        grid=(bsz, nt),
        in_specs=[pl.BlockSpec((tb_len, D_WIDTH), row(1)), pl.BlockSpec((tb_len, D_WIDTH), row(2)),
                  pl.BlockSpec((tb_len, D_WIDTH), row(3)),
                  pl.BlockSpec((tb_len, nlo), row(4 * D_WIDTH // nlo)),
                  pl.BlockSpec((1, 1, ncol), lambda b, i: (b, 0, 0)), st,
                  pl.BlockSpec((1, ncol), fix2), vec,
                  pl.BlockSpec(ww2.shape, fix2), vec, pl.BlockSpec(wa2.shape, fix2), pl.BlockSpec(wg2.shape, fix2),
                  vec, vec, vec, vec, vec],
        out_specs=(pl.BlockSpec((tb_len, D_WIDTH), lambda b, i: (b * nt + i, 0)), st),
        out_shape=(jax.ShapeDtypeStruct((bsz * t, D_WIDTH), F32),
                   jax.ShapeDtypeStruct((bsz, npair, D_HEAD, LANES), F32)),
        scratch_shapes=[pltpu.VMEM((tb_len + 8, ncol), F32), pltpu.VMEM((npair, D_HEAD, LANES), F32),
                        big(), big(), big(), big(), big(), big(), big()],
        compiler_params=_cparams(("parallel", "arbitrary")),
        name="rwkv7",
    )(proj, proj, proj, proj, sh0, s0, mu, w0, ww2, a0, wa2, wg2, kkp, kap, rkp, lng, lnb)


def _pad_cols(w, mult=LANES):
    n = w.shape[1]
    return jnp.pad(w, ((0, 0), (0, (-n) % mult)))


def _pairs_from_heads(s):
    b = s.shape[0]
    return s.reshape(b, D_HEADS // 2, 2, D_HEAD, D_HEAD).transpose(0, 1, 3, 2, 4).reshape(b, D_HEADS // 2, D_HEAD, LANES)


def _heads_from_pairs(s):
    b = s.shape[0]
    return s.reshape(b, D_HEADS // 2, D_HEAD, 2, D_HEAD).transpose(0, 1, 3, 2, 4).reshape(b, D_HEADS, D_HEAD, D_HEAD)


def _trunk(x, p, *, bsz, t, seq_blk, n_pad, attn_fn, st):
    depth = p['norm_mix'].shape[0]
    new = {k: [] for k in ('a_k', 'a_v', 'b_c', 'b_n', 'b_m', 'b_conv', 'c_re', 'c_im', 'd_s', 'd_shift')}
    nt = t // seq_blk
    y = None
    for layer in range(depth):
        g_mix = p['norm_mix'][layer][None]
        if layer % 2 == 0:
            e = layer // 2
            proj = rms_matmul(x, g_mix, p['ev_w_in'][e])
            lam_init = 0.8 - 0.6 * math.exp(-0.3 * layer)
            oa = attn_fn(proj, e, lam_init)
            gcols = proj[:, 7 * HALF:7 * HALF + 2 * B_HEADS]
            gates_r = gcols.reshape(bsz * nt, seq_blk, 2 * B_HEADS).transpose(0, 2, 1)
            hb, bc, bn, bm = mlstm(proj, gates_r, st['b_c'][e], st['b_n'][e][:, :, None, :], st['b_m'][e][:, None, :],
                                   st['b_conv'][e], p['b_conv_w'][e], p['b_conv_b'][e][None],
                                   p['b_ig_bias'][e], p['b_fg_bias'][e], p['b_norm'][e][None],
                                   bsz=bsz, t=t, ln=seq_blk, n_pad=n_pad)
            p3 = proj.reshape(bsz, t, -1)
            new['a_k'].append(p3[:, :, HALF:2 * HALF])
            new['a_v'].append(p3[:, :, 2 * HALF:3 * HALF])
            new['b_c'].append(bc)
            new['b_n'].append(bn[:, :, 0, :])
            new['b_m'].append(bm[:, 0, :])
            new['b_conv'].append(p3[:, t - (B_CONV - 1):, 3 * HALF:5 * HALF])
            x = outproj(x, oa, hb, p['ev_w_out'][e][:HALF], p['ev_w_out'][e][HALF:])
        else:
            o = layer // 2
            proj = rms_matmul(x, g_mix, p['od_w_in'][o])
            sp = p['s5'][o]
            oc, cre, cim = s5(proj, sp['ar'], sp['ai'], sp['bbr'], sp['bbi'], sp['cre'], sp['cim'],
                              p['c_d'][o][None], p['c_w_glu'][o],
                              st['c_re'][o].reshape(bsz, 1, -1), st['c_im'][o].reshape(bsz, 1, -1),
                              bsz=bsz, t=t, tb_len=seq_blk, n_pad=n_pad)
            od, ds = rwkv(proj, st['d_shift'][o], _pairs_from_heads(st['d_s'][o]),
                          p['d_mu'][o][None], p['d_w0'][o][None], p['d_w_w2'][o], p['d_a0'][o][None],
                          p['d_w_a2'][o], p['d_w_g2'][o], p['d_k_k'][o][None], p['d_k_a'][o][None],
                          p['d_r_k'][o][None], p['d_ln_g'][o][None], p['d_ln_b'][o][None],
                          bsz=bsz, t=t, tb_len=seq_blk, n_pad=n_pad)
            new['c_re'].append(cre.reshape(bsz, C_GROUPS, C_STATE))
            new['c_im'].append(cim.reshape(bsz, C_GROUPS, C_STATE))
            new['d_s'].append(_heads_from_pairs(ds))
            new['d_shift'].append(proj.reshape(bsz, t, -1)[:, -1:, HALF:])
            x = outproj(x, oc, od, p['od_w_out'][o][:HALF], p['od_w_out'][o][HALF:])
        gf = p['norm_final'][None] if layer == depth - 1 else None
        x = ffn(x, p['norm_ffn'][layer][None], p['ffn_w1'][layer], p['ffn_w3'][layer], p['ffn_w2'][layer], gf=gf)
    return x, new


def kernel(x_prompt, x_sample, cache_a_k, cache_a_v, state_b_c, state_b_n, state_b_m, state_b_conv, state_c_re, state_c_im, state_d_s, state_d_shift, meta_tokens, rel_bias, norm_mix, norm_ffn, norm_final, ev_w_in, ev_w_out, a_lq1, a_lk1, a_lq2, a_lk2, a_subln, b_conv_w, b_conv_b, b_ig_bias, b_fg_bias, b_norm, od_w_in, od_w_out, c_lam_re, c_lam_im, c_log_dt, c_b_re, c_b_im, c_c_re, c_c_im, c_d, c_w_glu, d_mu, d_w0, d_w_w2, d_a0, d_w_a2, d_w_g2, d_k_k, d_k_a, d_r_k, d_ln_g, d_ln_b, ffn_w1, ffn_w3, ffn_w2):
    bp, sp_len, dm = x_prompt.shape
    bs, ts, _ = x_sample.shape
    n_even, n_odd = ev_w_in.shape[0], od_w_in.shape[0]
    assert bp == 1 and sp_len % CHUNK == 0 and ts % 8 == 0 and ts >= B_CONV - 1
    dt = x_prompt.dtype

    ns = C_GROUPS * C_STATE
    s5p = []
    for o in range(n_odd):
        bre = jnp.tile(c_b_re[o].transpose(2, 0, 1).reshape(C_GROUP, ns), (C_GROUPS, 1))
        bim = jnp.tile(c_b_im[o].transpose(2, 0, 1).reshape(C_GROUP, ns), (C_GROUPS, 1))
        ar, ai, bbr, bbi = s5_params(c_lam_re[o].reshape(1, ns), c_lam_im[o].reshape(1, ns),
                                     jnp.repeat(c_log_dt[o], C_STATE)[None], bre, bim)
        eye = jnp.eye(C_GROUPS, dtype=F32)
        blk = lambda c: (eye[:, None, :, None] * c.transpose(0, 2, 1)[:, :, None, :]).reshape(ns, C_WIDTH).astype(BF16)
        s5p.append(dict(ar=ar, ai=ai, bbr=bbr, bbi=bbi, cre=blk(c_c_re[o]), cim=blk(c_c_im[o])))
    p = dict(norm_mix=norm_mix, norm_ffn=norm_ffn, norm_final=norm_final,
             ev_w_in=[_pad_cols(ev_w_in[e]).astype(BF16) for e in range(n_even)],
             ev_w_out=ev_w_out.astype(BF16),
             od_w_in=[od_w_in[o].astype(BF16) for o in range(n_odd)], od_w_out=od_w_out.astype(BF16),
             b_conv_w=b_conv_w, b_conv_b=b_conv_b, b_ig_bias=b_ig_bias, b_fg_bias=b_fg_bias, b_norm=b_norm,
             s5=s5p, c_d=c_d, c_w_glu=c_w_glu.astype(BF16),
             d_mu=d_mu, d_w0=d_w0, d_w_w2=d_w_w2.astype(BF16), d_a0=d_a0, d_w_a2=d_w_a2.astype(BF16),
             d_w_g2=d_w_g2.astype(BF16), d_k_k=d_k_k, d_k_a=d_k_a, d_r_k=d_r_k, d_ln_g=d_ln_g, d_ln_b=d_ln_b,
             ffn_w1=ffn_w1.astype(BF16), ffn_w3=ffn_w3.astype(BF16), ffn_w2=ffn_w2.astype(BF16))
    lam_vecs = lambda e: (a_lq1[e][None], a_lk1[e][None], a_lq2[e][None], a_lk2[e][None], a_subln[e][None])

    tp = -(-(sp_len + CHUNK) // ATT_BLK) * ATT_BLK
    n_pad = tp - sp_len - N_META
    xp = jnp.concatenate([jnp.zeros((n_pad, dm), dt), meta_tokens.astype(dt), x_prompt[0]], axis=0)
    bias2 = jnp.stack([bias_tile(rel_bias, ATT_BLK, ATT_BLK, 0, causal=True),
                       bias_tile(rel_bias, ATT_BLK, ATT_BLK, -ATT_BLK)], axis=0)

    def attn_p(proj, e, lam_init):
        return attn_prompt(proj, bias2, rel_bias, *lam_vecs(e), n_pad=n_pad, lam_init=lam_init)

    zeros = lambda *s: jnp.zeros(s, F32)
    st_p = dict(b_c=zeros(n_even, bp, B_HEADS, B_DK, B_DK), b_n=zeros(n_even, bp, B_HEADS, B_DK),
                b_m=zeros(n_even, bp, B_HEADS), b_conv=zeros(n_even, bp, B_CONV - 1, 2 * HALF),
                c_re=zeros(n_odd, bp, C_GROUPS, C_STATE), c_im=zeros(n_odd, bp, C_GROUPS, C_STATE),
                d_s=zeros(n_odd, bp, D_HEADS, D_HEAD, D_HEAD), d_shift=zeros(n_odd, bp, 1, d_mu.shape[1]))
    y_p, new_p = _trunk(xp, p, bsz=bp, t=tp, seq_blk=SEQ_BLK, n_pad=n_pad, attn_fn=attn_p, st=st_p)

    tk = cache_a_k.shape[2]
    past_len = tk - N_META
    cid = lambda pos: np.where(pos < N_META, 0, 1 + (pos - N_META) // CHUNK)
    q_pos = N_META + past_len + np.arange(ts)
    k_pos = np.arange(tk + ts)
    assert (cid(k_pos)[None, :] <= cid(q_pos)[:, None]).all()
    bias_past = bias_tile(rel_bias, ts, tk, -(N_META + past_len))
    bias_new = bias_tile(rel_bias, ts, ts, 0)

    def attn_s(proj, e, lam_init):
        return attn_sample(proj, cache_a_k[e].reshape(bs, tk, HALF), cache_a_v[e].reshape(bs, tk, HALF),
                           bias_past, bias_new, *lam_vecs(e), ts=ts, lam_init=lam_init)

    st_s = dict(b_c=state_b_c, b_n=state_b_n, b_m=state_b_m, b_conv=state_b_conv, c_re=state_c_re,
                c_im=state_c_im, d_s=state_d_s, d_shift=state_d_shift)
    y_s, new_s = _trunk(x_sample.reshape(bs * ts, dm), p, bsz=bs, t=ts, seq_blk=ts, n_pad=0, attn_fn=attn_s, st=st_s)

    def pack(new, bsz, t, drop):
        kv = lambda a: a[:, drop:].reshape(bsz, t - drop, A_HEADS, A_DV)
        return (jnp.stack([kv(a) for a in new['a_k']]), jnp.stack([kv(a) for a in new['a_v']]),
                jnp.stack(new['b_c']), jnp.stack(new['b_n']), jnp.stack(new['b_m']), jnp.stack(new['b_conv']),
                jnp.stack(new['c_re']), jnp.stack(new['c_im']), jnp.stack(new['d_s']), jnp.stack(new['d_shift']))

    out_p = pack(new_p, bp, tp, n_pad)
    out_s = pack(new_s, bs, ts, 0)
    return (y_p[n_pad + N_META:][None], y_s.reshape(bs, ts, dm)) + out_p + out_s
```

```python
import functools
import math

import numpy as np
import jax
import jax.numpy as jnp
from jax import lax
from jax.experimental import pallas as pl
from jax.experimental.pallas import tpu as pltpu

F32 = jnp.float32
BF16 = jnp.bfloat16
I32 = jnp.int32

CHUNK = 64
N_META = 16
EPS = 1e-6
A_HEADS = 4
A_DH = 64
A_DV = 128
N_BUCKETS = 32
MAX_DIST = 128
B_HEADS = 4
B_DK = 128
B_CONV = 4
C_GROUP = 16
C_GROUPS = 32
C_STATE = 64
C_WIDTH = C_GROUP * C_GROUPS
D_HEAD = 64
D_HEADS = 8
D_WIDTH = D_HEAD * D_HEADS
D_GN_EPS = 64e-5
HALF = 512

NEG = -1e30
LOG2E = math.log2(math.e)
LANES = 128
VMEM_LIMIT = 56 * 1024 * 1024

ROW_TILE = 512
ATT_BLK = 512
ATT_STRIP = 64
SEQ_BLK = 128
RWKV_GRP = 16
FFN_TH = 1408


def _cparams(sem):
    return pltpu.CompilerParams(dimension_semantics=sem, vmem_limit_bytes=VMEM_LIMIT)


def _dot(a, b):
    return jnp.dot(a, b, preferred_element_type=F32)


def _dot_nt(a, b):
    return lax.dot_general(a, b, (((1,), (1,)), ((), ())), preferred_element_type=F32)


def _dot_tn(a, b):
    return lax.dot_general(a, b, (((0,), (0,)), ((), ())), preferred_element_type=F32)


def _sigmoid(x):
    return 1.0 / (1.0 + jnp.exp(-x))


def _softplus(x):
    return jnp.maximum(x, 0.0) + jnp.log1p(jnp.exp(-jnp.abs(x)))


def _rms(x, g):
    return x * lax.rsqrt(jnp.mean(x * x, axis=-1, keepdims=True) + EPS) * g


def _split3(x):
    hi = x.astype(BF16)
    r1 = x - hi.astype(F32)
    mid = r1.astype(BF16)
    lo = (r1 - mid.astype(F32)).astype(BF16)
    return hi, mid, lo


def _segsum(x, ones):
    hi, mid, lo = _split3(x)
    return _dot(hi, ones) + _dot(mid, ones) + _dot(lo, ones)


def _block_ones(n, seg):
    r = lax.broadcasted_iota(I32, (n, n), 0) // seg
    c = lax.broadcasted_iota(I32, (n, n), 1) // seg
    return jnp.where(r == c, 1.0, 0.0).astype(BF16)


def _rms_mm_kernel(x_ref, g_ref, w_ref, o_ref):
    h = _rms(x_ref[...], g_ref[...])
    o_ref[...] = _dot(h.astype(BF16), w_ref[...])


def rms_matmul(x, g, w, tm=ROW_TILE):
    m, d = x.shape
    tm = min(tm, m)
    n = w.shape[1]
    return pl.pallas_call(
        _rms_mm_kernel,
        grid=(m // tm,),
        in_specs=[pl.BlockSpec((tm, d), lambda i: (i, 0)),
                  pl.BlockSpec((1, d), lambda i: (0, 0)),
                  pl.BlockSpec((d, n), lambda i: (0, 0))],
        out_specs=pl.BlockSpec((tm, n), lambda i: (i, 0)),
        out_shape=jax.ShapeDtypeStruct((m, n), F32),
        compiler_params=_cparams(("parallel",)),
        name="rms_matmul",
    )(x, g, w)


def _outproj_kernel(x_ref, a_ref, b_ref, wa_ref, wb_ref, o_ref):
    acc = _dot(a_ref[...].astype(BF16), wa_ref[...]) + _dot(b_ref[...].astype(BF16), wb_ref[...])
    o_ref[...] = x_ref[...] + acc


def outproj(x, a, b, wa, wb, tm=ROW_TILE):
    m, d = x.shape
    tm = min(tm, m)
    k = a.shape[1]
    row = lambda i: (i, 0)
    fix = lambda i: (0, 0)
    return pl.pallas_call(
        _outproj_kernel,
        grid=(m // tm,),
        in_specs=[pl.BlockSpec((tm, d), row), pl.BlockSpec((tm, k), row), pl.BlockSpec((tm, k), row),
                  pl.BlockSpec((k, d), fix), pl.BlockSpec((k, d), fix)],
        out_specs=pl.BlockSpec((tm, d), row),
        out_shape=jax.ShapeDtypeStruct((m, d), F32),
        compiler_params=_cparams(("parallel",)),
        name="outproj",
    )(x, a, b, wa, wb)


def _ffn_kernel(x_ref, g_ref, gf_ref, w1_ref, w3_ref, w2_ref, o_ref, h_scr, *, final_norm):
    j = pl.program_id(1)

    @pl.when(j == 0)
    def _():
        x = x_ref[...]
        h_scr[...] = _rms(x, g_ref[...]).astype(BF16)
        o_ref[...] = x

    h = h_scr[...]
    a = _dot(h, w1_ref[...])
    b = _dot(h, w3_ref[...])
    u = (a * _sigmoid(a)) * b
    o_ref[...] += _dot(u.astype(BF16), w2_ref[...])

    if final_norm:
        @pl.when(j == pl.num_programs(1) - 1)
        def _():
            o_ref[...] = _rms(o_ref[...], gf_ref[...])


def ffn(x, g, w1, w3, w2, gf=None, tm=ROW_TILE, th=FFN_TH):
    m, d = x.shape
    tm = min(tm, m)
    hid = w1.shape[1]
    final_norm = gf is not None
    if gf is None:
        gf = g
    return pl.pallas_call(
        functools.partial(_ffn_kernel, final_norm=final_norm),
        grid=(m // tm, hid // th),
        in_specs=[pl.BlockSpec((tm, d), lambda i, j: (i, 0)),
                  pl.BlockSpec((1, d), lambda i, j: (0, 0)),
                  pl.BlockSpec((1, d), lambda i, j: (0, 0)),
                  pl.BlockSpec((d, th), lambda i, j: (0, j)),
                  pl.BlockSpec((d, th), lambda i, j: (0, j)),
                  pl.BlockSpec((th, d), lambda i, j: (j, 0))],
        out_specs=pl.BlockSpec((tm, d), lambda i, j: (i, 0)),
        out_shape=jax.ShapeDtypeStruct((m, d), F32),
        scratch_shapes=[pltpu.VMEM((tm, d), BF16)],
        compiler_params=_cparams(("parallel", "arbitrary")),
        name="ffn",
    )(x, g, gf, w1, w3, w2)


def _bias_kernel(rb_ref, o_ref, *, rel0, causal, scale):
    nq, nk = o_ref.shape[1], o_ref.shape[2]
    a = lax.broadcasted_iota(I32, (nq, nk), 0)
    b = lax.broadcasted_iota(I32, (nq, nk), 1)
    rel = b - a + rel0
    nb = N_BUCKETS // 2
    max_exact = nb // 2
    ret = jnp.where(rel > 0, nb, 0)
    n = jnp.abs(rel)
    nf = jnp.maximum(n, 1).astype(F32)
    large = max_exact + (jnp.log(nf / max_exact) / math.log(MAX_DIST / max_exact) * (nb - max_exact)).astype(I32)
    large = jnp.minimum(large, nb - 1)
    bucket = ret + jnp.where(n < max_exact, n, large)
    for h in range(A_HEADS):
        acc = jnp.zeros((nq, nk), F32)
        for bk in range(N_BUCKETS):
            acc = jnp.where(bucket == bk, rb_ref[bk, h] * scale, acc)
        if causal:
            acc = jnp.where((b // CHUNK) <= (a // CHUNK), acc, NEG)
        o_ref[h] = acc


def bias_tile(rel_bias, nq, nk, rel0, causal=False, scale=1.0):
    return pl.pallas_call(
        functools.partial(_bias_kernel, rel0=rel0, causal=causal, scale=scale),
        in_specs=[pl.BlockSpec(memory_space=pltpu.SMEM)],
        out_specs=pl.BlockSpec(memory_space=pltpu.VMEM),
        out_shape=jax.ShapeDtypeStruct((A_HEADS, nq, nk), F32),
        compiler_params=pltpu.CompilerParams(vmem_limit_bytes=VMEM_LIMIT),
        name="bias_tile",
    )(rel_bias)


def _lambda(lq1_ref, lk1_ref, lq2_ref, lk2_ref, lam_init):
    s1 = jnp.sum(lq1_ref[...] * lk1_ref[...], axis=-1, keepdims=True)
    s2 = jnp.sum(lq2_ref[...] * lk2_ref[...], axis=-1, keepdims=True)
    return jnp.exp(s1) - jnp.exp(s2) + lam_init


def _attn_finish(acc0, l0, acc1, l1, lam, g, lam_init):
    o = acc0 / l0 - lam * (acc1 / l1)
    return _rms(o, g) * (1.0 - lam_init)


def _attn_prompt_kernel(qi_ref, kj_ref, rb_ref, q_ref, k_ref, v_ref, bias_ref,
                        lq1_ref, lk1_ref, lq2_ref, lk2_ref, g_ref, o_ref,
                        m_scr, l_scr, acc_scr, s_scr, p_scr, al_scr, *, blk, n_pad, lam_init):
    s = pl.program_id(0)
    i = qi_ref[s]
    j = kj_ref[s]
    d = i - j
    nct = blk // LANES
    nstrip = blk // ATT_STRIP

    @pl.when(j == 0)
    def _():
        m_scr[...] = jnp.full(m_scr.shape, NEG, F32)
        l_scr[...] = jnp.zeros(l_scr.shape, F32)
        acc_scr[...] = jnp.zeros(acc_scr.shape, F32)

    def update(general):
        for h in range(A_HEADS):
            far = rb_ref[N_BUCKETS // 2 - 1, h] * LOG2E
            for mm in range(2):
                c0 = h * 2 * A_DH + mm * A_DH
                qb = (q_ref[:, c0:c0 + A_DH] * (A_DH ** -0.5 * LOG2E)).astype(BF16)
                s_scr[mm] = _dot_nt(qb, k_ref[:, c0:c0 + A_DH].astype(BF16))
            for mm in range(2):
                idx = 2 * h + mm
                for r in range(nstrip):
                    rows = slice(r * ATT_STRIP, (r + 1) * ATT_STRIP)
                    tiles = []
                    for c in range(nct):
                        cols = slice(c * LANES, (c + 1) * LANES)
                        t = s_scr[mm, rows, cols]
                        if general:
                            kpos = j * blk + c * LANES + lax.broadcasted_iota(I32, (1, LANES), 1)
                            near = bias_ref[jnp.minimum(d, 1), h, rows, cols]
                            t = t + jnp.where(d < 2, near, far) + jnp.where(kpos < n_pad, NEG, 0.0)
                        tiles.append(t)
                    mx = functools.reduce(jnp.maximum, tiles)
                    m_cur = jnp.broadcast_to(jnp.max(mx, axis=-1, keepdims=True), (ATT_STRIP, LANES))
                    if not general:
                        m_cur = m_cur + far
                    m_old = m_scr[idx, rows, :]
                    m_new = jnp.maximum(m_old, m_cur)
                    alpha = jnp.exp2(m_old - m_new)
                    m_sub = m_new if general else m_new - far
                    ps = [jnp.exp2(t - m_sub) for t in tiles]
                    l_scr[idx, rows, :] = alpha * l_scr[idx, rows, :] + functools.reduce(jnp.add, ps)
                    m_scr[idx, rows, :] = m_new
                    al_scr[mm, rows, :] = alpha
                    for c in range(nct):
                        p_scr[mm * blk + r * ATT_STRIP:mm * blk + (r + 1) * ATT_STRIP,
                              c * LANES:(c + 1) * LANES] = ps[c].astype(BF16)
            pv = _dot(p_scr[...], v_ref[:, h * A_DV:(h + 1) * A_DV].astype(BF16))
            for mm in range(2):
                idx = 2 * h + mm
                acc_scr[idx] = al_scr[mm] * acc_scr[idx] + pv[mm * blk:(mm + 1) * blk]

    is_far = jnp.logical_and(d >= 2, j > 0)

    @pl.when(is_far)
    def _():
        update(False)

    @pl.when(jnp.logical_not(is_far))
    def _():
        update(True)

    @pl.when(d == 0)
    def _():
        lam = _lambda(lq1_ref, lk1_ref, lq2_ref, lk2_ref, lam_init)
        rows = i * blk + lax.broadcasted_iota(I32, (blk, 1), 0)
        valid = rows >= n_pad
        for h in range(A_HEADS):
            l0 = jnp.sum(l_scr[2 * h], axis=-1, keepdims=True)
            l1 = jnp.sum(l_scr[2 * h + 1], axis=-1, keepdims=True)
            y = _attn_finish(acc_scr[2 * h], l0, acc_scr[2 * h + 1], l1, lam, g_ref[...], lam_init)
            o_ref[:, h * A_DV:(h + 1) * A_DV] = jnp.where(valid, y, 0.0)


def attn_prompt(proj, bias2, rel_bias, lq1, lk1, lq2, lk2, g, *, n_pad, lam_init, blk=ATT_BLK):
    t = proj.shape[0]
    nb = t // blk
    qi = np.array([i for i in range(nb) for _ in range(i + 1)], np.int32)
    kj = np.array([j for i in range(nb) for j in range(i + 1)], np.int32)
    vec = lambda n: pl.BlockSpec((1, n), lambda s, qi, kj: (0, 0))
    grid_spec = pltpu.PrefetchScalarGridSpec(
        num_scalar_prefetch=2,
        grid=(len(qi),),
        in_specs=[pl.BlockSpec(memory_space=pltpu.SMEM),
                  pl.BlockSpec((blk, HALF), lambda s, qi, kj: (qi[s], 0)),
                  pl.BlockSpec((blk, HALF), lambda s, qi, kj: (kj[s], 1)),
                  pl.BlockSpec((blk, HALF), lambda s, qi, kj: (kj[s], 2)),
                  pl.BlockSpec((2, A_HEADS, blk, blk), lambda s, qi, kj: (0, 0, 0, 0)),
                  vec(A_DH), vec(A_DH), vec(A_DH), vec(A_DH), vec(A_DV)],
        out_specs=pl.BlockSpec((blk, HALF), lambda s, qi, kj: (qi[s], 0)),
        scratch_shapes=[pltpu.VMEM((2 * A_HEADS, blk, LANES), F32),
                        pltpu.VMEM((2 * A_HEADS, blk, LANES), F32),
                        pltpu.VMEM((2 * A_HEADS, blk, A_DV), F32),
                        pltpu.VMEM((2, blk, blk), F32),
                        pltpu.VMEM((2 * blk, blk), BF16),
                        pltpu.VMEM((2, blk, LANES), F32)],
    )
    assert n_pad <= blk and blk >= MAX_DIST
    return pl.pallas_call(
        functools.partial(_attn_prompt_kernel, blk=blk, n_pad=n_pad, lam_init=lam_init),
        grid_spec=grid_spec,
        out_shape=jax.ShapeDtypeStruct((t, HALF), F32),
        compiler_params=_cparams(("arbitrary",)),
        name="attn_prompt",
    )(jnp.asarray(qi), jnp.asarray(kj), rel_bias, proj, proj, proj, bias2, lq1, lk1, lq2, lk2, g)


def _attn_sample_kernel(q_ref, kn_ref, vn_ref, kc_ref, vc_ref, bp_ref, bn_ref,
                        lq1_ref, lk1_ref, lq2_ref, lk2_ref, g_ref, o_ref, *, lam_init):
    lam = _lambda(lq1_ref, lk1_ref, lq2_ref, lk2_ref, lam_init)
    for h in range(A_HEADS):
        vsl = slice(h * A_DV, (h + 1) * A_DV)
        vp = vc_ref[0, :, vsl].astype(BF16)
        vn = vn_ref[:, vsl].astype(BF16)
        outs = []
        for mm in range(2):
            c0 = h * 2 * A_DH + mm * A_DH
            qb = (q_ref[:, c0:c0 + A_DH] * (A_DH ** -0.5)).astype(BF16)
            sp = _dot_nt(qb, kc_ref[0, :, c0:c0 + A_DH].astype(BF16)) + bp_ref[h]
            sn = _dot_nt(qb, kn_ref[:, c0:c0 + A_DH].astype(BF16)) + bn_ref[h]
            mx = jnp.maximum(jnp.max(sp, axis=-1, keepdims=True), jnp.max(sn, axis=-1, keepdims=True))
            pp = jnp.exp(sp - mx)
            pn = jnp.exp(sn - mx)
            l = jnp.sum(pp, axis=-1, keepdims=True) + jnp.sum(pn, axis=-1, keepdims=True)
            acc = _dot(pp.astype(BF16), vp) + _dot(pn.astype(BF16), vn)
            outs.append((acc, l))
        o_ref[:, vsl] = _attn_finish(outs[0][0], outs[0][1], outs[1][0], outs[1][1], lam, g_ref[...], lam_init)


def attn_sample(proj, kc, vc, bias_past, bias_new, lq1, lk1, lq2, lk2, g, *, ts, lam_init):
    bsz, tk, _ = kc.shape
    vec = lambda n: pl.BlockSpec((1, n), lambda b: (0, 0))
    return pl.pallas_call(
        functools.partial(_attn_sample_kernel, lam_init=lam_init),
        grid=(bsz,),
        in_specs=[pl.BlockSpec((ts, HALF), lambda b: (b, 0)),
                  pl.BlockSpec((ts, HALF), lambda b: (b, 1)),
                  pl.BlockSpec((ts, HALF), lambda b: (b, 2)),
                  pl.BlockSpec((1, tk, HALF), lambda b: (b, 0, 0)),
                  pl.BlockSpec((1, tk, HALF), lambda b: (b, 0, 0)),
                  pl.BlockSpec((A_HEADS, ts, tk), lambda b: (0, 0, 0)),
                  pl.BlockSpec((A_HEADS, ts, ts), lambda b: (0, 0, 0)),
                  vec(A_DH), vec(A_DH), vec(A_DH), vec(A_DH), vec(A_DV)],
        out_specs=pl.BlockSpec((ts, HALF), lambda b: (b, 0)),
        out_shape=jax.ShapeDtypeStruct((bsz * ts, HALF), F32),
        compiler_params=_cparams(("parallel",)),
        name="attn_sample",
    )(proj, proj, proj, kc, vc, bias_past, bias_new, lq1, lk1, lq2, lk2, g)


def _mlstm_kernel(bq_ref, bk_ref, bv_ref, bo_ref, gc_ref, gr_ref, c0_ref, n0_ref, m0_ref, conv0_ref,
                  cw_ref, cb_ref, gbc_ref, gbr_ref, bn_ref,
                  h_ref, c_out, n_out, m_out,
                  xbuf, c_scr, n_scr, m_scr, *, ln, n_pad):
    tb = pl.program_id(1)
    nt = pl.num_programs(1)
    halo = B_CONV - 1
    base = 8 - halo

    @pl.when(tb == 0)
    def _():
        xbuf[base:8, :] = conv0_ref[0]
        c_scr[...] = c0_ref[0]
        n_scr[...] = n0_ref[0]
        for h in range(B_HEADS):
            m_scr[h] = m0_ref[0, :, h:h + 1]

    xbuf[8:8 + ln, 0:HALF] = bq_ref[...]
    xbuf[8:8 + ln, HALF:2 * HALF] = bk_ref[...]
    conv = cb_ref[...] + cw_ref[0:1, :] * xbuf[base:base + ln, :]
    for jj in range(1, B_CONV):
        conv = conv + cw_ref[jj:jj + 1, :] * xbuf[base + jj:base + jj + ln, :]
    xbuf[base:8, :] = xbuf[8 + ln - halo:8 + ln, :]
    conv = conv * _sigmoid(conv)
    q_all = conv[:, 0:HALF] * (B_DK ** -0.5)
    k_all = conv[:, HALF:2 * HALF]

    rows = tb * ln + lax.broadcasted_iota(I32, (ln, 1), 0)
    cols = tb * ln + lax.broadcasted_iota(I32, (1, ln), 1)
    valid_c = rows >= n_pad
    valid_r = cols >= n_pad

    gc = gc_ref[...] + gbc_ref[...]
    gr = gr_ref[0] + gbr_ref[...]
    li_c = jnp.where(valid_c, gc, NEG)
    li_r = jnp.where(valid_r, gr, NEG)
    lf_c = jnp.where(valid_c, -_softplus(-gc), 0.0)
    lf_r = jnp.where(valid_r, -_softplus(-gr), 0.0)

    ri = lax.broadcasted_iota(I32, (ln, ln), 0)
    ci = lax.broadcasted_iota(I32, (ln, ln), 1)
    tril = ri >= ci
    tril_f = jnp.where(tril, 1.0, 0.0)
    triu_f = jnp.where(ri <= ci, 1.0, 0.0)
    b_c = jnp.dot(tril_f, lf_c, preferred_element_type=F32, precision=lax.Precision.HIGHEST)
    b_r = jnp.dot(lf_r, triu_f, preferred_element_type=F32, precision=lax.Precision.HIGHEST)

    for h in range(B_HEADS):
        sl = slice(h * B_DK, (h + 1) * B_DK)
        qh = q_all[:, sl].astype(BF16)
        kh = k_all[:, sl]
        vh = bv_ref[:, sl].astype(BF16)
        c = c_scr[h]
        n = n_scr[h]
        m = m_scr[h]
        bc = b_c[:, B_HEADS + h:B_HEADS + h + 1]
        br = b_r[B_HEADS + h:B_HEADS + h + 1, :]
        inter = bc + m
        dmat = jnp.where(tril, bc - br + li_r[h:h + 1, :], NEG)
        mt = jnp.maximum(inter, jnp.max(dmat, axis=-1, keepdims=True))
        w_inter = jnp.exp(inter - mt)
        s = _dot_nt(qh, kh.astype(BF16)) * jnp.exp(dmat - mt)
        num = w_inter * _dot(qh, c.astype(BF16)) + _dot(s.astype(BF16), vh)
        qn = jnp.sum(qh.astype(F32) * n, axis=-1, keepdims=True)
        den = w_inter * qn + jnp.sum(s, axis=-1, keepdims=True)
        hh = num / jnp.maximum(jnp.abs(den), jnp.exp(-mt))
        b_last = bc[ln - 1:ln, :]
        g = b_last - bc + li_c[:, h:h + 1]
        m_new = jnp.maximum(b_last + m, jnp.max(g, axis=0, keepdims=True))
        decay = jnp.exp(b_last + m - m_new)
        wk = (jnp.exp(g - m_new) * kh)
        c_scr[h] = decay * c + _dot_tn(wk.astype(BF16), vh)
        n_scr[h] = decay * n + jnp.sum(wk, axis=0, keepdims=True)
        m_scr[h] = m_new
        hn = _rms(hh, bn_ref[:, sl]) * _sigmoid(bo_ref[:, sl])
        h_ref[:, sl] = jnp.where(valid_c, hn, 0.0)

    @pl.when(tb == nt - 1)
    def _():
        c_out[0] = c_scr[...]
        n_out[0] = n_scr[...]
        lane = lax.broadcasted_iota(I32, (1, B_HEADS), 1)
        mrow = jnp.zeros((1, B_HEADS), F32)
        for h in range(B_HEADS):
            mrow = jnp.where(lane == h, m_scr[h], mrow)
        m_out[0] = mrow


def mlstm(proj, gates_r, c0, n0, m0, conv0, cw, cb, igb, fgb, bnorm, *, bsz, t, ln, n_pad):
    nt = t // ln
    gb = jnp.concatenate([igb, fgb])
    gw = proj.shape[1] // LANES - 1
    row = lambda c: (lambda b, i: (b * nt + i, c))
    fix2 = lambda b, i: (0, 0)
    out_shapes = (jax.ShapeDtypeStruct((bsz * t, HALF), F32),
                  jax.ShapeDtypeStruct((bsz, B_HEADS, B_DK, B_DK), F32),
                  jax.ShapeDtypeStruct((bsz, B_HEADS, 1, B_DK), F32),
                  jax.ShapeDtypeStruct((bsz, 1, B_HEADS), F32))
    return pl.pallas_call(
        functools.partial(_mlstm_kernel, ln=ln, n_pad=n_pad),
        grid=(bsz, nt),
        in_specs=[pl.BlockSpec((ln, HALF), row(3)), pl.BlockSpec((ln, HALF), row(4)),
                  pl.BlockSpec((ln, HALF), row(5)), pl.BlockSpec((ln, HALF), row(6)),
                  pl.BlockSpec((ln, LANES), row(gw)),
                  pl.BlockSpec((1, 2 * B_HEADS, ln), lambda b, i: (b * nt + i, 0, 0)),
                  pl.BlockSpec((1, B_HEADS, B_DK, B_DK), lambda b, i: (b, 0, 0, 0)),
                  pl.BlockSpec((1, B_HEADS, 1, B_DK), lambda b, i: (b, 0, 0, 0)),
                  pl.BlockSpec((1, 1, B_HEADS), lambda b, i: (b, 0, 0)),
                  pl.BlockSpec((1, B_CONV - 1, 2 * HALF), lambda b, i: (b, 0, 0)),
                  pl.BlockSpec((B_CONV, 2 * HALF), fix2), pl.BlockSpec((1, 2 * HALF), fix2),
                  pl.BlockSpec((1, LANES), fix2), pl.BlockSpec((2 * B_HEADS, 1), fix2),
                  pl.BlockSpec((1, HALF), fix2)],
        out_specs=(pl.BlockSpec((ln, HALF), lambda b, i: (b * nt + i, 0)),
                   pl.BlockSpec((1, B_HEADS, B_DK, B_DK), lambda b, i: (b, 0, 0, 0)),
                   pl.BlockSpec((1, B_HEADS, 1, B_DK), lambda b, i: (b, 0, 0, 0)),
                   pl.BlockSpec((1, 1, B_HEADS), lambda b, i: (b, 0, 0))),
        out_shape=out_shapes,
        scratch_shapes=[pltpu.VMEM((ln + 8, 2 * HALF), F32),
                        pltpu.VMEM((B_HEADS, B_DK, B_DK), F32),
                        pltpu.VMEM((B_HEADS, 1, B_DK), F32),
                        pltpu.VMEM((B_HEADS, 1, 1), F32)],
        compiler_params=_cparams(("parallel", "arbitrary")),
        name="mlstm",
    )(proj, proj, proj, proj, proj, gates_r, c0, n0, m0, conv0, cw, cb,
      jnp.pad(gb, (0, LANES - 2 * B_HEADS))[None], gb[:, None], bnorm)


def _s5_param_kernel(lr_ref, li_ref, ldt_ref, bre_ref, bim_ref, ar_ref, ai_ref, bbr_ref, bbi_ref):
    lr = lr_ref[...]
    li = li_ref[...]
    dt = jnp.exp(ldt_ref[...])
    mag = jnp.exp(lr * dt)
    ar = mag * jnp.cos(li * dt)
    ai = mag * jnp.sin(li * dt)
    den = lr * lr + li * li
    cr = ((ar - 1.0) * lr + ai * li) / den
    ci = (ai * lr - (ar - 1.0) * li) / den
    ar_ref[...] = ar
    ai_ref[...] = ai
    nr, nc = bre_ref.shape
    same = (lax.broadcasted_iota(I32, (nr, nc), 0) // C_GROUP) == (lax.broadcasted_iota(I32, (nr, nc), 1) // C_STATE)
    br = bre_ref[...]
    bi = bim_ref[...]
    bbr_ref[...] = jnp.where(same, cr * br - ci * bi, 0.0).astype(BF16)
    bbi_ref[...] = jnp.where(same, cr * bi + ci * br, 0.0).astype(BF16)


def s5_params(lr, li, ldt, bre_rep, bim_rep):
    ns = lr.shape[1]
    vm = pl.BlockSpec(memory_space=pltpu.VMEM)
    return pl.pallas_call(
        _s5_param_kernel,
        in_specs=[vm] * 5,
        out_specs=(vm, vm, vm, vm),
        out_shape=(jax.ShapeDtypeStruct((1, ns), F32), jax.ShapeDtypeStruct((1, ns), F32),
                   jax.ShapeDtypeStruct(bre_rep.shape, BF16), jax.ShapeDtypeStruct(bre_rep.shape, BF16)),
        compiler_params=pltpu.CompilerParams(vmem_limit_bytes=VMEM_LIMIT),
        name="s5_params",
    )(lr, li, ldt, bre_rep, bim_rep)


def _s5_kernel(u_ref, ar_ref, ai_ref, bbr_ref, bbi_ref, cre_ref, cim_ref, d_ref, wg_ref, x0r_ref, x0i_ref,
               o_ref, xr_out, xi_out, sr_scr, si_scr, bur, bui, xra, xia, *, tb_len, n_pad):
    tb = pl.program_id(1)
    nt = pl.num_programs(1)

    @pl.when(tb == 0)
    def _():
        sr_scr[...] = x0r_ref[0]
        si_scr[...] = x0i_ref[0]

    u = u_ref[...]
    ub = u.astype(BF16)
    bur[...] = _dot(ub, bbr_ref[...])
    bui[...] = _dot(ub, bbi_ref[...])
    ar = ar_ref[...]
    ai = ai_ref[...]

    sub = lax.broadcasted_iota(I32, (8, ar.shape[1]), 0)

    def body(grp, carry):
        xr, xi = carry
        base = pl.multiple_of(grp * 8, 8)
        br8 = bur[pl.ds(base, 8), :]
        bi8 = bui[pl.ds(base, 8), :]
        xr8 = jnp.zeros(br8.shape, F32)
        xi8 = jnp.zeros(br8.shape, F32)
        for i in range(8):
            nxr = ar * xr - ai * xi + br8[i:i + 1, :]
            nxi = ar * xi + ai * xr + bi8[i:i + 1, :]
            xr, xi = nxr, nxi
            xr8 = jnp.where(sub == i, xr, xr8)
            xi8 = jnp.where(sub == i, xi, xi8)
        xra[pl.ds(base, 8), :] = xr8
        xia[pl.ds(base, 8), :] = xi8
        return xr, xi

    xr, xi = lax.fori_loop(0, tb_len // 8, body, (sr_scr[...], si_scr[...]))
    sr_scr[...] = xr
    si_scr[...] = xi

    y = _dot(xra[...].astype(BF16), cre_ref[...]) - _dot(xia[...].astype(BF16), cim_ref[...]) + d_ref[...] * u
    yg = 0.5 * y * (1.0 + jnp.tanh(math.sqrt(2.0 / math.pi) * (y + 0.044715 * (y * y * y))))
    oc = yg * _sigmoid(_dot(yg.astype(BF16), wg_ref[...]))
    rows = tb * tb_len + lax.broadcasted_iota(I32, (tb_len, 1), 0)
    o_ref[...] = jnp.where(rows >= n_pad, oc, 0.0)

    @pl.when(tb == nt - 1)
    def _():
        xr_out[0] = xr
        xi_out[0] = xi


def s5(proj, ar, ai, bbr, bbi, cre, cim, dskip, wglu, x0r, x0i, *, bsz, t, tb_len, n_pad):
    nt = t // tb_len
    ns = ar.shape[1]
    fix2 = lambda b, i: (0, 0)
    st = pl.BlockSpec((1, 1, ns), lambda b, i: (b, 0, 0))
    return pl.pallas_call(
        functools.partial(_s5_kernel, tb_len=tb_len, n_pad=n_pad),
        grid=(bsz, nt),
        in_specs=[pl.BlockSpec((tb_len, HALF), lambda b, i: (b * nt + i, 0)),
                  pl.BlockSpec((1, ns), fix2), pl.BlockSpec((1, ns), fix2),
                  pl.BlockSpec((HALF, ns), fix2), pl.BlockSpec((HALF, ns), fix2),
                  pl.BlockSpec((ns, HALF), fix2), pl.BlockSpec((ns, HALF), fix2),
                  pl.BlockSpec((1, HALF), fix2), pl.BlockSpec((HALF, HALF), fix2), st, st],
        out_specs=(pl.BlockSpec((tb_len, HALF), lambda b, i: (b * nt + i, 0)), st, st),
        out_shape=(jax.ShapeDtypeStruct((bsz * t, HALF), F32),
                   jax.ShapeDtypeStruct((bsz, 1, ns), F32), jax.ShapeDtypeStruct((bsz, 1, ns), F32)),
        scratch_shapes=[pltpu.VMEM((1, ns), F32), pltpu.VMEM((1, ns), F32),
                        pltpu.VMEM((tb_len, ns), F32), pltpu.VMEM((tb_len, ns), F32),
                        pltpu.VMEM((tb_len, ns), F32), pltpu.VMEM((tb_len, ns), F32)],
        compiler_params=_cparams(("parallel", "arbitrary")),
        name="s5",
    )(proj, ar, ai, bbr, bbi, cre, cim, dskip, wglu, x0r, x0i)


def _rwkv_kernel(r_ref, k_ref, v_ref, lo_ref, sh0_ref, s0_ref,
                 mu_ref, w0_ref, ww2_ref, a0_ref, wa2_ref, wg2_ref, kk_ref, ka_ref, rk_ref, lng_ref, lnb_ref,
                 o_ref, s_out,
                 xbuf, s_scr, w_scr, nkk_scr, b_scr, k_scr, r_scr, v_scr, c_scr, be_scr, ga_scr, y_scr,
                 *, tb_len, n_pad):
    tb = pl.program_id(1)
    nt = pl.num_programs(1)
    npair = D_HEADS // 2
    ncol = xbuf.shape[1]

    @pl.when(tb == 0)
    def _():
        xbuf[7:8, :] = sh0_ref[0]
        s_scr[...] = s0_ref[0]

    xbuf[8:8 + tb_len, 0:D_WIDTH] = r_ref[...]
    xbuf[8:8 + tb_len, D_WIDTH:2 * D_WIDTH] = k_ref[...]
    xbuf[8:8 + tb_len, 2 * D_WIDTH:3 * D_WIDTH] = v_ref[...]
    xbuf[8:8 + tb_len, 3 * D_WIDTH:ncol] = lo_ref[...]
    cur = xbuf[8:8 + tb_len, :]
    prev = xbuf[7:7 + tb_len, :]
    xbuf[7:8, :] = xbuf[7 + tb_len:8 + tb_len, :]
    xm = cur + mu_ref[...] * (prev - cur)
    r = xm[:, 0:D_WIDTH]
    k = xm[:, D_WIDTH:2 * D_WIDTH]
    v = xm[:, 2 * D_WIDTH:3 * D_WIDTH]
    c0 = 3 * D_WIDTH
    wlo = xm[:, c0:c0 + 64]
    alo = xm[:, c0 + 64:c0 + 128]
    glo = xm[:, c0 + 128:c0 + 256]

    w_raw = w0_ref[...] + _dot(jnp.tanh(wlo).astype(BF16), ww2_ref[...])
    decay = jnp.exp(-jnp.exp(-_softplus(-w_raw) - 0.5))
    a = _sigmoid(a0_ref[...] + _dot(alo.astype(BF16), wa2_ref[...]))
    g = _dot(_sigmoid(glo).astype(BF16), wg2_ref[...])

    ones_h = _block_ones(D_WIDTH, D_HEAD)
    kk = k * kk_ref[...]
    kk = kk / jnp.maximum(jnp.sqrt(_segsum(kk * kk, ones_h)), 1e-12)
    k2 = k * (1.0 + (a - 1.0) * ka_ref[...])

    nkk = -kk
    bb = kk * a
    w_scr[...] = decay
    nkk_scr[...] = nkk
    b_scr[...] = bb
    k_scr[...] = k2
    r_scr[...] = r
    v_scr[...] = v
    c_scr[...] = pltpu.roll(decay, 1, 0) * nkk
    be_scr[...] = _segsum(pltpu.roll(bb, 1, 0) * nkk, ones_h)
    ga_scr[...] = _segsum(pltpu.roll(k2, 1, 0) * nkk, ones_h)

    ones_p = _block_ones(LANES, D_HEAD)
    eye2 = jnp.where((lax.broadcasted_iota(I32, (D_HEAD, LANES), 1) % D_HEAD)
                     == lax.broadcasted_iota(I32, (D_HEAD, LANES), 0), 1.0, 0.0)

    sub = lax.broadcasted_iota(I32, (RWKV_GRP, LANES), 0)
    rows_p = [slice(p * D_HEAD, (p + 1) * D_HEAD) for p in range(npair)]
    nrow = npair * D_HEAD
    pairs = range(npair)

    def split2(x):
        hi = x.astype(BF16)
        return [hi, (x - hi.astype(F32)).astype(BF16)]

    def stack(parts):
        return jnp.concatenate(parts, axis=0)

    def pieces(res, start, n):
        return res[start:start + n] + res[start + n:start + 2 * n]

    def y_rows(yt, ycol, i):
        return [jnp.where(sub == i, jnp.sum(ycol[rows_p[p]] * eye2, axis=0, keepdims=True), yt[p]) for p in pairs]

    def body(grp, state):
        base = pl.multiple_of(grp * RWKV_GRP, RWKV_GRP)
        tile = lambda scr: [scr[pl.ds(base, RWKV_GRP), p * LANES:(p + 1) * LANES] for p in pairs]
        nkk_t, c_t, be_t, ga_t = tile(nkk_scr), tile(c_scr), tile(be_scr), tile(ga_scr)
        w_t, b_t, k_t, r_t, v_t = tile(w_scr), tile(b_scr), tile(k_scr), tile(r_scr), tile(v_scr)
        vh = [x.astype(BF16).astype(F32) for x in v_t]
        vl = [x - h for x, h in zip(v_t, vh)]

        def vcol_lhs(i):
            return [stack([vh[p][i:i + 1] * eye2 for p in pairs]).astype(BF16),
                    stack([vl[p][i:i + 1] * eye2 for p in pairs]).astype(BF16)]

        yt = [jnp.zeros((RWKV_GRP, LANES), F32) for _ in pairs]
        sp = list(state)
        res = _dot(stack(vcol_lhs(0) + vcol_lhs(1)), ones_p)
        vc = [pieces(res, 0, nrow), pieces(res, 2 * nrow, nrow)]
        yprods = None
        npairs_t = RWKV_GRP // 2
        for q in range(npairs_t):
            t0, t1 = 2 * q, 2 * q + 1
            r0, r1 = slice(t0, t0 + 1), slice(t1, t1 + 1)
            chain = stack([sp[p] * nkk_t[p][r0] for p in pairs] + [sp[p] * c_t[p][r1] for p in pairs])
            lhs = split2(chain)
            if q + 1 < npairs_t:
                lhs = lhs + vcol_lhs(t0 + 2) + vcol_lhs(t1 + 2)
            if yprods is not None:
                lhs = lhs + split2(yprods)
            res = _dot(stack(lhs), ones_p)
            both = pieces(res, 0, 2 * nrow)
            sa0, tmp = both[0:nrow], both[nrow:2 * nrow]
            off = 4 * nrow
            if q + 1 < npairs_t:
                vc_next = [pieces(res, off, nrow), pieces(res, off + 2 * nrow, nrow)]
                off += 4 * nrow
            if yprods is not None:
                ycols = pieces(res, off, 2 * nrow)
                yt = y_rows(yt, ycols[0:nrow], t0 - 2)
                yt = y_rows(yt, ycols[nrow:2 * nrow], t1 - 2)
            s0 = [sp[p] * w_t[p][r0] + sa0[rows_p[p]] * b_t[p][r0] + vc[0][rows_p[p]] * k_t[p][r0] for p in pairs]
            sa1 = [tmp[rows_p[p]] + sa0[rows_p[p]] * be_t[p][r1] + vc[0][rows_p[p]] * ga_t[p][r1] for p in pairs]
            sp = [s0[p] * w_t[p][r1] + sa1[p] * b_t[p][r1] + vc[1][rows_p[p]] * k_t[p][r1] for p in pairs]
            yprods = stack([s0[p] * r_t[p][r0] for p in pairs] + [sp[p] * r_t[p][r1] for p in pairs])
            if q + 1 < npairs_t:
                vc = vc_next
        ycols = pieces(_dot(stack(split2(yprods)), ones_p), 0, 2 * nrow)
        yt = y_rows(yt, ycols[0:nrow], RWKV_GRP - 2)
        yt = y_rows(yt, ycols[nrow:2 * nrow], RWKV_GRP - 1)
        for p in pairs:
            y_scr[pl.ds(base, RWKV_GRP), p * LANES:(p + 1) * LANES] = yt[p]
        return tuple(sp)

    state = lax.fori_loop(0, tb_len // RWKV_GRP, body, tuple(s_scr[p] for p in pairs))
    for p in pairs:
        s_scr[p] = state[p]

    y = y_scr[...]
    inv = 1.0 / D_HEAD
    mean = _segsum(y, ones_h) * inv
    yc = y - mean
    var = _segsum(yc * yc, ones_h) * inv
    y = yc * lax.rsqrt(var + D_GN_EPS) * lng_ref[...] + lnb_ref[...]
    y = y + _segsum(r * k2 * rk_ref[...], ones_h) * v
    rows = tb * tb_len + lax.broadcasted_iota(I32, (tb_len, 1), 0)
    o_ref[...] = jnp.where(rows >= n_pad, y * g, 0.0)

    @pl.when(tb == nt - 1)
    def _():
        s_out[0] = s_scr[...]


def rwkv(proj, sh0, s0, mu, w0, ww2, a0, wa2, wg2, kkp, kap, rkp, lng, lnb, *, bsz, t, tb_len, n_pad):
    nt = t // tb_len
    npair = D_HEADS // 2
    ncol = mu.shape[1]
    nlo = ncol - 3 * D_WIDTH
    row = lambda c: (lambda b, i: (b * nt + i, c))
    fix2 = lambda b, i: (0, 0)
    vec = pl.BlockSpec((1, D_WIDTH), fix2)
    st = pl.BlockSpec((1, npair, D_HEAD, LANES), lambda b, i: (b, 0, 0, 0))
    big = lambda: pltpu.VMEM((tb_len, D_WIDTH), F32)
    return pl.pallas_call(
        functools.partial(_rwkv_kernel, tb_len=tb_len, n_pad=n_pad),
        grid=(bsz, nt),
        in_specs=[pl.BlockSpec((tb_len, D_WIDTH), row(1)), pl.BlockSpec((tb_len, D_WIDTH), row(2)),
                  pl.BlockSpec((tb_len, D_WIDTH), row(3)),
                  pl.BlockSpec((tb_len, nlo), row(4 * D_WIDTH // nlo)),
                  pl.BlockSpec((1, 1, ncol), lambda b, i: (b, 0, 0)), st,
                  pl.BlockSpec((1, ncol), fix2), vec,
                  pl.BlockSpec(ww2.shape, fix2), vec, pl.BlockSpec(wa2.shape, fix2), pl.BlockSpec(wg2.shape, fix2),
                  vec, vec, vec, vec, vec],
        out_specs=(pl.BlockSpec((tb_len, D_WIDTH), lambda b, i: (b * nt + i, 0)), st),
        out_shape=(jax.ShapeDtypeStruct((bsz * t, D_WIDTH), F32),
                   jax.ShapeDtypeStruct((bsz, npair, D_HEAD, LANES), F32)),
        scratch_shapes=[pltpu.VMEM((tb_len + 8, ncol), F32), pltpu.VMEM((npair, D_HEAD, LANES), F32),
                        big(), big(), big(), big(), big(), big(), big(), big(), big(), big()],
        compiler_params=_cparams(("parallel", "arbitrary")),
        name="rwkv7",
    )(proj, proj, proj, proj, sh0, s0, mu, w0, ww2, a0, wa2, wg2, kkp, kap, rkp, lng, lnb)


def _pad_cols(w, mult=LANES):
    n = w.shape[1]
    return jnp.pad(w, ((0, 0), (0, (-n) % mult)))


def _pairs_from_heads(s):
    b = s.shape[0]
    return s.reshape(b, D_HEADS // 2, 2, D_HEAD, D_HEAD).transpose(0, 1, 3, 2, 4).reshape(b, D_HEADS // 2, D_HEAD, LANES)


def _heads_from_pairs(s):
    b = s.shape[0]
    return s.reshape(b, D_HEADS // 2, D_HEAD, 2, D_HEAD).transpose(0, 1, 3, 2, 4).reshape(b, D_HEADS, D_HEAD, D_HEAD)


def _trunk(x, p, *, bsz, t, seq_blk, n_pad, attn_fn, st):
    depth = p['norm_mix'].shape[0]
    new = {k: [] for k in ('a_k', 'a_v', 'b_c', 'b_n', 'b_m', 'b_conv', 'c_re', 'c_im', 'd_s', 'd_shift')}
    nt = t // seq_blk
    y = None
    for layer in range(depth):
        g_mix = p['norm_mix'][layer][None]
        if layer % 2 == 0:
            e = layer // 2
            proj = rms_matmul(x, g_mix, p['ev_w_in'][e])
            lam_init = 0.8 - 0.6 * math.exp(-0.3 * layer)
            oa = attn_fn(proj, e, lam_init)
            gcols = proj[:, 7 * HALF:7 * HALF + 2 * B_HEADS]
            gates_r = gcols.reshape(bsz * nt, seq_blk, 2 * B_HEADS).transpose(0, 2, 1)
            hb, bc, bn, bm = mlstm(proj, gates_r, st['b_c'][e], st['b_n'][e][:, :, None, :], st['b_m'][e][:, None, :],
                                   st['b_conv'][e], p['b_conv_w'][e], p['b_conv_b'][e][None],
                                   p['b_ig_bias'][e], p['b_fg_bias'][e], p['b_norm'][e][None],
                                   bsz=bsz, t=t, ln=seq_blk, n_pad=n_pad)
            p3 = proj.reshape(bsz, t, -1)
            new['a_k'].append(p3[:, :, HALF:2 * HALF])
            new['a_v'].append(p3[:, :, 2 * HALF:3 * HALF])
            new['b_c'].append(bc)
            new['b_n'].append(bn[:, :, 0, :])
            new['b_m'].append(bm[:, 0, :])
            new['b_conv'].append(p3[:, t - (B_CONV - 1):, 3 * HALF:5 * HALF])
            x = outproj(x, oa, hb, p['ev_w_out'][e][:HALF], p['ev_w_out'][e][HALF:])
        else:
            o = layer // 2
            proj = rms_matmul(x, g_mix, p['od_w_in'][o])
            sp = p['s5'][o]
            oc, cre, cim = s5(proj, sp['ar'], sp['ai'], sp['bbr'], sp['bbi'], sp['cre'], sp['cim'],
                              p['c_d'][o][None], p['c_w_glu'][o],
                              st['c_re'][o].reshape(bsz, 1, -1), st['c_im'][o].reshape(bsz, 1, -1),
                              bsz=bsz, t=t, tb_len=seq_blk, n_pad=n_pad)
            od, ds = rwkv(proj, st['d_shift'][o], _pairs_from_heads(st['d_s'][o]),
                          p['d_mu'][o][None], p['d_w0'][o][None], p['d_w_w2'][o], p['d_a0'][o][None],
                          p['d_w_a2'][o], p['d_w_g2'][o], p['d_k_k'][o][None], p['d_k_a'][o][None],
                          p['d_r_k'][o][None], p['d_ln_g'][o][None], p['d_ln_b'][o][None],
                          bsz=bsz, t=t, tb_len=seq_blk, n_pad=n_pad)
            new['c_re'].append(cre.reshape(bsz, C_GROUPS, C_STATE))
            new['c_im'].append(cim.reshape(bsz, C_GROUPS, C_STATE))
            new['d_s'].append(_heads_from_pairs(ds))
            new['d_shift'].append(proj.reshape(bsz, t, -1)[:, -1:, HALF:])
            x = outproj(x, oc, od, p['od_w_out'][o][:HALF], p['od_w_out'][o][HALF:])
        gf = p['norm_final'][None] if layer == depth - 1 else None
        x = ffn(x, p['norm_ffn'][layer][None], p['ffn_w1'][layer], p['ffn_w3'][layer], p['ffn_w2'][layer], gf=gf)
    return x, new


def kernel(x_prompt, x_sample, cache_a_k, cache_a_v, state_b_c, state_b_n, state_b_m, state_b_conv, state_c_re, state_c_im, state_d_s, state_d_shift, meta_tokens, rel_bias, norm_mix, norm_ffn, norm_final, ev_w_in, ev_w_out, a_lq1, a_lk1, a_lq2, a_lk2, a_subln, b_conv_w, b_conv_b, b_ig_bias, b_fg_bias, b_norm, od_w_in, od_w_out, c_lam_re, c_lam_im, c_log_dt, c_b_re, c_b_im, c_c_re, c_c_im, c_d, c_w_glu, d_mu, d_w0, d_w_w2, d_a0, d_w_a2, d_w_g2, d_k_k, d_k_a, d_r_k, d_ln_g, d_ln_b, ffn_w1, ffn_w3, ffn_w2):
    bp, sp_len, dm = x_prompt.shape
    bs, ts, _ = x_sample.shape
    n_even, n_odd = ev_w_in.shape[0], od_w_in.shape[0]
    assert bp == 1 and sp_len % CHUNK == 0 and ts % RWKV_GRP == 0 and ts >= B_CONV - 1
    dt = x_prompt.dtype

    ns = C_GROUPS * C_STATE
    s5p = []
    for o in range(n_odd):
        bre = jnp.tile(c_b_re[o].transpose(2, 0, 1).reshape(C_GROUP, ns), (C_GROUPS, 1))
        bim = jnp.tile(c_b_im[o].transpose(2, 0, 1).reshape(C_GROUP, ns), (C_GROUPS, 1))
        ar, ai, bbr, bbi = s5_params(c_lam_re[o].reshape(1, ns), c_lam_im[o].reshape(1, ns),
                                     jnp.repeat(c_log_dt[o], C_STATE)[None], bre, bim)
        eye = jnp.eye(C_GROUPS, dtype=F32)
        blk = lambda c: (eye[:, None, :, None] * c.transpose(0, 2, 1)[:, :, None, :]).reshape(ns, C_WIDTH).astype(BF16)
        s5p.append(dict(ar=ar, ai=ai, bbr=bbr, bbi=bbi, cre=blk(c_c_re[o]), cim=blk(c_c_im[o])))
    p = dict(norm_mix=norm_mix, norm_ffn=norm_ffn, norm_final=norm_final,
             ev_w_in=[_pad_cols(ev_w_in[e]).astype(BF16) for e in range(n_even)],
             ev_w_out=ev_w_out.astype(BF16),
             od_w_in=[od_w_in[o].astype(BF16) for o in range(n_odd)], od_w_out=od_w_out.astype(BF16),
             b_conv_w=b_conv_w, b_conv_b=b_conv_b, b_ig_bias=b_ig_bias, b_fg_bias=b_fg_bias, b_norm=b_norm,
             s5=s5p, c_d=c_d, c_w_glu=c_w_glu.astype(BF16),
             d_mu=d_mu, d_w0=d_w0, d_w_w2=d_w_w2.astype(BF16), d_a0=d_a0, d_w_a2=d_w_a2.astype(BF16),
             d_w_g2=d_w_g2.astype(BF16), d_k_k=d_k_k, d_k_a=d_k_a, d_r_k=d_r_k, d_ln_g=d_ln_g, d_ln_b=d_ln_b,
             ffn_w1=ffn_w1.astype(BF16), ffn_w3=ffn_w3.astype(BF16), ffn_w2=ffn_w2.astype(BF16))
    lam_vecs = lambda e: (a_lq1[e][None], a_lk1[e][None], a_lq2[e][None], a_lk2[e][None], a_subln[e][None])

    tp = -(-(sp_len + CHUNK) // ATT_BLK) * ATT_BLK
    n_pad = tp - sp_len - N_META
    xp = jnp.concatenate([jnp.zeros((n_pad, dm), dt), meta_tokens.astype(dt), x_prompt[0]], axis=0)
    bias2 = jnp.stack([bias_tile(rel_bias, ATT_BLK, ATT_BLK, 0, causal=True, scale=LOG2E),
                       bias_tile(rel_bias, ATT_BLK, ATT_BLK, -ATT_BLK, scale=LOG2E)], axis=0)

    def attn_p(proj, e, lam_init):
        return attn_prompt(proj, bias2, rel_bias, *lam_vecs(e), n_pad=n_pad, lam_init=lam_init)

    zeros = lambda *s: jnp.zeros(s, F32)
    st_p = dict(b_c=zeros(n_even, bp, B_HEADS, B_DK, B_DK), b_n=zeros(n_even, bp, B_HEADS, B_DK),
                b_m=zeros(n_even, bp, B_HEADS), b_conv=zeros(n_even, bp, B_CONV - 1, 2 * HALF),
                c_re=zeros(n_odd, bp, C_GROUPS, C_STATE), c_im=zeros(n_odd, bp, C_GROUPS, C_STATE),
                d_s=zeros(n_odd, bp, D_HEADS, D_HEAD, D_HEAD), d_shift=zeros(n_odd, bp, 1, d_mu.shape[1]))
    y_p, new_p = _trunk(xp, p, bsz=bp, t=tp, seq_blk=SEQ_BLK, n_pad=n_pad, attn_fn=attn_p, st=st_p)

    tk = cache_a_k.shape[2]
    past_len = tk - N_META
    cid = lambda pos: np.where(pos < N_META, 0, 1 + (pos - N_META) // CHUNK)
    q_pos = N_META + past_len + np.arange(ts)
    k_pos = np.arange(tk + ts)
    assert (cid(k_pos)[None, :] <= cid(q_pos)[:, None]).all()
    bias_past = bias_tile(rel_bias, ts, tk, -(N_META + past_len))
    bias_new = bias_tile(rel_bias, ts, ts, 0)

    def attn_s(proj, e, lam_init):
        return attn_sample(proj, cache_a_k[e].reshape(bs, tk, HALF), cache_a_v[e].reshape(bs, tk, HALF),
                           bias_past, bias_new, *lam_vecs(e), ts=ts, lam_init=lam_init)

    st_s = dict(b_c=state_b_c, b_n=state_b_n, b_m=state_b_m, b_conv=state_b_conv, c_re=state_c_re,
                c_im=state_c_im, d_s=state_d_s, d_shift=state_d_shift)
    y_s, new_s = _trunk(x_sample.reshape(bs * ts, dm), p, bsz=bs, t=ts, seq_blk=ts, n_pad=0, attn_fn=attn_s, st=st_s)

    def pack(new, bsz, t, drop):
        kv = lambda a: a[:, drop:].reshape(bsz, t - drop, A_HEADS, A_DV)
        return (jnp.stack([kv(a) for a in new['a_k']]), jnp.stack([kv(a) for a in new['a_v']]),
                jnp.stack(new['b_c']), jnp.stack(new['b_n']), jnp.stack(new['b_m']), jnp.stack(new['b_conv']),
                jnp.stack(new['c_re']), jnp.stack(new['c_im']), jnp.stack(new['d_s']), jnp.stack(new['d_shift']))

    out_p = pack(new_p, bp, tp, n_pad)
    out_s = pack(new_s, bs, ts, 0)
    return (y_p[n_pad + N_META:][None], y_s.reshape(bs, ts, dm)) + out_p + out_s
```

```python
import functools
import math

import numpy as np
import jax
import jax.numpy as jnp
from jax import lax
from jax.experimental import pallas as pl
from jax.experimental.pallas import tpu as pltpu

F32 = jnp.float32
BF16 = jnp.bfloat16
I32 = jnp.int32

CHUNK = 64
N_META = 16
EPS = 1e-6
A_HEADS = 4
A_DH = 64
A_DV = 128
N_BUCKETS = 32
MAX_DIST = 128
B_HEADS = 4
B_DK = 128
B_CONV = 4
C_GROUP = 16
C_GROUPS = 32
C_STATE = 64
C_WIDTH = C_GROUP * C_GROUPS
D_HEAD = 64
D_HEADS = 8
D_WIDTH = D_HEAD * D_HEADS
D_GN_EPS = 64e-5
HALF = 512

NEG = -1e30
LOG2E = math.log2(math.e)
LANES = 128
VMEM_LIMIT = 56 * 1024 * 1024

ROW_TILE = 512
ATT_BLK = 512
ATT_STRIP = 64
SEQ_BLK = 128
RWKV_GRP = 16
FFN_TH = 1408


def _cparams(sem):
    return pltpu.CompilerParams(dimension_semantics=sem, vmem_limit_bytes=VMEM_LIMIT)


def _dot(a, b):
    return jnp.dot(a, b, preferred_element_type=F32)


def _dot_nt(a, b):
    return lax.dot_general(a, b, (((1,), (1,)), ((), ())), preferred_element_type=F32)


def _dot_tn(a, b):
    return lax.dot_general(a, b, (((0,), (0,)), ((), ())), preferred_element_type=F32)


def _sigmoid(x):
    return 1.0 / (1.0 + jnp.exp(-x))


def _softplus(x):
    return jnp.maximum(x, 0.0) + jnp.log1p(jnp.exp(-jnp.abs(x)))


def _rms(x, g):
    return x * lax.rsqrt(jnp.mean(x * x, axis=-1, keepdims=True) + EPS) * g


def _split3(x):
    hi = x.astype(BF16)
    r1 = x - hi.astype(F32)
    mid = r1.astype(BF16)
    lo = (r1 - mid.astype(F32)).astype(BF16)
    return hi, mid, lo


def _segsum(x, ones):
    hi, mid, lo = _split3(x)
    return _dot(hi, ones) + _dot(mid, ones) + _dot(lo, ones)


def _block_ones(n, seg):
    r = lax.broadcasted_iota(I32, (n, n), 0) // seg
    c = lax.broadcasted_iota(I32, (n, n), 1) // seg
    return jnp.where(r == c, 1.0, 0.0).astype(BF16)


def _rms_mm_kernel(x_ref, g_ref, w_ref, o_ref):
    h = _rms(x_ref[...], g_ref[...])
    o_ref[...] = _dot(h.astype(BF16), w_ref[...])


def rms_matmul(x, g, w, tm=ROW_TILE):
    m, d = x.shape
    tm = min(tm, m)
    n = w.shape[1]
    return pl.pallas_call(
        _rms_mm_kernel,
        grid=(m // tm,),
        in_specs=[pl.BlockSpec((tm, d), lambda i: (i, 0)),
                  pl.BlockSpec((1, d), lambda i: (0, 0)),
                  pl.BlockSpec((d, n), lambda i: (0, 0))],
        out_specs=pl.BlockSpec((tm, n), lambda i: (i, 0)),
        out_shape=jax.ShapeDtypeStruct((m, n), F32),
        compiler_params=_cparams(("parallel",)),
        name="rms_matmul",
    )(x, g, w)


def _outproj_kernel(x_ref, a_ref, b_ref, wa_ref, wb_ref, o_ref):
    acc = _dot(a_ref[...].astype(BF16), wa_ref[...]) + _dot(b_ref[...].astype(BF16), wb_ref[...])
    o_ref[...] = x_ref[...] + acc


def outproj(x, a, b, wa, wb, tm=ROW_TILE):
    m, d = x.shape
    tm = min(tm, m)
    k = a.shape[1]
    row = lambda i: (i, 0)
    fix = lambda i: (0, 0)
    return pl.pallas_call(
        _outproj_kernel,
        grid=(m // tm,),
        in_specs=[pl.BlockSpec((tm, d), row), pl.BlockSpec((tm, k), row), pl.BlockSpec((tm, k), row),
                  pl.BlockSpec((k, d), fix), pl.BlockSpec((k, d), fix)],
        out_specs=pl.BlockSpec((tm, d), row),
        out_shape=jax.ShapeDtypeStruct((m, d), F32),
        compiler_params=_cparams(("parallel",)),
        name="outproj",
    )(x, a, b, wa, wb)


def _ffn_kernel(x_ref, g_ref, gf_ref, w1_ref, w3_ref, w2_ref, o_ref, h_scr, *, final_norm):
    j = pl.program_id(1)

    @pl.when(j == 0)
    def _():
        x = x_ref[...]
        h_scr[...] = _rms(x, g_ref[...]).astype(BF16)
        o_ref[...] = x

    h = h_scr[...]
    a = _dot(h, w1_ref[...])
    b = _dot(h, w3_ref[...])
    u = (a * _sigmoid(a)) * b
    o_ref[...] += _dot(u.astype(BF16), w2_ref[...])

    if final_norm:
        @pl.when(j == pl.num_programs(1) - 1)
        def _():
            o_ref[...] = _rms(o_ref[...], gf_ref[...])


def ffn(x, g, w1, w3, w2, gf=None, drop_rows=0, tm=ROW_TILE, th=FFN_TH):
    m, d = x.shape
    tm = min(tm, m)
    hid = w1.shape[1]
    final_norm = gf is not None
    if gf is None:
        gf = g
    skip, rem = divmod(drop_rows, tm)
    assert rem == 0
    return pl.pallas_call(
        functools.partial(_ffn_kernel, final_norm=final_norm),
        grid=(m // tm, hid // th),
        in_specs=[pl.BlockSpec((tm, d), lambda i, j: (i, 0)),
                  pl.BlockSpec((1, d), lambda i, j: (0, 0)),
                  pl.BlockSpec((1, d), lambda i, j: (0, 0)),
                  pl.BlockSpec((d, th), lambda i, j: (0, j)),
                  pl.BlockSpec((d, th), lambda i, j: (0, j)),
                  pl.BlockSpec((th, d), lambda i, j: (j, 0))],
        out_specs=pl.BlockSpec((tm, d), lambda i, j: (jnp.maximum(i - skip, 0), 0)),
        out_shape=jax.ShapeDtypeStruct((m - drop_rows, d), F32),
        scratch_shapes=[pltpu.VMEM((tm, d), BF16)],
        compiler_params=_cparams(("arbitrary" if skip else "parallel", "arbitrary")),
        name="ffn",
    )(x, g, gf, w1, w3, w2)


def _bias_kernel(rb_ref, o_ref, *, rel0, causal, scale):
    nq, nk = o_ref.shape[1], o_ref.shape[2]
    a = lax.broadcasted_iota(I32, (nq, nk), 0)
    b = lax.broadcasted_iota(I32, (nq, nk), 1)
    rel = b - a + rel0
    nb = N_BUCKETS // 2
    max_exact = nb // 2
    ret = jnp.where(rel > 0, nb, 0)
    n = jnp.abs(rel)
    nf = jnp.maximum(n, 1).astype(F32)
    large = max_exact + (jnp.log(nf / max_exact) / math.log(MAX_DIST / max_exact) * (nb - max_exact)).astype(I32)
    large = jnp.minimum(large, nb - 1)
    bucket = ret + jnp.where(n < max_exact, n, large)
    for h in range(A_HEADS):
        acc = jnp.zeros((nq, nk), F32)
        for bk in range(N_BUCKETS):
            acc = jnp.where(bucket == bk, rb_ref[bk, h] * scale, acc)
        if causal:
            acc = jnp.where((b // CHUNK) <= (a // CHUNK), acc, NEG)
        o_ref[h] = acc


def bias_tile(rel_bias, nq, nk, rel0, causal=False, scale=1.0):
    return pl.pallas_call(
        functools.partial(_bias_kernel, rel0=rel0, causal=causal, scale=scale),
        in_specs=[pl.BlockSpec(memory_space=pltpu.SMEM)],
        out_specs=pl.BlockSpec(memory_space=pltpu.VMEM),
        out_shape=jax.ShapeDtypeStruct((A_HEADS, nq, nk), F32),
        compiler_params=pltpu.CompilerParams(vmem_limit_bytes=VMEM_LIMIT),
        name="bias_tile",
    )(rel_bias)


def _lambda(lq1_ref, lk1_ref, lq2_ref, lk2_ref, lam_init):
    s1 = jnp.sum(lq1_ref[...] * lk1_ref[...], axis=-1, keepdims=True)
    s2 = jnp.sum(lq2_ref[...] * lk2_ref[...], axis=-1, keepdims=True)
    return jnp.exp(s1) - jnp.exp(s2) + lam_init


def _attn_finish(acc0, l0, acc1, l1, lam, g, lam_init):
    o = acc0 / l0 - lam * (acc1 / l1)
    return _rms(o, g) * (1.0 - lam_init)


def _attn_prompt_kernel(qi_ref, kj_ref, rb_ref, q_ref, k_ref, v_ref, bias_ref,
                        lq1_ref, lk1_ref, lq2_ref, lk2_ref, g_ref, o_ref,
                        m_scr, l_scr, acc_scr, s_scr, p_scr, al_scr, *, blk, n_pad, lam_init):
    s = pl.program_id(0)
    i = qi_ref[s]
    j = kj_ref[s]
    d = i - j
    nct = blk // LANES
    nstrip = blk // ATT_STRIP

    @pl.when(j == 0)
    def _():
        m_scr[...] = jnp.full(m_scr.shape, NEG, F32)
        l_scr[...] = jnp.zeros(l_scr.shape, F32)
        acc_scr[...] = jnp.zeros(acc_scr.shape, F32)

    def update(general):
        for h in range(A_HEADS):
            far = rb_ref[N_BUCKETS // 2 - 1, h] * LOG2E
            for mm in range(2):
                c0 = h * 2 * A_DH + mm * A_DH
                qb = (q_ref[:, c0:c0 + A_DH] * (A_DH ** -0.5 * LOG2E)).astype(BF16)
                s_scr[mm] = _dot_nt(qb, k_ref[:, c0:c0 + A_DH].astype(BF16))
            for mm in range(2):
                idx = 2 * h + mm
                for r in range(nstrip):
                    rows = slice(r * ATT_STRIP, (r + 1) * ATT_STRIP)
                    tiles = []
                    for c in range(nct):
                        cols = slice(c * LANES, (c + 1) * LANES)
                        t = s_scr[mm, rows, cols]
                        if general:
                            kpos = j * blk + c * LANES + lax.broadcasted_iota(I32, (1, LANES), 1)
                            near = bias_ref[jnp.minimum(d, 1), h, rows, cols]
                            t = t + jnp.where(d < 2, near, far) + jnp.where(kpos < n_pad, NEG, 0.0)
                        tiles.append(t)
                    mx = functools.reduce(jnp.maximum, tiles)
                    m_cur = jnp.broadcast_to(jnp.max(mx, axis=-1, keepdims=True), (ATT_STRIP, LANES))
                    if not general:
                        m_cur = m_cur + far
                    m_old = m_scr[idx, rows, :]
                    m_new = jnp.maximum(m_old, m_cur)
                    alpha = jnp.exp2(m_old - m_new)
                    m_sub = m_new if general else m_new - far
                    ps = [jnp.exp2(t - m_sub) for t in tiles]
                    l_scr[idx, rows, :] = alpha * l_scr[idx, rows, :] + functools.reduce(jnp.add, ps)
                    m_scr[idx, rows, :] = m_new
                    al_scr[mm, rows, :] = alpha
                    for c in range(nct):
                        p_scr[mm * blk + r * ATT_STRIP:mm * blk + (r + 1) * ATT_STRIP,
                              c * LANES:(c + 1) * LANES] = ps[c].astype(BF16)
            pv = _dot(p_scr[...], v_ref[:, h * A_DV:(h + 1) * A_DV].astype(BF16))
            for mm in range(2):
                idx = 2 * h + mm
                acc_scr[idx] = al_scr[mm] * acc_scr[idx] + pv[mm * blk:(mm + 1) * blk]

    is_far = jnp.logical_and(d >= 2, j > 0)

    @pl.when(is_far)
    def _():
        update(False)

    @pl.when(jnp.logical_not(is_far))
    def _():
        update(True)

    @pl.when(d == 0)
    def _():
        lam = _lambda(lq1_ref, lk1_ref, lq2_ref, lk2_ref, lam_init)
        rows = i * blk + lax.broadcasted_iota(I32, (blk, 1), 0)
        valid = rows >= n_pad
        for h in range(A_HEADS):
            l0 = jnp.sum(l_scr[2 * h], axis=-1, keepdims=True)
            l1 = jnp.sum(l_scr[2 * h + 1], axis=-1, keepdims=True)
            y = _attn_finish(acc_scr[2 * h], l0, acc_scr[2 * h + 1], l1, lam, g_ref[...], lam_init)
            o_ref[:, h * A_DV:(h + 1) * A_DV] = jnp.where(valid, y, 0.0)


def attn_prompt(proj, bias2, rel_bias, lq1, lk1, lq2, lk2, g, *, n_pad, lam_init, blk=ATT_BLK):
    t = proj.shape[0]
    nb = t // blk
    qi = np.array([i for i in range(nb) for _ in range(i + 1)], np.int32)
    kj = np.array([j for i in range(nb) for j in range(i + 1)], np.int32)
    vec = lambda n: pl.BlockSpec((1, n), lambda s, qi, kj: (0, 0))
    grid_spec = pltpu.PrefetchScalarGridSpec(
        num_scalar_prefetch=2,
        grid=(len(qi),),
        in_specs=[pl.BlockSpec(memory_space=pltpu.SMEM),
                  pl.BlockSpec((blk, HALF), lambda s, qi, kj: (qi[s], 0)),
                  pl.BlockSpec((blk, HALF), lambda s, qi, kj: (kj[s], 1)),
                  pl.BlockSpec((blk, HALF), lambda s, qi, kj: (kj[s], 2)),
                  pl.BlockSpec((2, A_HEADS, blk, blk), lambda s, qi, kj: (0, 0, 0, 0)),
                  vec(A_DH), vec(A_DH), vec(A_DH), vec(A_DH), vec(A_DV)],
        out_specs=pl.BlockSpec((blk, HALF), lambda s, qi, kj: (qi[s], 0)),
        scratch_shapes=[pltpu.VMEM((2 * A_HEADS, blk, LANES), F32),
                        pltpu.VMEM((2 * A_HEADS, blk, LANES), F32),
                        pltpu.VMEM((2 * A_HEADS, blk, A_DV), F32),
                        pltpu.VMEM((2, blk, blk), F32),
                        pltpu.VMEM((2 * blk, blk), BF16),
                        pltpu.VMEM((2, blk, LANES), F32)],
    )
    assert n_pad <= blk and blk >= MAX_DIST
    return pl.pallas_call(
        functools.partial(_attn_prompt_kernel, blk=blk, n_pad=n_pad, lam_init=lam_init),
        grid_spec=grid_spec,
        out_shape=jax.ShapeDtypeStruct((t, HALF), F32),
        compiler_params=_cparams(("arbitrary",)),
        name="attn_prompt",
    )(jnp.asarray(qi), jnp.asarray(kj), rel_bias, proj, proj, proj, bias2, lq1, lk1, lq2, lk2, g)


def _attn_sample_kernel(q_ref, kn_ref, vn_ref, kc_ref, vc_ref, bp_ref, bn_ref,
                        lq1_ref, lk1_ref, lq2_ref, lk2_ref, g_ref, o_ref, *, lam_init):
    lam = _lambda(lq1_ref, lk1_ref, lq2_ref, lk2_ref, lam_init)
    for h in range(A_HEADS):
        vsl = slice(h * A_DV, (h + 1) * A_DV)
        vp = vc_ref[0, :, vsl].astype(BF16)
        vn = vn_ref[:, vsl].astype(BF16)
        outs = []
        for mm in range(2):
            c0 = h * 2 * A_DH + mm * A_DH
            qb = (q_ref[:, c0:c0 + A_DH] * (A_DH ** -0.5)).astype(BF16)
            sp = _dot_nt(qb, kc_ref[0, :, c0:c0 + A_DH].astype(BF16)) + bp_ref[h]
            sn = _dot_nt(qb, kn_ref[:, c0:c0 + A_DH].astype(BF16)) + bn_ref[h]
            mx = jnp.maximum(jnp.max(sp, axis=-1, keepdims=True), jnp.max(sn, axis=-1, keepdims=True))
            pp = jnp.exp(sp - mx)
            pn = jnp.exp(sn - mx)
            l = jnp.sum(pp, axis=-1, keepdims=True) + jnp.sum(pn, axis=-1, keepdims=True)
            acc = _dot(pp.astype(BF16), vp) + _dot(pn.astype(BF16), vn)
            outs.append((acc, l))
        o_ref[:, vsl] = _attn_finish(outs[0][0], outs[0][1], outs[1][0], outs[1][1], lam, g_ref[...], lam_init)


def attn_sample(proj, kc, vc, bias_past, bias_new, lq1, lk1, lq2, lk2, g, *, ts, lam_init):
    bsz, tk, _ = kc.shape
    vec = lambda n: pl.BlockSpec((1, n), lambda b: (0, 0))
    return pl.pallas_call(
        functools.partial(_attn_sample_kernel, lam_init=lam_init),
        grid=(bsz,),
        in_specs=[pl.BlockSpec((ts, HALF), lambda b: (b, 0)),
                  pl.BlockSpec((ts, HALF), lambda b: (b, 1)),
                  pl.BlockSpec((ts, HALF), lambda b: (b, 2)),
                  pl.BlockSpec((1, tk, HALF), lambda b: (b, 0, 0)),
                  pl.BlockSpec((1, tk, HALF), lambda b: (b, 0, 0)),
                  pl.BlockSpec((A_HEADS, ts, tk), lambda b: (0, 0, 0)),
                  pl.BlockSpec((A_HEADS, ts, ts), lambda b: (0, 0, 0)),
                  vec(A_DH), vec(A_DH), vec(A_DH), vec(A_DH), vec(A_DV)],
        out_specs=pl.BlockSpec((ts, HALF), lambda b: (b, 0)),
        out_shape=jax.ShapeDtypeStruct((bsz * ts, HALF), F32),
        compiler_params=_cparams(("parallel",)),
        name="attn_sample",
    )(proj, proj, proj, kc, vc, bias_past, bias_new, lq1, lk1, lq2, lk2, g)


def _mlstm_kernel(bq_ref, bk_ref, bv_ref, bo_ref, gc_ref, gr_ref, c0_ref, n0_ref, m0_ref, conv0_ref,
                  cw_ref, cb_ref, gbc_ref, gbr_ref, bn_ref,
                  h_ref, c_out, n_out, m_out,
                  xbuf, c_scr, n_scr, m_scr, *, ln, n_pad):
    tb = pl.program_id(1)
    nt = pl.num_programs(1)
    halo = B_CONV - 1
    base = 8 - halo

    @pl.when(tb == 0)
    def _():
        xbuf[base:8, :] = conv0_ref[0]
        c_scr[...] = c0_ref[0]
        n_scr[...] = n0_ref[0]
        for h in range(B_HEADS):
            m_scr[h] = m0_ref[0, :, h:h + 1]

    xbuf[8:8 + ln, 0:HALF] = bq_ref[...]
    xbuf[8:8 + ln, HALF:2 * HALF] = bk_ref[...]
    conv = cb_ref[...] + cw_ref[0:1, :] * xbuf[base:base + ln, :]
    for jj in range(1, B_CONV):
        conv = conv + cw_ref[jj:jj + 1, :] * xbuf[base + jj:base + jj + ln, :]
    xbuf[base:8, :] = xbuf[8 + ln - halo:8 + ln, :]
    conv = conv * _sigmoid(conv)
    q_all = conv[:, 0:HALF] * (B_DK ** -0.5)
    k_all = conv[:, HALF:2 * HALF]

    rows = tb * ln + lax.broadcasted_iota(I32, (ln, 1), 0)
    cols = tb * ln + lax.broadcasted_iota(I32, (1, ln), 1)
    valid_c = rows >= n_pad
    valid_r = cols >= n_pad

    gc = gc_ref[...] + gbc_ref[...]
    gr = gr_ref[0] + gbr_ref[...]
    li_c = jnp.where(valid_c, gc, NEG)
    li_r = jnp.where(valid_r, gr, NEG)
    lf_c = jnp.where(valid_c, -_softplus(-gc), 0.0)
    lf_r = jnp.where(valid_r, -_softplus(-gr), 0.0)

    ri = lax.broadcasted_iota(I32, (ln, ln), 0)
    ci = lax.broadcasted_iota(I32, (ln, ln), 1)
    tril = ri >= ci
    tril_f = jnp.where(tril, 1.0, 0.0)
    triu_f = jnp.where(ri <= ci, 1.0, 0.0)
    b_c = jnp.dot(tril_f, lf_c, preferred_element_type=F32, precision=lax.Precision.HIGHEST)
    b_r = jnp.dot(lf_r, triu_f, preferred_element_type=F32, precision=lax.Precision.HIGHEST)

    for h in range(B_HEADS):
        sl = slice(h * B_DK, (h + 1) * B_DK)
        qh = q_all[:, sl].astype(BF16)
        kh = k_all[:, sl]
        vh = bv_ref[:, sl].astype(BF16)
        c = c_scr[h]
        n = n_scr[h]
        m = m_scr[h]
        bc = b_c[:, B_HEADS + h:B_HEADS + h + 1]
        br = b_r[B_HEADS + h:B_HEADS + h + 1, :]
        inter = bc + m
        dmat = jnp.where(tril, bc - br + li_r[h:h + 1, :], NEG)
        mt = jnp.maximum(inter, jnp.max(dmat, axis=-1, keepdims=True))
        w_inter = jnp.exp(inter - mt)
        s = _dot_nt(qh, kh.astype(BF16)) * jnp.exp(dmat - mt)
        num = w_inter * _dot(qh, c.astype(BF16)) + _dot(s.astype(BF16), vh)
        qn = jnp.sum(qh.astype(F32) * n, axis=-1, keepdims=True)
        den = w_inter * qn + jnp.sum(s, axis=-1, keepdims=True)
        hh = num / jnp.maximum(jnp.abs(den), jnp.exp(-mt))
        b_last = bc[ln - 1:ln, :]
        g = b_last - bc + li_c[:, h:h + 1]
        m_new = jnp.maximum(b_last + m, jnp.max(g, axis=0, keepdims=True))
        decay = jnp.exp(b_last + m - m_new)
        wk = (jnp.exp(g - m_new) * kh)
        c_scr[h] = decay * c + _dot_tn(wk.astype(BF16), vh)
        n_scr[h] = decay * n + jnp.sum(wk, axis=0, keepdims=True)
        m_scr[h] = m_new
        hn = _rms(hh, bn_ref[:, sl]) * _sigmoid(bo_ref[:, sl])
        h_ref[:, sl] = jnp.where(valid_c, hn, 0.0)

    @pl.when(tb == nt - 1)
    def _():
        c_out[0] = c_scr[...]
        n_out[0] = n_scr[...]
        lane = lax.broadcasted_iota(I32, (1, B_HEADS), 1)
        mrow = jnp.zeros((1, B_HEADS), F32)
        for h in range(B_HEADS):
            mrow = jnp.where(lane == h, m_scr[h], mrow)
        m_out[0] = mrow


def mlstm(proj, gates_r, c0, n0, m0, conv0, cw, cb, igb, fgb, bnorm, *, bsz, t, ln, n_pad):
    nt = t // ln
    gb = jnp.concatenate([igb, fgb])
    gw = proj.shape[1] // LANES - 1
    row = lambda c: (lambda b, i: (b * nt + i, c))
    fix2 = lambda b, i: (0, 0)
    out_shapes = (jax.ShapeDtypeStruct((bsz * t, HALF), F32),
                  jax.ShapeDtypeStruct((bsz, B_HEADS, B_DK, B_DK), F32),
                  jax.ShapeDtypeStruct((bsz, B_HEADS, 1, B_DK), F32),
                  jax.ShapeDtypeStruct((bsz, 1, B_HEADS), F32))
    return pl.pallas_call(
        functools.partial(_mlstm_kernel, ln=ln, n_pad=n_pad),
        grid=(bsz, nt),
        in_specs=[pl.BlockSpec((ln, HALF), row(3)), pl.BlockSpec((ln, HALF), row(4)),
                  pl.BlockSpec((ln, HALF), row(5)), pl.BlockSpec((ln, HALF), row(6)),
                  pl.BlockSpec((ln, LANES), row(gw)),
                  pl.BlockSpec((1, 2 * B_HEADS, ln), lambda b, i: (b * nt + i, 0, 0)),
                  pl.BlockSpec((1, B_HEADS, B_DK, B_DK), lambda b, i: (b, 0, 0, 0)),
                  pl.BlockSpec((1, B_HEADS, 1, B_DK), lambda b, i: (b, 0, 0, 0)),
                  pl.BlockSpec((1, 1, B_HEADS), lambda b, i: (b, 0, 0)),
                  pl.BlockSpec((1, B_CONV - 1, 2 * HALF), lambda b, i: (b, 0, 0)),
                  pl.BlockSpec((B_CONV, 2 * HALF), fix2), pl.BlockSpec((1, 2 * HALF), fix2),
                  pl.BlockSpec((1, LANES), fix2), pl.BlockSpec((2 * B_HEADS, 1), fix2),
                  pl.BlockSpec((1, HALF), fix2)],
        out_specs=(pl.BlockSpec((ln, HALF), lambda b, i: (b * nt + i, 0)),
                   pl.BlockSpec((1, B_HEADS, B_DK, B_DK), lambda b, i: (b, 0, 0, 0)),
                   pl.BlockSpec((1, B_HEADS, 1, B_DK), lambda b, i: (b, 0, 0, 0)),
                   pl.BlockSpec((1, 1, B_HEADS), lambda b, i: (b, 0, 0))),
        out_shape=out_shapes,
        scratch_shapes=[pltpu.VMEM((ln + 8, 2 * HALF), F32),
                        pltpu.VMEM((B_HEADS, B_DK, B_DK), F32),
                        pltpu.VMEM((B_HEADS, 1, B_DK), F32),
                        pltpu.VMEM((B_HEADS, 1, 1), F32)],
        compiler_params=_cparams(("parallel", "arbitrary")),
        name="mlstm",
    )(proj, proj, proj, proj, proj, gates_r, c0, n0, m0, conv0, cw, cb,
      jnp.pad(gb, (0, LANES - 2 * B_HEADS))[None], gb[:, None], bnorm)


def _s5_param_kernel(lr_ref, li_ref, ldt_ref, bre_ref, bim_ref, ar_ref, ai_ref, bbr_ref, bbi_ref):
    lr = lr_ref[...]
    li = li_ref[...]
    dt = jnp.exp(ldt_ref[...])
    mag = jnp.exp(lr * dt)
    ar = mag * jnp.cos(li * dt)
    ai = mag * jnp.sin(li * dt)
    den = lr * lr + li * li
    cr = ((ar - 1.0) * lr + ai * li) / den
    ci = (ai * lr - (ar - 1.0) * li) / den
    ar_ref[...] = ar
    ai_ref[...] = ai
    nr, nc = bre_ref.shape
    same = (lax.broadcasted_iota(I32, (nr, nc), 0) // C_GROUP) == (lax.broadcasted_iota(I32, (nr, nc), 1) // C_STATE)
    br = bre_ref[...]
    bi = bim_ref[...]
    bbr_ref[...] = jnp.where(same, cr * br - ci * bi, 0.0).astype(BF16)
    bbi_ref[...] = jnp.where(same, cr * bi + ci * br, 0.0).astype(BF16)


def s5_params(lr, li, ldt, bre_rep, bim_rep):
    ns = lr.shape[1]
    vm = pl.BlockSpec(memory_space=pltpu.VMEM)
    return pl.pallas_call(
        _s5_param_kernel,
        in_specs=[vm] * 5,
        out_specs=(vm, vm, vm, vm),
        out_shape=(jax.ShapeDtypeStruct((1, ns), F32), jax.ShapeDtypeStruct((1, ns), F32),
                   jax.ShapeDtypeStruct(bre_rep.shape, BF16), jax.ShapeDtypeStruct(bre_rep.shape, BF16)),
        compiler_params=pltpu.CompilerParams(vmem_limit_bytes=VMEM_LIMIT),
        name="s5_params",
    )(lr, li, ldt, bre_rep, bim_rep)


def _s5_kernel(u_ref, ar_ref, ai_ref, bbr_ref, bbi_ref, cre_ref, cim_ref, d_ref, wg_ref, x0r_ref, x0i_ref,
               o_ref, xr_out, xi_out, sr_scr, si_scr, bur, bui, xra, xia, *, tb_len, n_pad):
    tb = pl.program_id(1)
    nt = pl.num_programs(1)

    @pl.when(tb == 0)
    def _():
        sr_scr[...] = x0r_ref[0]
        si_scr[...] = x0i_ref[0]

    u = u_ref[...]
    ub = u.astype(BF16)
    bur[...] = _dot(ub, bbr_ref[...])
    bui[...] = _dot(ub, bbi_ref[...])
    ar = ar_ref[...]
    ai = ai_ref[...]

    sub = lax.broadcasted_iota(I32, (8, ar.shape[1]), 0)

    def body(grp, carry):
        xr, xi = carry
        base = pl.multiple_of(grp * 8, 8)
        br8 = bur[pl.ds(base, 8), :]
        bi8 = bui[pl.ds(base, 8), :]
        xr8 = jnp.zeros(br8.shape, F32)
        xi8 = jnp.zeros(br8.shape, F32)
        for i in range(8):
            nxr = ar * xr - ai * xi + br8[i:i + 1, :]
            nxi = ar * xi + ai * xr + bi8[i:i + 1, :]
            xr, xi = nxr, nxi
            xr8 = jnp.where(sub == i, xr, xr8)
            xi8 = jnp.where(sub == i, xi, xi8)
        xra[pl.ds(base, 8), :] = xr8
        xia[pl.ds(base, 8), :] = xi8
        return xr, xi

    xr, xi = lax.fori_loop(0, tb_len // 8, body, (sr_scr[...], si_scr[...]))
    sr_scr[...] = xr
    si_scr[...] = xi

    y = _dot(xra[...].astype(BF16), cre_ref[...]) - _dot(xia[...].astype(BF16), cim_ref[...]) + d_ref[...] * u
    yg = 0.5 * y * (1.0 + jnp.tanh(math.sqrt(2.0 / math.pi) * (y + 0.044715 * (y * y * y))))
    oc = yg * _sigmoid(_dot(yg.astype(BF16), wg_ref[...]))
    rows = tb * tb_len + lax.broadcasted_iota(I32, (tb_len, 1), 0)
    o_ref[...] = jnp.where(rows >= n_pad, oc, 0.0)

    @pl.when(tb == nt - 1)
    def _():
        xr_out[0] = xr
        xi_out[0] = xi


def s5(proj, ar, ai, bbr, bbi, cre, cim, dskip, wglu, x0r, x0i, *, bsz, t, tb_len, n_pad):
    nt = t // tb_len
    ns = ar.shape[1]
    fix2 = lambda b, i: (0, 0)
    st = pl.BlockSpec((1, 1, ns), lambda b, i: (b, 0, 0))
    return pl.pallas_call(
        functools.partial(_s5_kernel, tb_len=tb_len, n_pad=n_pad),
        grid=(bsz, nt),
        in_specs=[pl.BlockSpec((tb_len, HALF), lambda b, i: (b * nt + i, 0)),
                  pl.BlockSpec((1, ns), fix2), pl.BlockSpec((1, ns), fix2),
                  pl.BlockSpec((HALF, ns), fix2), pl.BlockSpec((HALF, ns), fix2),
                  pl.BlockSpec((ns, HALF), fix2), pl.BlockSpec((ns, HALF), fix2),
                  pl.BlockSpec((1, HALF), fix2), pl.BlockSpec((HALF, HALF), fix2), st, st],
        out_specs=(pl.BlockSpec((tb_len, HALF), lambda b, i: (b * nt + i, 0)), st, st),
        out_shape=(jax.ShapeDtypeStruct((bsz * t, HALF), F32),
                   jax.ShapeDtypeStruct((bsz, 1, ns), F32), jax.ShapeDtypeStruct((bsz, 1, ns), F32)),
        scratch_shapes=[pltpu.VMEM((1, ns), F32), pltpu.VMEM((1, ns), F32),
                        pltpu.VMEM((tb_len, ns), F32), pltpu.VMEM((tb_len, ns), F32),
                        pltpu.VMEM((tb_len, ns), F32), pltpu.VMEM((tb_len, ns), F32)],
        compiler_params=_cparams(("parallel", "arbitrary")),
        name="s5",
    )(proj, ar, ai, bbr, bbi, cre, cim, dskip, wglu, x0r, x0i)


def _rwkv_kernel(r_ref, k_ref, v_ref, lo_ref, sh0_ref, s0_ref,
                 mu_ref, w0_ref, ww2_ref, a0_ref, wa2_ref, wg2_ref, kk_ref, ka_ref, rk_ref, lng_ref, lnb_ref,
                 o_ref, s_out,
                 xbuf, s_scr, w_scr, nkk_scr, b_scr, k_scr, r_scr, v_scr, c_scr, be_scr, ga_scr, y_scr,
                 *, tb_len, n_pad):
    tb = pl.program_id(1)
    nt = pl.num_programs(1)
    npair = D_HEADS // 2
    ncol = xbuf.shape[1]

    @pl.when(tb == 0)
    def _():
        xbuf[7:8, :] = sh0_ref[0]
        s_scr[...] = s0_ref[0]

    xbuf[8:8 + tb_len, 0:D_WIDTH] = r_ref[...]
    xbuf[8:8 + tb_len, D_WIDTH:2 * D_WIDTH] = k_ref[...]
    xbuf[8:8 + tb_len, 2 * D_WIDTH:3 * D_WIDTH] = v_ref[...]
    xbuf[8:8 + tb_len, 3 * D_WIDTH:ncol] = lo_ref[...]
    cur = xbuf[8:8 + tb_len, :]
    prev = xbuf[7:7 + tb_len, :]
    xbuf[7:8, :] = xbuf[7 + tb_len:8 + tb_len, :]
    xm = cur + mu_ref[...] * (prev - cur)
    r = xm[:, 0:D_WIDTH]
    k = xm[:, D_WIDTH:2 * D_WIDTH]
    v = xm[:, 2 * D_WIDTH:3 * D_WIDTH]
    c0 = 3 * D_WIDTH
    wlo = xm[:, c0:c0 + 64]
    alo = xm[:, c0 + 64:c0 + 128]
    glo = xm[:, c0 + 128:c0 + 256]

    w_raw = w0_ref[...] + _dot(jnp.tanh(wlo).astype(BF16), ww2_ref[...])
    decay = jnp.exp(-jnp.exp(-_softplus(-w_raw) - 0.5))
    a = _sigmoid(a0_ref[...] + _dot(alo.astype(BF16), wa2_ref[...]))
    g = _dot(_sigmoid(glo).astype(BF16), wg2_ref[...])

    ones_h = _block_ones(D_WIDTH, D_HEAD)
    kk = k * kk_ref[...]
    kk = kk / jnp.maximum(jnp.sqrt(_segsum(kk * kk, ones_h)), 1e-12)
    k2 = k * (1.0 + (a - 1.0) * ka_ref[...])

    nkk = -kk
    bb = kk * a
    w_scr[...] = decay
    nkk_scr[...] = nkk
    b_scr[...] = bb
    k_scr[...] = k2
    r_scr[...] = r
    v_scr[...] = v
    c_scr[...] = pltpu.roll(decay, 1, 0) * nkk
    be_scr[...] = _segsum(pltpu.roll(bb, 1, 0) * nkk, ones_h)
    ga_scr[...] = _segsum(pltpu.roll(k2, 1, 0) * nkk, ones_h)

    eye2 = jnp.where((lax.broadcasted_iota(I32, (D_HEAD, LANES), 1) % D_HEAD)
                     == lax.broadcasted_iota(I32, (D_HEAD, LANES), 0), 1.0, 0.0)

    sub = lax.broadcasted_iota(I32, (RWKV_GRP, LANES), 0)
    pairs = range(npair)
    nrow = npair * D_HEAD
    ri = lax.broadcasted_iota(I32, (2 * LANES, 2 * LANES), 0)
    ci = lax.broadcasted_iota(I32, (2 * LANES, 2 * LANES), 1)
    same_head = (ri % LANES) // D_HEAD == (ci % LANES) // D_HEAD
    ones_s = jnp.where(jnp.logical_and(same_head, ri // LANES >= ci // LANES), 1.0, 0.0).astype(BF16)

    def hilo(parts):
        x = jnp.concatenate(parts, axis=0)
        hi = x.astype(BF16)
        return jnp.concatenate([hi, (x - hi.astype(F32)).astype(BF16)], axis=1)

    def per_pair(res):
        return [res[p * D_HEAD:(p + 1) * D_HEAD] for p in pairs]

    def y_rows(yt, ycol, i):
        return [jnp.where(sub == i, jnp.sum(ycol[p] * eye2, axis=0, keepdims=True), yt[p]) for p in pairs]

    def body(grp, state):
        base = pl.multiple_of(grp * RWKV_GRP, RWKV_GRP)
        tile = lambda scr: [scr[pl.ds(base, RWKV_GRP), p * LANES:(p + 1) * LANES] for p in pairs]
        nkk_t, c_t, be_t, ga_t = tile(nkk_scr), tile(c_scr), tile(be_scr), tile(ga_scr)
        w_t, b_t, k_t, r_t, v_t = tile(w_scr), tile(b_scr), tile(k_scr), tile(r_scr), tile(v_scr)
        vh = [x.astype(BF16).astype(F32) for x in v_t]
        vl = [x - h for x, h in zip(v_t, vh)]

        def vcol_lhs(i):
            return jnp.concatenate([jnp.concatenate([vh[p][i:i + 1] * eye2 for p in pairs], axis=0).astype(BF16),
                                    jnp.concatenate([vl[p][i:i + 1] * eye2 for p in pairs], axis=0).astype(BF16)],
                                   axis=1)

        def segsums(lhs):
            res = _dot(jnp.concatenate(lhs, axis=0), ones_s)
            return [res[i * nrow:(i + 1) * nrow] for i in range(len(lhs))]

        def read_out(yt, res, i):
            second = res[:, LANES:2 * LANES]
            yt = y_rows(yt, per_pair(res[:, 0:LANES] - second), i)
            return y_rows(yt, per_pair(second), i + 1)

        yt = [jnp.zeros((RWKV_GRP, LANES), F32) for _ in pairs]
        sp = list(state)
        vc = [per_pair(r[:, 0:LANES]) for r in segsums([vcol_lhs(0), vcol_lhs(1)])]
        yprods = None
        for t0 in range(0, RWKV_GRP, 2):
            t1 = t0 + 1
            r0, r1 = slice(t0, t0 + 1), slice(t1, t1 + 1)
            more = t0 + 2 < RWKV_GRP
            lhs = [hilo([sp[p] * nkk_t[p][r0] for p in pairs]), hilo([sp[p] * c_t[p][r1] for p in pairs])]
            if more:
                lhs += [vcol_lhs(t0 + 2), vcol_lhs(t0 + 3)]
            if yprods is not None:
                lhs.append(yprods)
            res = segsums(lhs)
            sa0, tmp = per_pair(res[0][:, 0:LANES]), per_pair(res[1][:, 0:LANES])
            if yprods is not None:
                yt = read_out(yt, res[-1], t0 - 2)
            s0 = [sp[p] * w_t[p][r0] + sa0[p] * b_t[p][r0] + vc[0][p] * k_t[p][r0] for p in pairs]
            sa1 = [tmp[p] + sa0[p] * be_t[p][r1] + vc[0][p] * ga_t[p][r1] for p in pairs]
            sp = [s0[p] * w_t[p][r1] + sa1[p] * b_t[p][r1] + vc[1][p] * k_t[p][r1] for p in pairs]
            yprods = jnp.concatenate(
                [jnp.concatenate([s0[p] * r_t[p][r0] for p in pairs], axis=0).astype(BF16),
                 jnp.concatenate([sp[p] * r_t[p][r1] for p in pairs], axis=0).astype(BF16)], axis=1)
            if more:
                vc = [per_pair(res[2][:, 0:LANES]), per_pair(res[3][:, 0:LANES])]
        yt = read_out(yt, segsums([yprods])[0], RWKV_GRP - 2)
        for p in pairs:
            y_scr[pl.ds(base, RWKV_GRP), p * LANES:(p + 1) * LANES] = yt[p]
        return tuple(sp)

    state = lax.fori_loop(0, tb_len // RWKV_GRP, body, tuple(s_scr[p] for p in pairs))
    for p in pairs:
        s_scr[p] = state[p]

    y = y_scr[...]
    inv = 1.0 / D_HEAD
    mean = _segsum(y, ones_h) * inv
    yc = y - mean
    var = _segsum(yc * yc, ones_h) * inv
    y = yc * lax.rsqrt(var + D_GN_EPS) * lng_ref[...] + lnb_ref[...]
    y = y + _segsum(r * k2 * rk_ref[...], ones_h) * v
    rows = tb * tb_len + lax.broadcasted_iota(I32, (tb_len, 1), 0)
    o_ref[...] = jnp.where(rows >= n_pad, y * g, 0.0)

    @pl.when(tb == nt - 1)
    def _():
        s_out[0] = s_scr[...]


def rwkv(proj, sh0, s0, mu, w0, ww2, a0, wa2, wg2, kkp, kap, rkp, lng, lnb, *, bsz, t, tb_len, n_pad):
    nt = t // tb_len
    npair = D_HEADS // 2
    ncol = mu.shape[1]
    nlo = ncol - 3 * D_WIDTH
    row = lambda c: (lambda b, i: (b * nt + i, c))
    fix2 = lambda b, i: (0, 0)
    vec = pl.BlockSpec((1, D_WIDTH), fix2)
    st = pl.BlockSpec((1, npair, D_HEAD, LANES), lambda b, i: (b, 0, 0, 0))
    big = lambda: pltpu.VMEM((tb_len, D_WIDTH), F32)
    return pl.pallas_call(
        functools.partial(_rwkv_kernel, tb_len=tb_len, n_pad=n_pad),
        grid=(bsz, nt),
        in_specs=[pl.BlockSpec((tb_len, D_WIDTH), row(1)), pl.BlockSpec((tb_len, D_WIDTH), row(2)),
                  pl.BlockSpec((tb_len, D_WIDTH), row(3)),
                  pl.BlockSpec((tb_len, nlo), row(4 * D_WIDTH // nlo)),
                  pl.BlockSpec((1, 1, ncol), lambda b, i: (b, 0, 0)), st,
                  pl.BlockSpec((1, ncol), fix2), vec,
                  pl.BlockSpec(ww2.shape, fix2), vec, pl.BlockSpec(wa2.shape, fix2), pl.BlockSpec(wg2.shape, fix2),
                  vec, vec, vec, vec, vec],
        out_specs=(pl.BlockSpec((tb_len, D_WIDTH), lambda b, i: (b * nt + i, 0)), st),
        out_shape=(jax.ShapeDtypeStruct((bsz * t, D_WIDTH), F32),
                   jax.ShapeDtypeStruct((bsz, npair, D_HEAD, LANES), F32)),
        scratch_shapes=[pltpu.VMEM((tb_len + 8, ncol), F32), pltpu.VMEM((npair, D_HEAD, LANES), F32),
                        big(), big(), big(), big(), big(), big(), big(), big(), big(), big()],
        compiler_params=_cparams(("parallel", "arbitrary")),
        name="rwkv7",
    )(proj, proj, proj, proj, sh0, s0, mu, w0, ww2, a0, wa2, wg2, kkp, kap, rkp, lng, lnb)


def _pad_cols(w, mult=LANES):
    n = w.shape[1]
    return jnp.pad(w, ((0, 0), (0, (-n) % mult)))


def _pairs_from_heads(s):
    b = s.shape[0]
    return s.reshape(b, D_HEADS // 2, 2, D_HEAD, D_HEAD).transpose(0, 1, 3, 2, 4).reshape(b, D_HEADS // 2, D_HEAD, LANES)


def _heads_from_pairs(s):
    b = s.shape[0]
    return s.reshape(b, D_HEADS // 2, D_HEAD, 2, D_HEAD).transpose(0, 1, 3, 2, 4).reshape(b, D_HEADS, D_HEAD, D_HEAD)


def _trunk(x, p, *, bsz, t, seq_blk, n_pad, attn_fn, st, drop_rows=0):
    depth = p['norm_mix'].shape[0]
    new = {k: [] for k in ('a_k', 'a_v', 'b_c', 'b_n', 'b_m', 'b_conv', 'c_re', 'c_im', 'd_s', 'd_shift')}
    nt = t // seq_blk
    y = None
    for layer in range(depth):
        g_mix = p['norm_mix'][layer][None]
        if layer % 2 == 0:
            e = layer // 2
            proj = rms_matmul(x, g_mix, p['ev_w_in'][e])
            lam_init = 0.8 - 0.6 * math.exp(-0.3 * layer)
            oa = attn_fn(proj, e, lam_init)
            gcols = proj[:, 7 * HALF:7 * HALF + 2 * B_HEADS]
            gates_r = gcols.reshape(bsz * nt, seq_blk, 2 * B_HEADS).transpose(0, 2, 1)
            hb, bc, bn, bm = mlstm(proj, gates_r, st['b_c'][e], st['b_n'][e][:, :, None, :], st['b_m'][e][:, None, :],
                                   st['b_conv'][e], p['b_conv_w'][e], p['b_conv_b'][e][None],
                                   p['b_ig_bias'][e], p['b_fg_bias'][e], p['b_norm'][e][None],
                                   bsz=bsz, t=t, ln=seq_blk, n_pad=n_pad)
            p3 = proj.reshape(bsz, t, -1)
            new['a_k'].append(p3[:, :, HALF:2 * HALF])
            new['a_v'].append(p3[:, :, 2 * HALF:3 * HALF])
            new['b_c'].append(bc)
            new['b_n'].append(bn[:, :, 0, :])
            new['b_m'].append(bm[:, 0, :])
            new['b_conv'].append(p3[:, t - (B_CONV - 1):, 3 * HALF:5 * HALF])
            x = outproj(x, oa, hb, p['ev_w_out'][e][:HALF], p['ev_w_out'][e][HALF:])
        else:
            o = layer // 2
            proj = rms_matmul(x, g_mix, p['od_w_in'][o])
            sp = p['s5'][o]
            oc, cre, cim = s5(proj, sp['ar'], sp['ai'], sp['bbr'], sp['bbi'], sp['cre'], sp['cim'],
                              p['c_d'][o][None], p['c_w_glu'][o],
                              st['c_re'][o].reshape(bsz, 1, -1), st['c_im'][o].reshape(bsz, 1, -1),
                              bsz=bsz, t=t, tb_len=seq_blk, n_pad=n_pad)
            od, ds = rwkv(proj, st['d_shift'][o], _pairs_from_heads(st['d_s'][o]),
                          p['d_mu'][o][None], p['d_w0'][o][None], p['d_w_w2'][o], p['d_a0'][o][None],
                          p['d_w_a2'][o], p['d_w_g2'][o], p['d_k_k'][o][None], p['d_k_a'][o][None],
                          p['d_r_k'][o][None], p['d_ln_g'][o][None], p['d_ln_b'][o][None],
                          bsz=bsz, t=t, tb_len=seq_blk, n_pad=n_pad)
            new['c_re'].append(cre.reshape(bsz, C_GROUPS, C_STATE))
            new['c_im'].append(cim.reshape(bsz, C_GROUPS, C_STATE))
            new['d_s'].append(_heads_from_pairs(ds))
            new['d_shift'].append(proj.reshape(bsz, t, -1)[:, -1:, HALF:])
            x = outproj(x, oc, od, p['od_w_out'][o][:HALF], p['od_w_out'][o][HALF:])
        last = layer == depth - 1
        x = ffn(x, p['norm_ffn'][layer][None], p['ffn_w1'][layer], p['ffn_w3'][layer], p['ffn_w2'][layer],
                gf=p['norm_final'][None] if last else None, drop_rows=drop_rows if last else 0)
    return x, new


def kernel(x_prompt, x_sample, cache_a_k, cache_a_v, state_b_c, state_b_n, state_b_m, state_b_conv, state_c_re, state_c_im, state_d_s, state_d_shift, meta_tokens, rel_bias, norm_mix, norm_ffn, norm_final, ev_w_in, ev_w_out, a_lq1, a_lk1, a_lq2, a_lk2, a_subln, b_conv_w, b_conv_b, b_ig_bias, b_fg_bias, b_norm, od_w_in, od_w_out, c_lam_re, c_lam_im, c_log_dt, c_b_re, c_b_im, c_c_re, c_c_im, c_d, c_w_glu, d_mu, d_w0, d_w_w2, d_a0, d_w_a2, d_w_g2, d_k_k, d_k_a, d_r_k, d_ln_g, d_ln_b, ffn_w1, ffn_w3, ffn_w2):
    bp, sp_len, dm = x_prompt.shape
    bs, ts, _ = x_sample.shape
    n_even, n_odd = ev_w_in.shape[0], od_w_in.shape[0]
    assert bp == 1 and sp_len % CHUNK == 0 and ts % RWKV_GRP == 0 and ts >= B_CONV - 1
    dt = x_prompt.dtype

    ns = C_GROUPS * C_STATE
    s5p = []
    for o in range(n_odd):
        bre = jnp.tile(c_b_re[o].transpose(2, 0, 1).reshape(C_GROUP, ns), (C_GROUPS, 1))
        bim = jnp.tile(c_b_im[o].transpose(2, 0, 1).reshape(C_GROUP, ns), (C_GROUPS, 1))
        ar, ai, bbr, bbi = s5_params(c_lam_re[o].reshape(1, ns), c_lam_im[o].reshape(1, ns),
                                     jnp.repeat(c_log_dt[o], C_STATE)[None], bre, bim)
        eye = jnp.eye(C_GROUPS, dtype=F32)
        blk = lambda c: (eye[:, None, :, None] * c.transpose(0, 2, 1)[:, :, None, :]).reshape(ns, C_WIDTH).astype(BF16)
        s5p.append(dict(ar=ar, ai=ai, bbr=bbr, bbi=bbi, cre=blk(c_c_re[o]), cim=blk(c_c_im[o])))
    p = dict(norm_mix=norm_mix, norm_ffn=norm_ffn, norm_final=norm_final,
             ev_w_in=[_pad_cols(ev_w_in[e]).astype(BF16) for e in range(n_even)],
             ev_w_out=ev_w_out.astype(BF16),
             od_w_in=[od_w_in[o].astype(BF16) for o in range(n_odd)], od_w_out=od_w_out.astype(BF16),
             b_conv_w=b_conv_w, b_conv_b=b_conv_b, b_ig_bias=b_ig_bias, b_fg_bias=b_fg_bias, b_norm=b_norm,
             s5=s5p, c_d=c_d, c_w_glu=c_w_glu.astype(BF16),
             d_mu=d_mu, d_w0=d_w0, d_w_w2=d_w_w2.astype(BF16), d_a0=d_a0, d_w_a2=d_w_a2.astype(BF16),
             d_w_g2=d_w_g2.astype(BF16), d_k_k=d_k_k, d_k_a=d_k_a, d_r_k=d_r_k, d_ln_g=d_ln_g, d_ln_b=d_ln_b,
             ffn_w1=ffn_w1.astype(BF16), ffn_w3=ffn_w3.astype(BF16), ffn_w2=ffn_w2.astype(BF16))
    lam_vecs = lambda e: (a_lq1[e][None], a_lk1[e][None], a_lq2[e][None], a_lk2[e][None], a_subln[e][None])

    tp = -(-(sp_len + CHUNK) // ATT_BLK) * ATT_BLK
    n_pad = tp - sp_len - N_META
    xp = jnp.concatenate([jnp.zeros((n_pad, dm), dt), meta_tokens.astype(dt), x_prompt[0]], axis=0)
    bias2 = jnp.stack([bias_tile(rel_bias, ATT_BLK, ATT_BLK, 0, causal=True, scale=LOG2E),
                       bias_tile(rel_bias, ATT_BLK, ATT_BLK, -ATT_BLK, scale=LOG2E)], axis=0)

    def attn_p(proj, e, lam_init):
        return attn_prompt(proj, bias2, rel_bias, *lam_vecs(e), n_pad=n_pad, lam_init=lam_init)

    zeros = lambda *s: jnp.zeros(s, F32)
    st_p = dict(b_c=zeros(n_even, bp, B_HEADS, B_DK, B_DK), b_n=zeros(n_even, bp, B_HEADS, B_DK),
                b_m=zeros(n_even, bp, B_HEADS), b_conv=zeros(n_even, bp, B_CONV - 1, 2 * HALF),
                c_re=zeros(n_odd, bp, C_GROUPS, C_STATE), c_im=zeros(n_odd, bp, C_GROUPS, C_STATE),
                d_s=zeros(n_odd, bp, D_HEADS, D_HEAD, D_HEAD), d_shift=zeros(n_odd, bp, 1, d_mu.shape[1]))
    lead = n_pad + N_META
    drop = lead if lead % ROW_TILE == 0 else 0
    y_p, new_p = _trunk(xp, p, bsz=bp, t=tp, seq_blk=SEQ_BLK, n_pad=n_pad, attn_fn=attn_p, st=st_p, drop_rows=drop)

    tk = cache_a_k.shape[2]
    past_len = tk - N_META
    cid = lambda pos: np.where(pos < N_META, 0, 1 + (pos - N_META) // CHUNK)
    q_pos = N_META + past_len + np.arange(ts)
    k_pos = np.arange(tk + ts)
    assert (cid(k_pos)[None, :] <= cid(q_pos)[:, None]).all()
    bias_past = bias_tile(rel_bias, ts, tk, -(N_META + past_len))
    bias_new = bias_tile(rel_bias, ts, ts, 0)

    def attn_s(proj, e, lam_init):
        return attn_sample(proj, cache_a_k[e].reshape(bs, tk, HALF), cache_a_v[e].reshape(bs, tk, HALF),
                           bias_past, bias_new, *lam_vecs(e), ts=ts, lam_init=lam_init)

    st_s = dict(b_c=state_b_c, b_n=state_b_n, b_m=state_b_m, b_conv=state_b_conv, c_re=state_c_re,
                c_im=state_c_im, d_s=state_d_s, d_shift=state_d_shift)
    y_s, new_s = _trunk(x_sample.reshape(bs * ts, dm), p, bsz=bs, t=ts, seq_blk=ts, n_pad=0, attn_fn=attn_s, st=st_s)

    def pack(new, bsz, t, drop):
        kv = lambda a: a[:, drop:].reshape(bsz, t - drop, A_HEADS, A_DV)
        return (jnp.stack([kv(a) for a in new['a_k']]), jnp.stack([kv(a) for a in new['a_v']]),
                jnp.stack(new['b_c']), jnp.stack(new['b_n']), jnp.stack(new['b_m']), jnp.stack(new['b_conv']),
                jnp.stack(new['c_re']), jnp.stack(new['c_im']), jnp.stack(new['d_s']), jnp.stack(new['d_shift']))

    out_p = pack(new_p, bp, tp, n_pad)
    out_s = pack(new_s, bs, ts, 0)
    return (y_p[lead - drop:][None], y_s.reshape(bs, ts, dm)) + out_p + out_s
```

```python
import functools
import math

import numpy as np
import jax
import jax.numpy as jnp
from jax import lax
from jax.experimental import pallas as pl
from jax.experimental.pallas import tpu as pltpu

F32 = jnp.float32
BF16 = jnp.bfloat16
I32 = jnp.int32

CHUNK = 64
N_META = 16
EPS = 1e-6
A_HEADS = 4
A_DH = 64
A_DV = 128
N_BUCKETS = 32
MAX_DIST = 128
B_HEADS = 4
B_DK = 128
B_CONV = 4
C_GROUP = 16
C_GROUPS = 32
C_STATE = 64
C_WIDTH = C_GROUP * C_GROUPS
D_HEAD = 64
D_HEADS = 8
D_WIDTH = D_HEAD * D_HEADS
D_GN_EPS = 64e-5
HALF = 512

NEG = -1e30
LOG2E = math.log2(math.e)
LANES = 128
VMEM_LIMIT = 56 * 1024 * 1024

ROW_TILE = 512
ATT_BLK = 512
ATT_STRIP = 64
SEQ_BLK = 128
S5_GRP = 8
S5_SHIFTS = (1, 2, 4)
RWKV_GRP = 32
FFN_TH = 1408


def _cparams(sem):
    return pltpu.CompilerParams(dimension_semantics=sem, vmem_limit_bytes=VMEM_LIMIT)


def _dot(a, b):
    return jnp.dot(a, b, preferred_element_type=F32)


def _dot_nt(a, b):
    return lax.dot_general(a, b, (((1,), (1,)), ((), ())), preferred_element_type=F32)


def _dot_tn(a, b):
    return lax.dot_general(a, b, (((0,), (0,)), ((), ())), preferred_element_type=F32)


def _sigmoid(x):
    return 1.0 / (1.0 + jnp.exp(-x))


def _softplus(x):
    return jnp.maximum(x, 0.0) + jnp.log1p(jnp.exp(-jnp.abs(x)))


def _rms(x, g):
    return x * lax.rsqrt(jnp.mean(x * x, axis=-1, keepdims=True) + EPS) * g


def _segsum(x, ones):
    n, w = x.shape[0], ones.shape[0]
    hi = x.astype(BF16)
    lo = (x - hi.astype(F32)).astype(BF16)
    outs = []
    for c in range(x.shape[1] // w):
        res = _dot(jnp.concatenate([hi[:, c * w:(c + 1) * w], lo[:, c * w:(c + 1) * w]], axis=0), ones)
        outs.append(res[:n] + res[n:])
    return jnp.concatenate(outs, axis=1)


def _block_ones(n, seg):
    r = lax.broadcasted_iota(I32, (n, n), 0) // seg
    c = lax.broadcasted_iota(I32, (n, n), 1) // seg
    return jnp.where(r == c, 1.0, 0.0).astype(BF16)


def _rms_mm_kernel(x_ref, g_ref, w_ref, o_ref):
    h = _rms(x_ref[...], g_ref[...])
    o_ref[...] = _dot(h.astype(BF16), w_ref[...])


def rms_matmul(x, g, w, tm=ROW_TILE):
    m, d = x.shape
    tm = min(tm, m)
    n = w.shape[1]
    return pl.pallas_call(
        _rms_mm_kernel,
        grid=(m // tm,),
        in_specs=[pl.BlockSpec((tm, d), lambda i: (i, 0)),
                  pl.BlockSpec((1, d), lambda i: (0, 0)),
                  pl.BlockSpec((d, n), lambda i: (0, 0))],
        out_specs=pl.BlockSpec((tm, n), lambda i: (i, 0)),
        out_shape=jax.ShapeDtypeStruct((m, n), F32),
        compiler_params=_cparams(("parallel",)),
        name="rms_matmul",
    )(x, g, w)


def _outproj_kernel(x_ref, a_ref, b_ref, wa_ref, wb_ref, o_ref):
    acc = _dot(a_ref[...].astype(BF16), wa_ref[...]) + _dot(b_ref[...].astype(BF16), wb_ref[...])
    o_ref[...] = x_ref[...] + acc


def outproj(x, a, b, wa, wb, tm=ROW_TILE):
    m, d = x.shape
    tm = min(tm, m)
    k = a.shape[1]
    row = lambda i: (i, 0)
    fix = lambda i: (0, 0)
    return pl.pallas_call(
        _outproj_kernel,
        grid=(m // tm,),
        in_specs=[pl.BlockSpec((tm, d), row), pl.BlockSpec((tm, k), row), pl.BlockSpec((tm, k), row),
                  pl.BlockSpec((k, d), fix), pl.BlockSpec((k, d), fix)],
        out_specs=pl.BlockSpec((tm, d), row),
        out_shape=jax.ShapeDtypeStruct((m, d), F32),
        compiler_params=_cparams(("parallel",)),
        name="outproj",
    )(x, a, b, wa, wb)


def _ffn_kernel(x_ref, g_ref, gf_ref, w1_ref, w3_ref, w2_ref, o_ref, h_scr, *, final_norm):
    j = pl.program_id(1)

    @pl.when(j == 0)
    def _():
        x = x_ref[...]
        h_scr[...] = _rms(x, g_ref[...]).astype(BF16)
        o_ref[...] = x

    h = h_scr[...]
    a = _dot(h, w1_ref[...])
    b = _dot(h, w3_ref[...])
    u = (a * _sigmoid(a)) * b
    o_ref[...] += _dot(u.astype(BF16), w2_ref[...])

    if final_norm:
        @pl.when(j == pl.num_programs(1) - 1)
        def _():
            o_ref[...] = _rms(o_ref[...], gf_ref[...])


def ffn(x, g, w1, w3, w2, gf=None, drop_rows=0, tm=ROW_TILE, th=FFN_TH):
    m, d = x.shape
    tm = min(tm, m)
    hid = w1.shape[1]
    final_norm = gf is not None
    if gf is None:
        gf = g
    skip, rem = divmod(drop_rows, tm)
    assert rem == 0
    return pl.pallas_call(
        functools.partial(_ffn_kernel, final_norm=final_norm),
        grid=(m // tm, hid // th),
        in_specs=[pl.BlockSpec((tm, d), lambda i, j: (i, 0)),
                  pl.BlockSpec((1, d), lambda i, j: (0, 0)),
                  pl.BlockSpec((1, d), lambda i, j: (0, 0)),
                  pl.BlockSpec((d, th), lambda i, j: (0, j)),
                  pl.BlockSpec((d, th), lambda i, j: (0, j)),
                  pl.BlockSpec((th, d), lambda i, j: (j, 0))],
        out_specs=pl.BlockSpec((tm, d), lambda i, j: (jnp.maximum(i - skip, 0), 0)),
        out_shape=jax.ShapeDtypeStruct((m - drop_rows, d), F32),
        scratch_shapes=[pltpu.VMEM((tm, d), BF16)],
        compiler_params=_cparams(("arbitrary" if skip else "parallel", "arbitrary")),
        name="ffn",
    )(x, g, gf, w1, w3, w2)


def _bias_kernel(rb_ref, o_ref, *, rel0, causal, scale):
    nq, nk = o_ref.shape[1], o_ref.shape[2]
    a = lax.broadcasted_iota(I32, (nq, nk), 0)
    b = lax.broadcasted_iota(I32, (nq, nk), 1)
    rel = b - a + rel0
    nb = N_BUCKETS // 2
    max_exact = nb // 2
    ret = jnp.where(rel > 0, nb, 0)
    n = jnp.abs(rel)
    nf = jnp.maximum(n, 1).astype(F32)
    large = max_exact + (jnp.log(nf / max_exact) / math.log(MAX_DIST / max_exact) * (nb - max_exact)).astype(I32)
    large = jnp.minimum(large, nb - 1)
    bucket = ret + jnp.where(n < max_exact, n, large)
    for h in range(A_HEADS):
        acc = jnp.zeros((nq, nk), F32)
        for bk in range(N_BUCKETS):
            acc = jnp.where(bucket == bk, rb_ref[bk, h] * scale, acc)
        if causal:
            acc = jnp.where((b // CHUNK) <= (a // CHUNK), acc, NEG)
        o_ref[h] = acc


def bias_tile(rel_bias, nq, nk, rel0, causal=False, scale=1.0):
    return pl.pallas_call(
        functools.partial(_bias_kernel, rel0=rel0, causal=causal, scale=scale),
        in_specs=[pl.BlockSpec(memory_space=pltpu.SMEM)],
        out_specs=pl.BlockSpec(memory_space=pltpu.VMEM),
        out_shape=jax.ShapeDtypeStruct((A_HEADS, nq, nk), F32),
        compiler_params=pltpu.CompilerParams(vmem_limit_bytes=VMEM_LIMIT),
        name="bias_tile",
    )(rel_bias)


def _lambda(lq1_ref, lk1_ref, lq2_ref, lk2_ref, lam_init):
    s1 = jnp.sum(lq1_ref[...] * lk1_ref[...], axis=-1, keepdims=True)
    s2 = jnp.sum(lq2_ref[...] * lk2_ref[...], axis=-1, keepdims=True)
    return jnp.exp(s1) - jnp.exp(s2) + lam_init


def _attn_finish(acc0, l0, acc1, l1, lam, g, lam_init):
    o = acc0 / l0 - lam * (acc1 / l1)
    return _rms(o, g) * (1.0 - lam_init)


def _attn_prompt_kernel(qi_ref, kj_ref, rb_ref, q_ref, k_ref, v_ref, bias_ref,
                        lq1_ref, lk1_ref, lq2_ref, lk2_ref, g_ref, o_ref,
                        m_scr, l_scr, acc_scr, s_scr, p_scr, al_scr, *, blk, n_pad, lam_init):
    s = pl.program_id(0)
    i = qi_ref[s]
    j = kj_ref[s]
    d = i - j
    nct = blk // LANES
    nstrip = blk // ATT_STRIP

    @pl.when(j == 0)
    def _():
        m_scr[...] = jnp.full(m_scr.shape, NEG, F32)
        l_scr[...] = jnp.zeros(l_scr.shape, F32)
        acc_scr[...] = jnp.zeros(acc_scr.shape, F32)

    def update(general):
        for h in range(A_HEADS):
            far = rb_ref[N_BUCKETS // 2 - 1, h] * LOG2E
            for mm in range(2):
                c0 = h * 2 * A_DH + mm * A_DH
                qb = (q_ref[:, c0:c0 + A_DH] * (A_DH ** -0.5 * LOG2E)).astype(BF16)
                s_scr[2 * h + mm] = _dot_nt(qb, k_ref[:, c0:c0 + A_DH].astype(BF16))
            for mm in range(2):
                idx = 2 * h + mm
                for r in range(nstrip):
                    rows = slice(r * ATT_STRIP, (r + 1) * ATT_STRIP)
                    tiles = []
                    for c in range(nct):
                        cols = slice(c * LANES, (c + 1) * LANES)
                        t = s_scr[idx, rows, cols]
                        if general:
                            kpos = j * blk + c * LANES + lax.broadcasted_iota(I32, (1, LANES), 1)
                            near = bias_ref[jnp.minimum(d, 1), h, rows, cols]
                            t = t + jnp.where(d < 2, near, far) + jnp.where(kpos < n_pad, NEG, 0.0)
                        tiles.append(t)
                    mx = functools.reduce(jnp.maximum, tiles)
                    m_cur = jnp.broadcast_to(jnp.max(mx, axis=-1, keepdims=True), (ATT_STRIP, LANES))
                    if not general:
                        m_cur = m_cur + far
                    m_old = m_scr[idx, rows, :]
                    m_new = jnp.maximum(m_old, m_cur)
                    alpha = jnp.exp2(m_old - m_new)
                    m_sub = m_new if general else m_new - far
                    ps = [jnp.exp2(t - m_sub) for t in tiles]
                    l_scr[idx, rows, :] = alpha * l_scr[idx, rows, :] + functools.reduce(jnp.add, ps)
                    m_scr[idx, rows, :] = m_new
                    al_scr[idx, rows, :] = alpha
                    for c in range(nct):
                        p_scr[h, mm * blk + r * ATT_STRIP:mm * blk + (r + 1) * ATT_STRIP,
                              c * LANES:(c + 1) * LANES] = ps[c].astype(BF16)
            pv = _dot(p_scr[h], v_ref[:, h * A_DV:(h + 1) * A_DV].astype(BF16))
            for mm in range(2):
                idx = 2 * h + mm
                acc_scr[idx] = al_scr[idx] * acc_scr[idx] + pv[mm * blk:(mm + 1) * blk]

    is_far = jnp.logical_and(d >= 2, j > 0)

    @pl.when(is_far)
    def _():
        update(False)

    @pl.when(jnp.logical_not(is_far))
    def _():
        update(True)

    @pl.when(d == 0)
    def _():
        lam = _lambda(lq1_ref, lk1_ref, lq2_ref, lk2_ref, lam_init)
        rows = i * blk + lax.broadcasted_iota(I32, (blk, 1), 0)
        valid = rows >= n_pad
        for h in range(A_HEADS):
            l0 = jnp.sum(l_scr[2 * h], axis=-1, keepdims=True)
            l1 = jnp.sum(l_scr[2 * h + 1], axis=-1, keepdims=True)
            y = _attn_finish(acc_scr[2 * h], l0, acc_scr[2 * h + 1], l1, lam, g_ref[...], lam_init)
            o_ref[:, h * A_DV:(h + 1) * A_DV] = jnp.where(valid, y, 0.0)


def attn_prompt(proj, bias2, rel_bias, lq1, lk1, lq2, lk2, g, *, n_pad, lam_init, blk=ATT_BLK):
    t = proj.shape[0]
    nb = t // blk
    qi = np.array([i for i in range(nb) for _ in range(i + 1)], np.int32)
    kj = np.array([j for i in range(nb) for j in range(i + 1)], np.int32)
    vec = lambda n: pl.BlockSpec((1, n), lambda s, qi, kj: (0, 0))
    grid_spec = pltpu.PrefetchScalarGridSpec(
        num_scalar_prefetch=2,
        grid=(len(qi),),
        in_specs=[pl.BlockSpec(memory_space=pltpu.SMEM),
                  pl.BlockSpec((blk, HALF), lambda s, qi, kj: (qi[s], 0)),
                  pl.BlockSpec((blk, HALF), lambda s, qi, kj: (kj[s], 1)),
                  pl.BlockSpec((blk, HALF), lambda s, qi, kj: (kj[s], 2)),
                  pl.BlockSpec((2, A_HEADS, blk, blk), lambda s, qi, kj: (0, 0, 0, 0)),
                  vec(A_DH), vec(A_DH), vec(A_DH), vec(A_DH), vec(A_DV)],
        out_specs=pl.BlockSpec((blk, HALF), lambda s, qi, kj: (qi[s], 0)),
        scratch_shapes=[pltpu.VMEM((2 * A_HEADS, blk, LANES), F32),
                        pltpu.VMEM((2 * A_HEADS, blk, LANES), F32),
                        pltpu.VMEM((2 * A_HEADS, blk, A_DV), F32),
                        pltpu.VMEM((2 * A_HEADS, blk, blk), F32),
                        pltpu.VMEM((A_HEADS, 2 * blk, blk), BF16),
                        pltpu.VMEM((2 * A_HEADS, blk, LANES), F32)],
    )
    assert n_pad <= blk and blk >= MAX_DIST
    return pl.pallas_call(
        functools.partial(_attn_prompt_kernel, blk=blk, n_pad=n_pad, lam_init=lam_init),
        grid_spec=grid_spec,
        out_shape=jax.ShapeDtypeStruct((t, HALF), F32),
        compiler_params=_cparams(("arbitrary",)),
        name="attn_prompt",
    )(jnp.asarray(qi), jnp.asarray(kj), rel_bias, proj, proj, proj, bias2, lq1, lk1, lq2, lk2, g)


def _attn_sample_kernel(q_ref, kn_ref, vn_ref, kc_ref, vc_ref, bp_ref, bn_ref,
                        lq1_ref, lk1_ref, lq2_ref, lk2_ref, g_ref, o_ref, *, lam_init):
    lam = _lambda(lq1_ref, lk1_ref, lq2_ref, lk2_ref, lam_init)
    for h in range(A_HEADS):
        vsl = slice(h * A_DV, (h + 1) * A_DV)
        vp = vc_ref[0, :, vsl].astype(BF16)
        vn = vn_ref[:, vsl].astype(BF16)
        outs = []
        for mm in range(2):
            c0 = h * 2 * A_DH + mm * A_DH
            qb = (q_ref[:, c0:c0 + A_DH] * (A_DH ** -0.5)).astype(BF16)
            sp = _dot_nt(qb, kc_ref[0, :, c0:c0 + A_DH].astype(BF16)) + bp_ref[h]
            sn = _dot_nt(qb, kn_ref[:, c0:c0 + A_DH].astype(BF16)) + bn_ref[h]
            mx = jnp.maximum(jnp.max(sp, axis=-1, keepdims=True), jnp.max(sn, axis=-1, keepdims=True))
            pp = jnp.exp(sp - mx)
            pn = jnp.exp(sn - mx)
            l = jnp.sum(pp, axis=-1, keepdims=True) + jnp.sum(pn, axis=-1, keepdims=True)
            acc = _dot(pp.astype(BF16), vp) + _dot(pn.astype(BF16), vn)
            outs.append((acc, l))
        o_ref[:, vsl] = _attn_finish(outs[0][0], outs[0][1], outs[1][0], outs[1][1], lam, g_ref[...], lam_init)


def attn_sample(proj, kc, vc, bias_past, bias_new, lq1, lk1, lq2, lk2, g, *, ts, lam_init):
    bsz, tk, _ = kc.shape
    vec = lambda n: pl.BlockSpec((1, n), lambda b: (0, 0))
    return pl.pallas_call(
        functools.partial(_attn_sample_kernel, lam_init=lam_init),
        grid=(bsz,),
        in_specs=[pl.BlockSpec((ts, HALF), lambda b: (b, 0)),
                  pl.BlockSpec((ts, HALF), lambda b: (b, 1)),
                  pl.BlockSpec((ts, HALF), lambda b: (b, 2)),
                  pl.BlockSpec((1, tk, HALF), lambda b: (b, 0, 0)),
                  pl.BlockSpec((1, tk, HALF), lambda b: (b, 0, 0)),
                  pl.BlockSpec((A_HEADS, ts, tk), lambda b: (0, 0, 0)),
                  pl.BlockSpec((A_HEADS, ts, ts), lambda b: (0, 0, 0)),
                  vec(A_DH), vec(A_DH), vec(A_DH), vec(A_DH), vec(A_DV)],
        out_specs=pl.BlockSpec((ts, HALF), lambda b: (b, 0)),
        out_shape=jax.ShapeDtypeStruct((bsz * ts, HALF), F32),
        compiler_params=_cparams(("parallel",)),
        name="attn_sample",
    )(proj, proj, proj, kc, vc, bias_past, bias_new, lq1, lk1, lq2, lk2, g)


def _mlstm_kernel(bq_ref, bk_ref, bv_ref, bo_ref, gc_ref, gr_ref, c0_ref, n0_ref, m0_ref, conv0_ref,
                  cw_ref, cb_ref, gbc_ref, gbr_ref, bn_ref,
                  h_ref, c_out, n_out, m_out,
                  xbuf, c_scr, n_scr, m_scr, *, ln, n_pad):
    tb = pl.program_id(1)
    nt = pl.num_programs(1)
    halo = B_CONV - 1
    base = 8 - halo

    @pl.when(tb == 0)
    def _():
        xbuf[base:8, :] = conv0_ref[0]
        c_scr[...] = c0_ref[0]
        n_scr[...] = n0_ref[0]
        for h in range(B_HEADS):
            m_scr[h] = m0_ref[0, :, h:h + 1]

    xbuf[8:8 + ln, 0:HALF] = bq_ref[...]
    xbuf[8:8 + ln, HALF:2 * HALF] = bk_ref[...]
    conv = cb_ref[...] + cw_ref[0:1, :] * xbuf[base:base + ln, :]
    for jj in range(1, B_CONV):
        conv = conv + cw_ref[jj:jj + 1, :] * xbuf[base + jj:base + jj + ln, :]
    xbuf[base:8, :] = xbuf[8 + ln - halo:8 + ln, :]
    conv = conv * _sigmoid(conv)
    q_all = conv[:, 0:HALF] * (B_DK ** -0.5)
    k_all = conv[:, HALF:2 * HALF]

    rows = tb * ln + lax.broadcasted_iota(I32, (ln, 1), 0)
    cols = tb * ln + lax.broadcasted_iota(I32, (1, ln), 1)
    valid_c = rows >= n_pad
    valid_r = cols >= n_pad

    gc = gc_ref[...] + gbc_ref[...]
    gr = gr_ref[0] + gbr_ref[...]
    li_c = jnp.where(valid_c, gc, NEG)
    li_r = jnp.where(valid_r, gr, NEG)
    lf_c = jnp.where(valid_c, -_softplus(-gc), 0.0)
    lf_r = jnp.where(valid_r, -_softplus(-gr), 0.0)

    ri = lax.broadcasted_iota(I32, (ln, ln), 0)
    ci = lax.broadcasted_iota(I32, (ln, ln), 1)
    tril = ri >= ci
    tril_f = jnp.where(tril, 1.0, 0.0)
    triu_f = jnp.where(ri <= ci, 1.0, 0.0)
    b_c = jnp.dot(tril_f, lf_c, preferred_element_type=F32, precision=lax.Precision.HIGHEST)
    b_r = jnp.dot(lf_r, triu_f, preferred_element_type=F32, precision=lax.Precision.HIGHEST)

    for h in range(B_HEADS):
        sl = slice(h * B_DK, (h + 1) * B_DK)
        qh = q_all[:, sl].astype(BF16)
        kh = k_all[:, sl]
        vh = bv_ref[:, sl].astype(BF16)
        c = c_scr[h]
        n = n_scr[h]
        m = m_scr[h]
        bc = b_c[:, B_HEADS + h:B_HEADS + h + 1]
        br = b_r[B_HEADS + h:B_HEADS + h + 1, :]
        inter = bc + m
        dmat = jnp.where(tril, bc - br + li_r[h:h + 1, :], NEG)
        mt = jnp.maximum(inter, jnp.max(dmat, axis=-1, keepdims=True))
        w_inter = jnp.exp(inter - mt)
        s = _dot_nt(qh, kh.astype(BF16)) * jnp.exp(dmat - mt)
        num = w_inter * _dot(qh, c.astype(BF16)) + _dot(s.astype(BF16), vh)
        qn = jnp.sum(qh.astype(F32) * n, axis=-1, keepdims=True)
        den = w_inter * qn + jnp.sum(s, axis=-1, keepdims=True)
        hh = num / jnp.maximum(jnp.abs(den), jnp.exp(-mt))
        b_last = bc[ln - 1:ln, :]
        g = b_last - bc + li_c[:, h:h + 1]
        m_new = jnp.maximum(b_last + m, jnp.max(g, axis=0, keepdims=True))
        decay = jnp.exp(b_last + m - m_new)
        wk = (jnp.exp(g - m_new) * kh)
        c_scr[h] = decay * c + _dot_tn(wk.astype(BF16), vh)
        n_scr[h] = decay * n + jnp.sum(wk, axis=0, keepdims=True)
        m_scr[h] = m_new
        hn = _rms(hh, bn_ref[:, sl]) * _sigmoid(bo_ref[:, sl])
        h_ref[:, sl] = jnp.where(valid_c, hn, 0.0)

    @pl.when(tb == nt - 1)
    def _():
        c_out[0] = c_scr[...]
        n_out[0] = n_scr[...]
        lane = lax.broadcasted_iota(I32, (1, B_HEADS), 1)
        mrow = jnp.zeros((1, B_HEADS), F32)
        for h in range(B_HEADS):
            mrow = jnp.where(lane == h, m_scr[h], mrow)
        m_out[0] = mrow


def mlstm(proj, gates_r, c0, n0, m0, conv0, cw, cb, igb, fgb, bnorm, *, bsz, t, ln, n_pad):
    nt = t // ln
    gb = jnp.concatenate([igb, fgb])
    gw = proj.shape[1] // LANES - 1
    row = lambda c: (lambda b, i: (b * nt + i, c))
    fix2 = lambda b, i: (0, 0)
    out_shapes = (jax.ShapeDtypeStruct((bsz * t, HALF), F32),
                  jax.ShapeDtypeStruct((bsz, B_HEADS, B_DK, B_DK), F32),
                  jax.ShapeDtypeStruct((bsz, B_HEADS, 1, B_DK), F32),
                  jax.ShapeDtypeStruct((bsz, 1, B_HEADS), F32))
    return pl.pallas_call(
        functools.partial(_mlstm_kernel, ln=ln, n_pad=n_pad),
        grid=(bsz, nt),
        in_specs=[pl.BlockSpec((ln, HALF), row(3)), pl.BlockSpec((ln, HALF), row(4)),
                  pl.BlockSpec((ln, HALF), row(5)), pl.BlockSpec((ln, HALF), row(6)),
                  pl.BlockSpec((ln, LANES), row(gw)),
                  pl.BlockSpec((1, 2 * B_HEADS, ln), lambda b, i: (b * nt + i, 0, 0)),
                  pl.BlockSpec((1, B_HEADS, B_DK, B_DK), lambda b, i: (b, 0, 0, 0)),
                  pl.BlockSpec((1, B_HEADS, 1, B_DK), lambda b, i: (b, 0, 0, 0)),
                  pl.BlockSpec((1, 1, B_HEADS), lambda b, i: (b, 0, 0)),
                  pl.BlockSpec((1, B_CONV - 1, 2 * HALF), lambda b, i: (b, 0, 0)),
                  pl.BlockSpec((B_CONV, 2 * HALF), fix2), pl.BlockSpec((1, 2 * HALF), fix2),
                  pl.BlockSpec((1, LANES), fix2), pl.BlockSpec((2 * B_HEADS, 1), fix2),
                  pl.BlockSpec((1, HALF), fix2)],
        out_specs=(pl.BlockSpec((ln, HALF), lambda b, i: (b * nt + i, 0)),
                   pl.BlockSpec((1, B_HEADS, B_DK, B_DK), lambda b, i: (b, 0, 0, 0)),
                   pl.BlockSpec((1, B_HEADS, 1, B_DK), lambda b, i: (b, 0, 0, 0)),
                   pl.BlockSpec((1, 1, B_HEADS), lambda b, i: (b, 0, 0))),
        out_shape=out_shapes,
        scratch_shapes=[pltpu.VMEM((ln + 8, 2 * HALF), F32),
                        pltpu.VMEM((B_HEADS, B_DK, B_DK), F32),
                        pltpu.VMEM((B_HEADS, 1, B_DK), F32),
                        pltpu.VMEM((B_HEADS, 1, 1), F32)],
        compiler_params=_cparams(("parallel", "arbitrary")),
        name="mlstm",
    )(proj, proj, proj, proj, proj, gates_r, c0, n0, m0, conv0, cw, cb,
      jnp.pad(gb, (0, LANES - 2 * B_HEADS))[None], gb[:, None], bnorm)


def _s5_param_kernel(lr_ref, li_ref, ldt_ref, bre_ref, bim_ref, pwr_ref, pwi_ref, amr_ref, ami_ref, bbr_ref, bbi_ref):
    lr = lr_ref[...]
    li = li_ref[...]
    dt = jnp.exp(ldt_ref[...])
    mag = jnp.exp(lr * dt)
    ar = mag * jnp.cos(li * dt)
    ai = mag * jnp.sin(li * dt)
    den = lr * lr + li * li
    cr = ((ar - 1.0) * lr + ai * li) / den
    ci = (ai * lr - (ar - 1.0) * li) / den
    sub = lax.broadcasted_iota(I32, pwr_ref.shape, 0)
    pr, pi = ar, ai
    pwr = jnp.zeros(pwr_ref.shape, F32)
    pwi = jnp.zeros(pwr_ref.shape, F32)
    for t in range(S5_GRP):
        pwr = jnp.where(sub == t, pr, pwr)
        pwi = jnp.where(sub == t, pi, pwi)
        if t + 1 in S5_SHIFTS:
            k = S5_SHIFTS.index(t + 1)
            amr_ref[k] = jnp.where(sub >= t + 1, pr, 0.0)
            ami_ref[k] = jnp.where(sub >= t + 1, pi, 0.0)
        pr, pi = pr * ar - pi * ai, pr * ai + pi * ar
    pwr_ref[...] = pwr
    pwi_ref[...] = pwi
    nr, nc = bre_ref.shape
    same = (lax.broadcasted_iota(I32, (nr, nc), 0) // C_GROUP) == (lax.broadcasted_iota(I32, (nr, nc), 1) // C_STATE)
    br = bre_ref[...]
    bi = bim_ref[...]
    bbr_ref[...] = jnp.where(same, cr * br - ci * bi, 0.0).astype(BF16)
    bbi_ref[...] = jnp.where(same, cr * bi + ci * br, 0.0).astype(BF16)


def s5_params(lr, li, ldt, bre_rep, bim_rep):
    ns = lr.shape[1]
    vm = pl.BlockSpec(memory_space=pltpu.VMEM)
    return pl.pallas_call(
        _s5_param_kernel,
        in_specs=[vm] * 5,
        out_specs=(vm,) * 6,
        out_shape=(jax.ShapeDtypeStruct((S5_GRP, ns), F32), jax.ShapeDtypeStruct((S5_GRP, ns), F32),
                   jax.ShapeDtypeStruct((len(S5_SHIFTS), S5_GRP, ns), F32),
                   jax.ShapeDtypeStruct((len(S5_SHIFTS), S5_GRP, ns), F32),
                   jax.ShapeDtypeStruct(bre_rep.shape, BF16), jax.ShapeDtypeStruct(bre_rep.shape, BF16)),
        compiler_params=pltpu.CompilerParams(vmem_limit_bytes=VMEM_LIMIT),
        name="s5_params",
    )(lr, li, ldt, bre_rep, bim_rep)


def _s5_kernel(u_ref, pwr_ref, pwi_ref, amr_ref, ami_ref, bbr_ref, bbi_ref, cre_ref, cim_ref, d_ref, wg_ref, x0r_ref, x0i_ref,
               o_ref, xr_out, xi_out, sr_scr, si_scr, bur, bui, xra, xia, *, tb_len, n_pad):
    tb = pl.program_id(1)
    nt = pl.num_programs(1)

    @pl.when(tb == 0)
    def _():
        sr_scr[...] = x0r_ref[0]
        si_scr[...] = x0i_ref[0]

    u = u_ref[...]
    ub = u.astype(BF16)
    bur[...] = _dot(ub, bbr_ref[...])
    bui[...] = _dot(ub, bbi_ref[...])
    def body(grp, carry):
        xr_prev, xi_prev = carry
        base = pl.multiple_of(grp * S5_GRP, S5_GRP)
        xr = bur[pl.ds(base, S5_GRP), :]
        xi = bui[pl.ds(base, S5_GRP), :]
        for k, d in enumerate(S5_SHIFTS):
            sr = pltpu.roll(xr, d, 0)
            si = pltpu.roll(xi, d, 0)
            ar, ai = amr_ref[k], ami_ref[k]
            xr, xi = xr + ar * sr - ai * si, xi + ar * si + ai * sr
        pr, pi = pwr_ref[...], pwi_ref[...]
        xr, xi = xr + pr * xr_prev - pi * xi_prev, xi + pr * xi_prev + pi * xr_prev
        xra[pl.ds(base, S5_GRP), :] = xr
        xia[pl.ds(base, S5_GRP), :] = xi
        return xr[S5_GRP - 1:S5_GRP], xi[S5_GRP - 1:S5_GRP]

    xr, xi = lax.fori_loop(0, tb_len // S5_GRP, body, (sr_scr[...], si_scr[...]))
    sr_scr[...] = xr
    si_scr[...] = xi

    y = _dot(xra[...].astype(BF16), cre_ref[...]) - _dot(xia[...].astype(BF16), cim_ref[...]) + d_ref[...] * u
    yg = 0.5 * y * (1.0 + jnp.tanh(math.sqrt(2.0 / math.pi) * (y + 0.044715 * (y * y * y))))
    oc = yg * _sigmoid(_dot(yg.astype(BF16), wg_ref[...]))
    rows = tb * tb_len + lax.broadcasted_iota(I32, (tb_len, 1), 0)
    o_ref[...] = jnp.where(rows >= n_pad, oc, 0.0)

    @pl.when(tb == nt - 1)
    def _():
        xr_out[0] = xr
        xi_out[0] = xi


def s5(proj, pwr, pwi, amr, ami, bbr, bbi, cre, cim, dskip, wglu, x0r, x0i, *, bsz, t, tb_len, n_pad):
    nt = t // tb_len
    ns = pwr.shape[1]
    fix2 = lambda b, i: (0, 0)
    st = pl.BlockSpec((1, 1, ns), lambda b, i: (b, 0, 0))
    return pl.pallas_call(
        functools.partial(_s5_kernel, tb_len=tb_len, n_pad=n_pad),
        grid=(bsz, nt),
        in_specs=[pl.BlockSpec((tb_len, HALF), lambda b, i: (b * nt + i, 0)),
                  pl.BlockSpec((S5_GRP, ns), fix2), pl.BlockSpec((S5_GRP, ns), fix2),
                  pl.BlockSpec(amr.shape, lambda b, i: (0, 0, 0)), pl.BlockSpec(amr.shape, lambda b, i: (0, 0, 0)),
                  pl.BlockSpec((HALF, ns), fix2), pl.BlockSpec((HALF, ns), fix2),
                  pl.BlockSpec((ns, HALF), fix2), pl.BlockSpec((ns, HALF), fix2),
                  pl.BlockSpec((1, HALF), fix2), pl.BlockSpec((HALF, HALF), fix2), st, st],
        out_specs=(pl.BlockSpec((tb_len, HALF), lambda b, i: (b * nt + i, 0)), st, st),
        out_shape=(jax.ShapeDtypeStruct((bsz * t, HALF), F32),
                   jax.ShapeDtypeStruct((bsz, 1, ns), F32), jax.ShapeDtypeStruct((bsz, 1, ns), F32)),
        scratch_shapes=[pltpu.VMEM((1, ns), F32), pltpu.VMEM((1, ns), F32),
                        pltpu.VMEM((tb_len, ns), F32), pltpu.VMEM((tb_len, ns), F32),
                        pltpu.VMEM((tb_len, ns), F32), pltpu.VMEM((tb_len, ns), F32)],
        compiler_params=_cparams(("parallel", "arbitrary")),
        name="s5",
    )(proj, pwr, pwi, amr, ami, bbr, bbi, cre, cim, dskip, wglu, x0r, x0i)


def _rwkv_kernel(r_ref, k_ref, v_ref, lo_ref, sh0_ref, s0_ref,
                 mu_ref, w0_ref, ww2_ref, a0_ref, wa2_ref, wg2_ref, kk_ref, ka_ref, rk_ref, lng_ref, lnb_ref,
                 o_ref, s_out,
                 xbuf, s_scr, w_scr, nkk_scr, b_scr, k_scr, r_scr, v_scr, c_scr, be_scr, ga_scr, y_scr,
                 *, tb_len, n_pad):
    tb = pl.program_id(1)
    nt = pl.num_programs(1)
    npair = D_HEADS // 2
    ncol = xbuf.shape[1]

    @pl.when(tb == 0)
    def _():
        xbuf[7:8, :] = sh0_ref[0]
        s_scr[...] = s0_ref[0]

    xbuf[8:8 + tb_len, 0:D_WIDTH] = r_ref[...]
    xbuf[8:8 + tb_len, D_WIDTH:2 * D_WIDTH] = k_ref[...]
    xbuf[8:8 + tb_len, 2 * D_WIDTH:3 * D_WIDTH] = v_ref[...]
    xbuf[8:8 + tb_len, 3 * D_WIDTH:ncol] = lo_ref[...]
    cur = xbuf[8:8 + tb_len, :]
    prev = xbuf[7:7 + tb_len, :]
    xbuf[7:8, :] = xbuf[7 + tb_len:8 + tb_len, :]
    xm = cur + mu_ref[...] * (prev - cur)
    r = xm[:, 0:D_WIDTH]
    k = xm[:, D_WIDTH:2 * D_WIDTH]
    v = xm[:, 2 * D_WIDTH:3 * D_WIDTH]
    c0 = 3 * D_WIDTH
    wlo = xm[:, c0:c0 + 64]
    alo = xm[:, c0 + 64:c0 + 128]
    glo = xm[:, c0 + 128:c0 + 256]

    w_raw = w0_ref[...] + _dot(jnp.tanh(wlo).astype(BF16), ww2_ref[...])
    decay = jnp.exp(-jnp.exp(-_softplus(-w_raw) - 0.5))
    a = _sigmoid(a0_ref[...] + _dot(alo.astype(BF16), wa2_ref[...]))
    g = _dot(_sigmoid(glo).astype(BF16), wg2_ref[...])

    ones_h = _block_ones(2 * LANES, D_HEAD)
    kk = k * kk_ref[...]
    kk = kk / jnp.maximum(jnp.sqrt(_segsum(kk * kk, ones_h)), 1e-12)
    k2 = k * (1.0 + (a - 1.0) * ka_ref[...])

    nkk = -kk
    bb = kk * a
    w_scr[...] = decay
    nkk_scr[...] = nkk
    b_scr[...] = bb
    k_scr[...] = k2
    r_scr[...] = r
    v_scr[...] = v
    c_scr[...] = pltpu.roll(decay, 1, 0) * nkk
    be_scr[...] = _segsum(pltpu.roll(bb, 1, 0) * nkk, ones_h)
    ga_scr[...] = _segsum(pltpu.roll(k2, 1, 0) * nkk, ones_h)

    eye2 = jnp.where((lax.broadcasted_iota(I32, (D_HEAD, LANES), 1) % D_HEAD)
                     == lax.broadcasted_iota(I32, (D_HEAD, LANES), 0), 1.0, 0.0)

    sub = lax.broadcasted_iota(I32, (RWKV_GRP, LANES), 0)
    pairs = range(npair)
    nrow = npair * D_HEAD
    ri = lax.broadcasted_iota(I32, (2 * LANES, 2 * LANES), 0)
    ci = lax.broadcasted_iota(I32, (2 * LANES, 2 * LANES), 1)
    same_head = (ri % LANES) // D_HEAD == (ci % LANES) // D_HEAD
    ones_s = jnp.where(jnp.logical_and(same_head, ri // LANES >= ci // LANES), 1.0, 0.0).astype(BF16)

    def hilo(parts):
        x = jnp.concatenate(parts, axis=0)
        hi = x.astype(BF16)
        return jnp.concatenate([hi, (x - hi.astype(F32)).astype(BF16)], axis=1)

    def per_pair(res):
        return [res[p * D_HEAD:(p + 1) * D_HEAD] for p in pairs]

    def y_rows(yt, ycol, i):
        return [jnp.where(sub == i, jnp.sum(ycol[p] * eye2, axis=0, keepdims=True), yt[p]) for p in pairs]

    def body(grp, state):
        base = pl.multiple_of(grp * RWKV_GRP, RWKV_GRP)
        tile = lambda scr: [scr[pl.ds(base, RWKV_GRP), p * LANES:(p + 1) * LANES] for p in pairs]
        nkk_t, c_t, be_t, ga_t = tile(nkk_scr), tile(c_scr), tile(be_scr), tile(ga_scr)
        w_t, b_t, k_t, r_t, v_t = tile(w_scr), tile(b_scr), tile(k_scr), tile(r_scr), tile(v_scr)
        vh = [x.astype(BF16).astype(F32) for x in v_t]
        vl = [x - h for x, h in zip(v_t, vh)]

        def vcol_lhs(i):
            return jnp.concatenate([jnp.concatenate([vh[p][i:i + 1] * eye2 for p in pairs], axis=0).astype(BF16),
                                    jnp.concatenate([vl[p][i:i + 1] * eye2 for p in pairs], axis=0).astype(BF16)],
                                   axis=1)

        def segsums(lhs):
            res = _dot(jnp.concatenate(lhs, axis=0), ones_s)
            return [res[i * nrow:(i + 1) * nrow] for i in range(len(lhs))]

        def read_out(yt, res, i):
            second = res[:, LANES:2 * LANES]
            yt = y_rows(yt, per_pair(res[:, 0:LANES] - second), i)
            return y_rows(yt, per_pair(second), i + 1)

        yt = [jnp.zeros((RWKV_GRP, LANES), F32) for _ in pairs]
        sp = list(state)
        vc = [per_pair(r[:, 0:LANES]) for r in segsums([vcol_lhs(0), vcol_lhs(1)])]
        yprods = None
        for t0 in range(0, RWKV_GRP, 2):
            t1 = t0 + 1
            r0, r1 = slice(t0, t0 + 1), slice(t1, t1 + 1)
            more = t0 + 2 < RWKV_GRP
            lhs = [hilo([sp[p] * nkk_t[p][r0] for p in pairs]), hilo([sp[p] * c_t[p][r1] for p in pairs])]
            if more:
                lhs += [vcol_lhs(t0 + 2), vcol_lhs(t0 + 3)]
            if yprods is not None:
                lhs.append(yprods)
            res = segsums(lhs)
            sa0, tmp = per_pair(res[0][:, 0:LANES]), per_pair(res[1][:, 0:LANES])
            if yprods is not None:
                yt = read_out(yt, res[-1], t0 - 2)
            s0 = [sp[p] * w_t[p][r0] + sa0[p] * b_t[p][r0] + vc[0][p] * k_t[p][r0] for p in pairs]
            sa1 = [tmp[p] + sa0[p] * be_t[p][r1] + vc[0][p] * ga_t[p][r1] for p in pairs]
            sp = [s0[p] * w_t[p][r1] + sa1[p] * b_t[p][r1] + vc[1][p] * k_t[p][r1] for p in pairs]
            yprods = jnp.concatenate(
                [jnp.concatenate([s0[p] * r_t[p][r0] for p in pairs], axis=0).astype(BF16),
                 jnp.concatenate([sp[p] * r_t[p][r1] for p in pairs], axis=0).astype(BF16)], axis=1)
            if more:
                vc = [per_pair(res[2][:, 0:LANES]), per_pair(res[3][:, 0:LANES])]
        yt = read_out(yt, segsums([yprods])[0], RWKV_GRP - 2)
        for p in pairs:
            y_scr[pl.ds(base, RWKV_GRP), p * LANES:(p + 1) * LANES] = yt[p]
        return tuple(sp)

    state = lax.fori_loop(0, tb_len // RWKV_GRP, body, tuple(s_scr[p] for p in pairs))
    for p in pairs:
        s_scr[p] = state[p]

    y = y_scr[...]
    inv = 1.0 / D_HEAD
    mean = _segsum(y, ones_h) * inv
    yc = y - mean
    var = _segsum(yc * yc, ones_h) * inv
    y = yc * lax.rsqrt(var + D_GN_EPS) * lng_ref[...] + lnb_ref[...]
    y = y + _segsum(r * k2 * rk_ref[...], ones_h) * v
    rows = tb * tb_len + lax.broadcasted_iota(I32, (tb_len, 1), 0)
    o_ref[...] = jnp.where(rows >= n_pad, y * g, 0.0)

    @pl.when(tb == nt - 1)
    def _():
        s_out[0] = s_scr[...]


def rwkv(proj, sh0, s0, mu, w0, ww2, a0, wa2, wg2, kkp, kap, rkp, lng, lnb, *, bsz, t, tb_len, n_pad):
    nt = t // tb_len
    npair = D_HEADS // 2
    ncol = mu.shape[1]
    nlo = ncol - 3 * D_WIDTH
    row = lambda c: (lambda b, i: (b * nt + i, c))
    fix2 = lambda b, i: (0, 0)
    vec = pl.BlockSpec((1, D_WIDTH), fix2)
    st = pl.BlockSpec((1, npair, D_HEAD, LANES), lambda b, i: (b, 0, 0, 0))
    big = lambda: pltpu.VMEM((tb_len, D_WIDTH), F32)
    return pl.pallas_call(
        functools.partial(_rwkv_kernel, tb_len=tb_len, n_pad=n_pad),
        grid=(bsz, nt),
        in_specs=[pl.BlockSpec((tb_len, D_WIDTH), row(1)), pl.BlockSpec((tb_len, D_WIDTH), row(2)),
                  pl.BlockSpec((tb_len, D_WIDTH), row(3)),
                  pl.BlockSpec((tb_len, nlo), row(4 * D_WIDTH // nlo)),
                  pl.BlockSpec((1, 1, ncol), lambda b, i: (b, 0, 0)), st,
                  pl.BlockSpec((1, ncol), fix2), vec,
                  pl.BlockSpec(ww2.shape, fix2), vec, pl.BlockSpec(wa2.shape, fix2), pl.BlockSpec(wg2.shape, fix2),
                  vec, vec, vec, vec, vec],
        out_specs=(pl.BlockSpec((tb_len, D_WIDTH), lambda b, i: (b * nt + i, 0)), st),
        out_shape=(jax.ShapeDtypeStruct((bsz * t, D_WIDTH), F32),
                   jax.ShapeDtypeStruct((bsz, npair, D_HEAD, LANES), F32)),
        scratch_shapes=[pltpu.VMEM((tb_len + 8, ncol), F32), pltpu.VMEM((npair, D_HEAD, LANES), F32),
                        big(), big(), big(), big(), big(), big(), big(), big(), big(), big()],
        compiler_params=_cparams(("parallel", "arbitrary")),
        name="rwkv7",
    )(proj, proj, proj, proj, sh0, s0, mu, w0, ww2, a0, wa2, wg2, kkp, kap, rkp, lng, lnb)


def _pad_cols(w, mult=LANES):
    n = w.shape[1]
    return jnp.pad(w, ((0, 0), (0, (-n) % mult)))


def _pairs_from_heads(s):
    b = s.shape[0]
    return s.reshape(b, D_HEADS // 2, 2, D_HEAD, D_HEAD).transpose(0, 1, 3, 2, 4).reshape(b, D_HEADS // 2, D_HEAD, LANES)


def _heads_from_pairs(s):
    b = s.shape[0]
    return s.reshape(b, D_HEADS // 2, D_HEAD, 2, D_HEAD).transpose(0, 1, 3, 2, 4).reshape(b, D_HEADS, D_HEAD, D_HEAD)


def _trunk(x, p, *, bsz, t, seq_blk, n_pad, attn_fn, st, drop_rows=0):
    depth = p['norm_mix'].shape[0]
    new = {k: [] for k in ('a_k', 'a_v', 'b_c', 'b_n', 'b_m', 'b_conv', 'c_re', 'c_im', 'd_s', 'd_shift')}
    nt = t // seq_blk
    y = None
    for layer in range(depth):
        g_mix = p['norm_mix'][layer][None]
        if layer % 2 == 0:
            e = layer // 2
            proj = rms_matmul(x, g_mix, p['ev_w_in'][e])
            lam_init = 0.8 - 0.6 * math.exp(-0.3 * layer)
            oa = attn_fn(proj, e, lam_init)
            gcols = proj[:, 7 * HALF:7 * HALF + 2 * B_HEADS]
            gates_r = gcols.reshape(bsz * nt, seq_blk, 2 * B_HEADS).transpose(0, 2, 1)
            hb, bc, bn, bm = mlstm(proj, gates_r, st['b_c'][e], st['b_n'][e][:, :, None, :], st['b_m'][e][:, None, :],
                                   st['b_conv'][e], p['b_conv_w'][e], p['b_conv_b'][e][None],
                                   p['b_ig_bias'][e], p['b_fg_bias'][e], p['b_norm'][e][None],
                                   bsz=bsz, t=t, ln=seq_blk, n_pad=n_pad)
            p3 = proj.reshape(bsz, t, -1)
            new['a_k'].append(p3[:, :, HALF:2 * HALF])
            new['a_v'].append(p3[:, :, 2 * HALF:3 * HALF])
            new['b_c'].append(bc)
            new['b_n'].append(bn[:, :, 0, :])
            new['b_m'].append(bm[:, 0, :])
            new['b_conv'].append(p3[:, t - (B_CONV - 1):, 3 * HALF:5 * HALF])
            x = outproj(x, oa, hb, p['ev_w_out'][e][:HALF], p['ev_w_out'][e][HALF:])
        else:
            o = layer // 2
            proj = rms_matmul(x, g_mix, p['od_w_in'][o])
            sp = p['s5'][o]
            oc, cre, cim = s5(proj, sp['pwr'], sp['pwi'], sp['amr'], sp['ami'], sp['bbr'], sp['bbi'], sp['cre'], sp['cim'],
                              p['c_d'][o][None], p['c_w_glu'][o],
                              st['c_re'][o].reshape(bsz, 1, -1), st['c_im'][o].reshape(bsz, 1, -1),
                              bsz=bsz, t=t, tb_len=seq_blk, n_pad=n_pad)
            od, ds = rwkv(proj, st['d_shift'][o], _pairs_from_heads(st['d_s'][o]),
                          p['d_mu'][o][None], p['d_w0'][o][None], p['d_w_w2'][o], p['d_a0'][o][None],
                          p['d_w_a2'][o], p['d_w_g2'][o], p['d_k_k'][o][None], p['d_k_a'][o][None],
                          p['d_r_k'][o][None], p['d_ln_g'][o][None], p['d_ln_b'][o][None],
                          bsz=bsz, t=t, tb_len=seq_blk, n_pad=n_pad)
            new['c_re'].append(cre.reshape(bsz, C_GROUPS, C_STATE))
            new['c_im'].append(cim.reshape(bsz, C_GROUPS, C_STATE))
            new['d_s'].append(_heads_from_pairs(ds))
            new['d_shift'].append(proj.reshape(bsz, t, -1)[:, -1:, HALF:])
            x = outproj(x, oc, od, p['od_w_out'][o][:HALF], p['od_w_out'][o][HALF:])
        last = layer == depth - 1
        x = ffn(x, p['norm_ffn'][layer][None], p['ffn_w1'][layer], p['ffn_w3'][layer], p['ffn_w2'][layer],
                gf=p['norm_final'][None] if last else None, drop_rows=drop_rows if last else 0)
    return x, new


def kernel(x_prompt, x_sample, cache_a_k, cache_a_v, state_b_c, state_b_n, state_b_m, state_b_conv, state_c_re, state_c_im, state_d_s, state_d_shift, meta_tokens, rel_bias, norm_mix, norm_ffn, norm_final, ev_w_in, ev_w_out, a_lq1, a_lk1, a_lq2, a_lk2, a_subln, b_conv_w, b_conv_b, b_ig_bias, b_fg_bias, b_norm, od_w_in, od_w_out, c_lam_re, c_lam_im, c_log_dt, c_b_re, c_b_im, c_c_re, c_c_im, c_d, c_w_glu, d_mu, d_w0, d_w_w2, d_a0, d_w_a2, d_w_g2, d_k_k, d_k_a, d_r_k, d_ln_g, d_ln_b, ffn_w1, ffn_w3, ffn_w2):
    bp, sp_len, dm = x_prompt.shape
    bs, ts, _ = x_sample.shape
    n_even, n_odd = ev_w_in.shape[0], od_w_in.shape[0]
    assert bp == 1 and sp_len % CHUNK == 0 and ts % RWKV_GRP == 0 and ts >= B_CONV - 1
    dt = x_prompt.dtype

    ns = C_GROUPS * C_STATE
    s5p = []
    for o in range(n_odd):
        bre = jnp.tile(c_b_re[o].transpose(2, 0, 1).reshape(C_GROUP, ns), (C_GROUPS, 1))
        bim = jnp.tile(c_b_im[o].transpose(2, 0, 1).reshape(C_GROUP, ns), (C_GROUPS, 1))
        pwr, pwi, amr, ami, bbr, bbi = s5_params(c_lam_re[o].reshape(1, ns), c_lam_im[o].reshape(1, ns),
                                     jnp.repeat(c_log_dt[o], C_STATE)[None], bre, bim)
        eye = jnp.eye(C_GROUPS, dtype=F32)
        blk = lambda c: (eye[:, None, :, None] * c.transpose(0, 2, 1)[:, :, None, :]).reshape(ns, C_WIDTH).astype(BF16)
        s5p.append(dict(pwr=pwr, pwi=pwi, amr=amr, ami=ami, bbr=bbr, bbi=bbi, cre=blk(c_c_re[o]), cim=blk(c_c_im[o])))
    p = dict(norm_mix=norm_mix, norm_ffn=norm_ffn, norm_final=norm_final,
             ev_w_in=[_pad_cols(ev_w_in[e]).astype(BF16) for e in range(n_even)],
             ev_w_out=ev_w_out.astype(BF16),
             od_w_in=[od_w_in[o].astype(BF16) for o in range(n_odd)], od_w_out=od_w_out.astype(BF16),
             b_conv_w=b_conv_w, b_conv_b=b_conv_b, b_ig_bias=b_ig_bias, b_fg_bias=b_fg_bias, b_norm=b_norm,
             s5=s5p, c_d=c_d, c_w_glu=c_w_glu.astype(BF16),
             d_mu=d_mu, d_w0=d_w0, d_w_w2=d_w_w2.astype(BF16), d_a0=d_a0, d_w_a2=d_w_a2.astype(BF16),
             d_w_g2=d_w_g2.astype(BF16), d_k_k=d_k_k, d_k_a=d_k_a, d_r_k=d_r_k, d_ln_g=d_ln_g, d_ln_b=d_ln_b,
             ffn_w1=ffn_w1.astype(BF16), ffn_w3=ffn_w3.astype(BF16), ffn_w2=ffn_w2.astype(BF16))
    lam_vecs = lambda e: (a_lq1[e][None], a_lk1[e][None], a_lq2[e][None], a_lk2[e][None], a_subln[e][None])

    tp = -(-(sp_len + CHUNK) // ATT_BLK) * ATT_BLK
    n_pad = tp - sp_len - N_META
    xp = jnp.concatenate([jnp.zeros((n_pad, dm), dt), meta_tokens.astype(dt), x_prompt[0]], axis=0)
    bias2 = jnp.stack([bias_tile(rel_bias, ATT_BLK, ATT_BLK, 0, causal=True, scale=LOG2E),
                       bias_tile(rel_bias, ATT_BLK, ATT_BLK, -ATT_BLK, scale=LOG2E)], axis=0)

    def attn_p(proj, e, lam_init):
        return attn_prompt(proj, bias2, rel_bias, *lam_vecs(e), n_pad=n_pad, lam_init=lam_init)

    zeros = lambda *s: jnp.zeros(s, F32)
    st_p = dict(b_c=zeros(n_even, bp, B_HEADS, B_DK, B_DK), b_n=zeros(n_even, bp, B_HEADS, B_DK),
                b_m=zeros(n_even, bp, B_HEADS), b_conv=zeros(n_even, bp, B_CONV - 1, 2 * HALF),
                c_re=zeros(n_odd, bp, C_GROUPS, C_STATE), c_im=zeros(n_odd, bp, C_GROUPS, C_STATE),
                d_s=zeros(n_odd, bp, D_HEADS, D_HEAD, D_HEAD), d_shift=zeros(n_odd, bp, 1, d_mu.shape[1]))
    lead = n_pad + N_META
    drop = lead if lead % ROW_TILE == 0 else 0
    y_p, new_p = _trunk(xp, p, bsz=bp, t=tp, seq_blk=SEQ_BLK, n_pad=n_pad, attn_fn=attn_p, st=st_p, drop_rows=drop)

    tk = cache_a_k.shape[2]
    past_len = tk - N_META
    cid = lambda pos: np.where(pos < N_META, 0, 1 + (pos - N_META) // CHUNK)
    q_pos = N_META + past_len + np.arange(ts)
    k_pos = np.arange(tk + ts)
    assert (cid(k_pos)[None, :] <= cid(q_pos)[:, None]).all()
    bias_past = bias_tile(rel_bias, ts, tk, -(N_META + past_len))
    bias_new = bias_tile(rel_bias, ts, ts, 0)

    def attn_s(proj, e, lam_init):
        return attn_sample(proj, cache_a_k[e].reshape(bs, tk, HALF), cache_a_v[e].reshape(bs, tk, HALF),
                           bias_past, bias_new, *lam_vecs(e), ts=ts, lam_init=lam_init)

    st_s = dict(b_c=state_b_c, b_n=state_b_n, b_m=state_b_m, b_conv=state_b_conv, c_re=state_c_re,
                c_im=state_c_im, d_s=state_d_s, d_shift=state_d_shift)
    y_s, new_s = _trunk(x_sample.reshape(bs * ts, dm), p, bsz=bs, t=ts, seq_blk=ts, n_pad=0, attn_fn=attn_s, st=st_s)

    def pack(new, bsz, t, drop):
        kv = lambda a: a[:, drop:].reshape(bsz, t - drop, A_HEADS, A_DV)
        return (jnp.stack([kv(a) for a in new['a_k']]), jnp.stack([kv(a) for a in new['a_v']]),
                jnp.stack(new['b_c']), jnp.stack(new['b_n']), jnp.stack(new['b_m']), jnp.stack(new['b_conv']),
                jnp.stack(new['c_re']), jnp.stack(new['c_im']), jnp.stack(new['d_s']), jnp.stack(new['d_shift']))

    out_p = pack(new_p, bp, tp, n_pad)
    out_s = pack(new_s, bs, ts, 0)
    return (y_p[lead - drop:][None], y_s.reshape(bs, ts, dm)) + out_p + out_s
```

```python
import functools
import math

import numpy as np
import jax
import jax.numpy as jnp
from jax import lax
from jax.experimental import pallas as pl
from jax.experimental.pallas import tpu as pltpu

F32 = jnp.float32
BF16 = jnp.bfloat16
I32 = jnp.int32

CHUNK = 64
N_META = 16
EPS = 1e-6
A_HEADS = 4
A_DH = 64
A_DV = 128
N_BUCKETS = 32
MAX_DIST = 128
B_HEADS = 4
B_DK = 128
B_CONV = 4
C_GROUP = 16
C_GROUPS = 32
C_STATE = 64
C_WIDTH = C_GROUP * C_GROUPS
D_HEAD = 64
D_HEADS = 8
D_WIDTH = D_HEAD * D_HEADS
D_GN_EPS = 64e-5
HALF = 512

NEG = -1e30
LOG2E = math.log2(math.e)
LANES = 128
VMEM_LIMIT = 56 * 1024 * 1024

ROW_TILE = 512
ATT_BLK = 512
ATT_STRIP = 64
SEQ_BLK = 128
S5_GRP = 8
S5_SHIFTS = (1, 2, 4)
RWKV_GRP = 32
FFN_TH = 1408


def _cparams(sem):
    return pltpu.CompilerParams(dimension_semantics=sem, vmem_limit_bytes=VMEM_LIMIT)


def _dot(a, b):
    return jnp.dot(a, b, preferred_element_type=F32)


def _dot_nt(a, b):
    return lax.dot_general(a, b, (((1,), (1,)), ((), ())), preferred_element_type=F32)


def _dot_tn(a, b):
    return lax.dot_general(a, b, (((0,), (0,)), ((), ())), preferred_element_type=F32)


def _sigmoid(x):
    return 1.0 / (1.0 + jnp.exp(-x))


def _softplus(x):
    return jnp.maximum(x, 0.0) + jnp.log1p(jnp.exp(-jnp.abs(x)))


def _rms(x, g):
    return x * lax.rsqrt(jnp.mean(x * x, axis=-1, keepdims=True) + EPS) * g


def _segsum(x, ones):
    n, w = x.shape[0], ones.shape[0]
    hi = x.astype(BF16)
    lo = (x - hi.astype(F32)).astype(BF16)
    outs = []
    for c in range(x.shape[1] // w):
        res = _dot(jnp.concatenate([hi[:, c * w:(c + 1) * w], lo[:, c * w:(c + 1) * w]], axis=0), ones)
        outs.append(res[:n] + res[n:])
    return jnp.concatenate(outs, axis=1)


def _block_ones(n, seg):
    r = lax.broadcasted_iota(I32, (n, n), 0) // seg
    c = lax.broadcasted_iota(I32, (n, n), 1) // seg
    return jnp.where(r == c, 1.0, 0.0).astype(BF16)


def _rms_mm_kernel(x_ref, g_ref, w_ref, o_ref):
    h = _rms(x_ref[...], g_ref[...])
    o_ref[...] = _dot(h.astype(BF16), w_ref[...])


def rms_matmul(x, g, w, tm=ROW_TILE):
    m, d = x.shape
    tm = min(tm, m)
    n = w.shape[1]
    return pl.pallas_call(
        _rms_mm_kernel,
        grid=(m // tm,),
        in_specs=[pl.BlockSpec((tm, d), lambda i: (i, 0)),
                  pl.BlockSpec((1, d), lambda i: (0, 0)),
                  pl.BlockSpec((d, n), lambda i: (0, 0))],
        out_specs=pl.BlockSpec((tm, n), lambda i: (i, 0)),
        out_shape=jax.ShapeDtypeStruct((m, n), F32),
        compiler_params=_cparams(("parallel",)),
        name="rms_matmul",
    )(x, g, w)


def _outproj_kernel(x_ref, a_ref, b_ref, wa_ref, wb_ref, o_ref):
    acc = _dot(a_ref[...].astype(BF16), wa_ref[...]) + _dot(b_ref[...].astype(BF16), wb_ref[...])
    o_ref[...] = x_ref[...] + acc


def outproj(x, a, b, wa, wb, tm=ROW_TILE):
    m, d = x.shape
    tm = min(tm, m)
    k = a.shape[1]
    row = lambda i: (i, 0)
    fix = lambda i: (0, 0)
    return pl.pallas_call(
        _outproj_kernel,
        grid=(m // tm,),
        in_specs=[pl.BlockSpec((tm, d), row), pl.BlockSpec((tm, k), row), pl.BlockSpec((tm, k), row),
                  pl.BlockSpec((k, d), fix), pl.BlockSpec((k, d), fix)],
        out_specs=pl.BlockSpec((tm, d), row),
        out_shape=jax.ShapeDtypeStruct((m, d), F32),
        compiler_params=_cparams(("parallel",)),
        name="outproj",
    )(x, a, b, wa, wb)


def _ffn_kernel(x_ref, g_ref, gf_ref, w1_ref, w3_ref, w2_ref, o_ref, h_scr, *, final_norm):
    j = pl.program_id(1)

    @pl.when(j == 0)
    def _():
        x = x_ref[...]
        h_scr[...] = _rms(x, g_ref[...]).astype(BF16)
        o_ref[...] = x

    h = h_scr[...]
    a = _dot(h, w1_ref[...])
    b = _dot(h, w3_ref[...])
    u = (a * _sigmoid(a)) * b
    o_ref[...] += _dot(u.astype(BF16), w2_ref[...])

    if final_norm:
        @pl.when(j == pl.num_programs(1) - 1)
        def _():
            o_ref[...] = _rms(o_ref[...], gf_ref[...])


def ffn(x, g, w1, w3, w2, gf=None, drop_rows=0, tm=ROW_TILE, th=FFN_TH):
    m, d = x.shape
    tm = min(tm, m)
    hid = w1.shape[1]
    final_norm = gf is not None
    if gf is None:
        gf = g
    skip, rem = divmod(drop_rows, tm)
    assert rem == 0
    return pl.pallas_call(
        functools.partial(_ffn_kernel, final_norm=final_norm),
        grid=(m // tm, hid // th),
        in_specs=[pl.BlockSpec((tm, d), lambda i, j: (i, 0)),
                  pl.BlockSpec((1, d), lambda i, j: (0, 0)),
                  pl.BlockSpec((1, d), lambda i, j: (0, 0)),
                  pl.BlockSpec((d, th), lambda i, j: (0, j)),
                  pl.BlockSpec((d, th), lambda i, j: (0, j)),
                  pl.BlockSpec((th, d), lambda i, j: (j, 0))],
        out_specs=pl.BlockSpec((tm, d), lambda i, j: (jnp.maximum(i - skip, 0), 0)),
        out_shape=jax.ShapeDtypeStruct((m - drop_rows, d), F32),
        scratch_shapes=[pltpu.VMEM((tm, d), BF16)],
        compiler_params=_cparams(("arbitrary" if skip else "parallel", "arbitrary")),
        name="ffn",
    )(x, g, gf, w1, w3, w2)


def _bias_kernel(rb_ref, o_ref, *, rel0, causal, scale):
    nq, nk = o_ref.shape[1], o_ref.shape[2]
    a = lax.broadcasted_iota(I32, (nq, nk), 0)
    b = lax.broadcasted_iota(I32, (nq, nk), 1)
    rel = b - a + rel0
    nb = N_BUCKETS // 2
    max_exact = nb // 2
    ret = jnp.where(rel > 0, nb, 0)
    n = jnp.abs(rel)
    nf = jnp.maximum(n, 1).astype(F32)
    large = max_exact + (jnp.log(nf / max_exact) / math.log(MAX_DIST / max_exact) * (nb - max_exact)).astype(I32)
    large = jnp.minimum(large, nb - 1)
    bucket = ret + jnp.where(n < max_exact, n, large)
    for h in range(A_HEADS):
        acc = jnp.zeros((nq, nk), F32)
        for bk in range(N_BUCKETS):
            acc = jnp.where(bucket == bk, rb_ref[bk, h] * scale, acc)
        if causal:
            acc = jnp.where((b // CHUNK) <= (a // CHUNK), acc, NEG)
        o_ref[h] = acc


def bias_tile(rel_bias, nq, nk, rel0, causal=False, scale=1.0):
    return pl.pallas_call(
        functools.partial(_bias_kernel, rel0=rel0, causal=causal, scale=scale),
        in_specs=[pl.BlockSpec(memory_space=pltpu.SMEM)],
        out_specs=pl.BlockSpec(memory_space=pltpu.VMEM),
        out_shape=jax.ShapeDtypeStruct((A_HEADS, nq, nk), F32),
        compiler_params=pltpu.CompilerParams(vmem_limit_bytes=VMEM_LIMIT),
        name="bias_tile",
    )(rel_bias)


def _lambda(lq1_ref, lk1_ref, lq2_ref, lk2_ref, lam_init):
    s1 = jnp.sum(lq1_ref[...] * lk1_ref[...], axis=-1, keepdims=True)
    s2 = jnp.sum(lq2_ref[...] * lk2_ref[...], axis=-1, keepdims=True)
    return jnp.exp(s1) - jnp.exp(s2) + lam_init


def _attn_finish(acc0, l0, acc1, l1, lam, g, lam_init):
    o = acc0 / l0 - lam * (acc1 / l1)
    return _rms(o, g) * (1.0 - lam_init)


def _attn_prompt_kernel(qi_ref, kj_ref, rb_ref, q_ref, k_ref, v_ref, bias_ref,
                        lq1_ref, lk1_ref, lq2_ref, lk2_ref, g_ref, o_ref,
                        m_scr, l_scr, acc_scr, s_scr, p_scr, al_scr, *, blk, n_pad, lam_init):
    s = pl.program_id(0)
    i = qi_ref[s]
    j = kj_ref[s]
    d = i - j
    nct = blk // LANES
    nstrip = blk // ATT_STRIP

    @pl.when(j == 0)
    def _():
        m_scr[...] = jnp.full(m_scr.shape, NEG, F32)
        l_scr[...] = jnp.zeros(l_scr.shape, F32)
        acc_scr[...] = jnp.zeros(acc_scr.shape, F32)

    def update(general):
        for h in range(A_HEADS):
            far = rb_ref[N_BUCKETS // 2 - 1, h] * LOG2E
            for mm in range(2):
                c0 = h * 2 * A_DH + mm * A_DH
                qb = (q_ref[:, c0:c0 + A_DH] * (A_DH ** -0.5 * LOG2E)).astype(BF16)
                s_scr[2 * h + mm] = _dot_nt(qb, k_ref[:, c0:c0 + A_DH].astype(BF16))
            for mm in range(2):
                idx = 2 * h + mm
                for r in range(nstrip):
                    rows = slice(r * ATT_STRIP, (r + 1) * ATT_STRIP)
                    tiles = []
                    for c in range(nct):
                        cols = slice(c * LANES, (c + 1) * LANES)
                        t = s_scr[idx, rows, cols]
                        if general:
                            kpos = j * blk + c * LANES + lax.broadcasted_iota(I32, (1, LANES), 1)
                            near = bias_ref[jnp.minimum(d, 1), h, rows, cols]
                            t = t + jnp.where(d < 2, near, far) + jnp.where(kpos < n_pad, NEG, 0.0)
                        tiles.append(t)
                    mx = functools.reduce(jnp.maximum, tiles)
                    m_cur = jnp.broadcast_to(jnp.max(mx, axis=-1, keepdims=True), (ATT_STRIP, LANES))
                    if not general:
                        m_cur = m_cur + far
                    m_old = m_scr[idx, rows, :]
                    m_new = jnp.maximum(m_old, m_cur)
                    alpha = jnp.exp2(m_old - m_new)
                    m_sub = m_new if general else m_new - far
                    ps = [jnp.exp2(t - m_sub) for t in tiles]
                    l_scr[idx, rows, :] = alpha * l_scr[idx, rows, :] + functools.reduce(jnp.add, ps)
                    m_scr[idx, rows, :] = m_new
                    al_scr[idx, rows, :] = alpha
                    for c in range(nct):
                        p_scr[h, mm * blk + r * ATT_STRIP:mm * blk + (r + 1) * ATT_STRIP,
                              c * LANES:(c + 1) * LANES] = ps[c].astype(BF16)
            pv = _dot(p_scr[h], v_ref[:, h * A_DV:(h + 1) * A_DV].astype(BF16))
            for mm in range(2):
                idx = 2 * h + mm
                acc_scr[idx] = al_scr[idx] * acc_scr[idx] + pv[mm * blk:(mm + 1) * blk]

    is_far = jnp.logical_and(d >= 2, j > 0)

    @pl.when(is_far)
    def _():
        update(False)

    @pl.when(jnp.logical_not(is_far))
    def _():
        update(True)

    @pl.when(d == 0)
    def _():
        lam = _lambda(lq1_ref, lk1_ref, lq2_ref, lk2_ref, lam_init)
        rows = i * blk + lax.broadcasted_iota(I32, (blk, 1), 0)
        valid = rows >= n_pad
        for h in range(A_HEADS):
            l0 = jnp.sum(l_scr[2 * h], axis=-1, keepdims=True)
            l1 = jnp.sum(l_scr[2 * h + 1], axis=-1, keepdims=True)
            y = _attn_finish(acc_scr[2 * h], l0, acc_scr[2 * h + 1], l1, lam, g_ref[...], lam_init)
            o_ref[:, h * A_DV:(h + 1) * A_DV] = jnp.where(valid, y, 0.0)


def attn_prompt(proj, bias2, rel_bias, lq1, lk1, lq2, lk2, g, *, n_pad, lam_init, blk=ATT_BLK):
    t = proj.shape[0]
    nb = t // blk
    qi = np.array([i for i in range(nb) for _ in range(i + 1)], np.int32)
    kj = np.array([j for i in range(nb) for j in range(i + 1)], np.int32)
    vec = lambda n: pl.BlockSpec((1, n), lambda s, qi, kj: (0, 0))
    grid_spec = pltpu.PrefetchScalarGridSpec(
        num_scalar_prefetch=2,
        grid=(len(qi),),
        in_specs=[pl.BlockSpec(memory_space=pltpu.SMEM),
                  pl.BlockSpec((blk, HALF), lambda s, qi, kj: (qi[s], 0)),
                  pl.BlockSpec((blk, HALF), lambda s, qi, kj: (kj[s], 1)),
                  pl.BlockSpec((blk, HALF), lambda s, qi, kj: (kj[s], 2)),
                  pl.BlockSpec((2, A_HEADS, blk, blk), lambda s, qi, kj: (0, 0, 0, 0)),
                  vec(A_DH), vec(A_DH), vec(A_DH), vec(A_DH), vec(A_DV)],
        out_specs=pl.BlockSpec((blk, HALF), lambda s, qi, kj: (qi[s], 0)),
        scratch_shapes=[pltpu.VMEM((2 * A_HEADS, blk, LANES), F32),
                        pltpu.VMEM((2 * A_HEADS, blk, LANES), F32),
                        pltpu.VMEM((2 * A_HEADS, blk, A_DV), F32),
                        pltpu.VMEM((2 * A_HEADS, blk, blk), F32),
                        pltpu.VMEM((A_HEADS, 2 * blk, blk), BF16),
                        pltpu.VMEM((2 * A_HEADS, blk, LANES), F32)],
    )
    assert n_pad <= blk and blk >= MAX_DIST
    return pl.pallas_call(
        functools.partial(_attn_prompt_kernel, blk=blk, n_pad=n_pad, lam_init=lam_init),
        grid_spec=grid_spec,
        out_shape=jax.ShapeDtypeStruct((t, HALF), F32),
        compiler_params=_cparams(("arbitrary",)),
        name="attn_prompt",
    )(jnp.asarray(qi), jnp.asarray(kj), rel_bias, proj, proj, proj, bias2, lq1, lk1, lq2, lk2, g)


def _attn_sample_kernel(q_ref, kn_ref, vn_ref, kc_ref, vc_ref, bp_ref, bn_ref,
                        lq1_ref, lk1_ref, lq2_ref, lk2_ref, g_ref, o_ref, *, lam_init):
    lam = _lambda(lq1_ref, lk1_ref, lq2_ref, lk2_ref, lam_init)
    for h in range(A_HEADS):
        vsl = slice(h * A_DV, (h + 1) * A_DV)
        vp = vc_ref[0, :, vsl].astype(BF16)
        vn = vn_ref[:, vsl].astype(BF16)
        outs = []
        for mm in range(2):
            c0 = h * 2 * A_DH + mm * A_DH
            qb = (q_ref[:, c0:c0 + A_DH] * (A_DH ** -0.5)).astype(BF16)
            sp = _dot_nt(qb, kc_ref[0, :, c0:c0 + A_DH].astype(BF16)) + bp_ref[h]
            sn = _dot_nt(qb, kn_ref[:, c0:c0 + A_DH].astype(BF16)) + bn_ref[h]
            mx = jnp.maximum(jnp.max(sp, axis=-1, keepdims=True), jnp.max(sn, axis=-1, keepdims=True))
            pp = jnp.exp(sp - mx)
            pn = jnp.exp(sn - mx)
            l = jnp.sum(pp, axis=-1, keepdims=True) + jnp.sum(pn, axis=-1, keepdims=True)
            acc = _dot(pp.astype(BF16), vp) + _dot(pn.astype(BF16), vn)
            outs.append((acc, l))
        o_ref[:, vsl] = _attn_finish(outs[0][0], outs[0][1], outs[1][0], outs[1][1], lam, g_ref[...], lam_init)


def attn_sample(proj, kc, vc, bias_past, bias_new, lq1, lk1, lq2, lk2, g, *, ts, lam_init):
    bsz, tk, _ = kc.shape
    vec = lambda n: pl.BlockSpec((1, n), lambda b: (0, 0))
    return pl.pallas_call(
        functools.partial(_attn_sample_kernel, lam_init=lam_init),
        grid=(bsz,),
        in_specs=[pl.BlockSpec((ts, HALF), lambda b: (b, 0)),
                  pl.BlockSpec((ts, HALF), lambda b: (b, 1)),
                  pl.BlockSpec((ts, HALF), lambda b: (b, 2)),
                  pl.BlockSpec((1, tk, HALF), lambda b: (b, 0, 0)),
                  pl.BlockSpec((1, tk, HALF), lambda b: (b, 0, 0)),
                  pl.BlockSpec((A_HEADS, ts, tk), lambda b: (0, 0, 0)),
                  pl.BlockSpec((A_HEADS, ts, ts), lambda b: (0, 0, 0)),
                  vec(A_DH), vec(A_DH), vec(A_DH), vec(A_DH), vec(A_DV)],
        out_specs=pl.BlockSpec((ts, HALF), lambda b: (b, 0)),
        out_shape=jax.ShapeDtypeStruct((bsz * ts, HALF), F32),
        compiler_params=_cparams(("parallel",)),
        name="attn_sample",
    )(proj, proj, proj, kc, vc, bias_past, bias_new, lq1, lk1, lq2, lk2, g)


def _mlstm_kernel(bq_ref, bk_ref, bv_ref, bo_ref, gc_ref, gr_ref, c0_ref, n0_ref, m0_ref, conv0_ref,
                  cw_ref, cb_ref, gbc_ref, gbr_ref, bn_ref,
                  h_ref, c_out, n_out, m_out,
                  xbuf, c_scr, n_scr, m_scr, *, ln, n_pad):
    tb = pl.program_id(1)
    nt = pl.num_programs(1)
    halo = B_CONV - 1
    base = 8 - halo

    @pl.when(tb == 0)
    def _():
        xbuf[base:8, :] = conv0_ref[0]
        c_scr[...] = c0_ref[0]
        n_scr[...] = n0_ref[0]
        for h in range(B_HEADS):
            m_scr[h] = m0_ref[0, :, h:h + 1]

    xbuf[8:8 + ln, 0:HALF] = bq_ref[...]
    xbuf[8:8 + ln, HALF:2 * HALF] = bk_ref[...]
    conv = cb_ref[...] + cw_ref[0:1, :] * xbuf[base:base + ln, :]
    for jj in range(1, B_CONV):
        conv = conv + cw_ref[jj:jj + 1, :] * xbuf[base + jj:base + jj + ln, :]
    xbuf[base:8, :] = xbuf[8 + ln - halo:8 + ln, :]
    conv = conv * _sigmoid(conv)
    q_all = conv[:, 0:HALF] * (B_DK ** -0.5)
    k_all = conv[:, HALF:2 * HALF]

    rows = tb * ln + lax.broadcasted_iota(I32, (ln, 1), 0)
    cols = tb * ln + lax.broadcasted_iota(I32, (1, ln), 1)
    valid_c = rows >= n_pad
    valid_r = cols >= n_pad

    gc = gc_ref[...] + gbc_ref[...]
    gr = gr_ref[0] + gbr_ref[...]
    li_c = jnp.where(valid_c, gc, NEG)
    li_r = jnp.where(valid_r, gr, NEG)
    lf_c = jnp.where(valid_c, -_softplus(-gc), 0.0)
    lf_r = jnp.where(valid_r, -_softplus(-gr), 0.0)

    ri = lax.broadcasted_iota(I32, (ln, ln), 0)
    ci = lax.broadcasted_iota(I32, (ln, ln), 1)
    tril = ri >= ci
    tril_f = jnp.where(tril, 1.0, 0.0)
    triu_f = jnp.where(ri <= ci, 1.0, 0.0)
    b_c = jnp.dot(tril_f, lf_c, preferred_element_type=F32, precision=lax.Precision.HIGHEST)
    b_r = jnp.dot(lf_r, triu_f, preferred_element_type=F32, precision=lax.Precision.HIGHEST)

    for h in range(B_HEADS):
        sl = slice(h * B_DK, (h + 1) * B_DK)
        qh = q_all[:, sl].astype(BF16)
        kh = k_all[:, sl]
        vh = bv_ref[:, sl].astype(BF16)
        c = c_scr[h]
        n = n_scr[h]
        m = m_scr[h]
        bc = b_c[:, B_HEADS + h:B_HEADS + h + 1]
        br = b_r[B_HEADS + h:B_HEADS + h + 1, :]
        inter = bc + m
        dmat = jnp.where(tril, bc - br + li_r[h:h + 1, :], NEG)
        mt = jnp.maximum(inter, jnp.max(dmat, axis=-1, keepdims=True))
        w_inter = jnp.exp(inter - mt)
        s = _dot_nt(qh, kh.astype(BF16)) * jnp.exp(dmat - mt)
        num = w_inter * _dot(qh, c.astype(BF16)) + _dot(s.astype(BF16), vh)
        qn = jnp.sum(qh.astype(F32) * n, axis=-1, keepdims=True)
        den = w_inter * qn + jnp.sum(s, axis=-1, keepdims=True)
        hh = num / jnp.maximum(jnp.abs(den), jnp.exp(-mt))
        b_last = bc[ln - 1:ln, :]
        g = b_last - bc + li_c[:, h:h + 1]
        m_new = jnp.maximum(b_last + m, jnp.max(g, axis=0, keepdims=True))
        decay = jnp.exp(b_last + m - m_new)
        wk = (jnp.exp(g - m_new) * kh)
        c_scr[h] = decay * c + _dot_tn(wk.astype(BF16), vh)
        n_scr[h] = decay * n + jnp.sum(wk, axis=0, keepdims=True)
        m_scr[h] = m_new
        hn = _rms(hh, bn_ref[:, sl]) * _sigmoid(bo_ref[:, sl])
        h_ref[:, sl] = jnp.where(valid_c, hn, 0.0)

    @pl.when(tb == nt - 1)
    def _():
        c_out[0] = c_scr[...]
        n_out[0] = n_scr[...]
        lane = lax.broadcasted_iota(I32, (1, B_HEADS), 1)
        mrow = jnp.zeros((1, B_HEADS), F32)
        for h in range(B_HEADS):
            mrow = jnp.where(lane == h, m_scr[h], mrow)
        m_out[0] = mrow


def mlstm(proj, gates_r, c0, n0, m0, conv0, cw, cb, igb, fgb, bnorm, *, bsz, t, ln, n_pad):
    nt = t // ln
    gb = jnp.concatenate([igb, fgb])
    gw = proj.shape[1] // LANES - 1
    row = lambda c: (lambda b, i: (b * nt + i, c))
    fix2 = lambda b, i: (0, 0)
    out_shapes = (jax.ShapeDtypeStruct((bsz * t, HALF), F32),
                  jax.ShapeDtypeStruct((bsz, B_HEADS, B_DK, B_DK), F32),
                  jax.ShapeDtypeStruct((bsz, B_HEADS, 1, B_DK), F32),
                  jax.ShapeDtypeStruct((bsz, 1, B_HEADS), F32))
    return pl.pallas_call(
        functools.partial(_mlstm_kernel, ln=ln, n_pad=n_pad),
        grid=(bsz, nt),
        in_specs=[pl.BlockSpec((ln, HALF), row(3)), pl.BlockSpec((ln, HALF), row(4)),
                  pl.BlockSpec((ln, HALF), row(5)), pl.BlockSpec((ln, HALF), row(6)),
                  pl.BlockSpec((ln, LANES), row(gw)),
                  pl.BlockSpec((1, 2 * B_HEADS, ln), lambda b, i: (b * nt + i, 0, 0)),
                  pl.BlockSpec((1, B_HEADS, B_DK, B_DK), lambda b, i: (b, 0, 0, 0)),
                  pl.BlockSpec((1, B_HEADS, 1, B_DK), lambda b, i: (b, 0, 0, 0)),
                  pl.BlockSpec((1, 1, B_HEADS), lambda b, i: (b, 0, 0)),
                  pl.BlockSpec((1, B_CONV - 1, 2 * HALF), lambda b, i: (b, 0, 0)),
                  pl.BlockSpec((B_CONV, 2 * HALF), fix2), pl.BlockSpec((1, 2 * HALF), fix2),
                  pl.BlockSpec((1, LANES), fix2), pl.BlockSpec((2 * B_HEADS, 1), fix2),
                  pl.BlockSpec((1, HALF), fix2)],
        out_specs=(pl.BlockSpec((ln, HALF), lambda b, i: (b * nt + i, 0)),
                   pl.BlockSpec((1, B_HEADS, B_DK, B_DK), lambda b, i: (b, 0, 0, 0)),
                   pl.BlockSpec((1, B_HEADS, 1, B_DK), lambda b, i: (b, 0, 0, 0)),
                   pl.BlockSpec((1, 1, B_HEADS), lambda b, i: (b, 0, 0))),
        out_shape=out_shapes,
        scratch_shapes=[pltpu.VMEM((ln + 8, 2 * HALF), F32),
                        pltpu.VMEM((B_HEADS, B_DK, B_DK), F32),
                        pltpu.VMEM((B_HEADS, 1, B_DK), F32),
                        pltpu.VMEM((B_HEADS, 1, 1), F32)],
        compiler_params=_cparams(("parallel", "arbitrary")),
        name="mlstm",
    )(proj, proj, proj, proj, proj, gates_r, c0, n0, m0, conv0, cw, cb,
      jnp.pad(gb, (0, LANES - 2 * B_HEADS))[None], gb[:, None], bnorm)


def _s5_param_kernel(lr_ref, li_ref, ldt_ref, bre_ref, bim_ref, pwr_ref, pwi_ref, amr_ref, ami_ref, bbr_ref, bbi_ref):
    lr = lr_ref[...]
    li = li_ref[...]
    dt = jnp.exp(ldt_ref[...])
    mag = jnp.exp(lr * dt)
    ar = mag * jnp.cos(li * dt)
    ai = mag * jnp.sin(li * dt)
    den = lr * lr + li * li
    cr = ((ar - 1.0) * lr + ai * li) / den
    ci = (ai * lr - (ar - 1.0) * li) / den
    sub = lax.broadcasted_iota(I32, pwr_ref.shape, 0)
    pr, pi = ar, ai
    pwr = jnp.zeros(pwr_ref.shape, F32)
    pwi = jnp.zeros(pwr_ref.shape, F32)
    for t in range(S5_GRP):
        pwr = jnp.where(sub == t, pr, pwr)
        pwi = jnp.where(sub == t, pi, pwi)
        if t + 1 in S5_SHIFTS:
            k = S5_SHIFTS.index(t + 1)
            amr_ref[k] = jnp.where(sub >= t + 1, pr, 0.0)
            ami_ref[k] = jnp.where(sub >= t + 1, pi, 0.0)
        pr, pi = pr * ar - pi * ai, pr * ai + pi * ar
    pwr_ref[...] = pwr
    pwi_ref[...] = pwi
    nr, nc = bre_ref.shape
    same = (lax.broadcasted_iota(I32, (nr, nc), 0) // C_GROUP) == (lax.broadcasted_iota(I32, (nr, nc), 1) // C_STATE)
    br = bre_ref[...]
    bi = bim_ref[...]
    bbr_ref[...] = jnp.where(same, cr * br - ci * bi, 0.0).astype(BF16)
    bbi_ref[...] = jnp.where(same, cr * bi + ci * br, 0.0).astype(BF16)


def s5_params(lr, li, ldt, bre_rep, bim_rep):
    ns = lr.shape[1]
    vm = pl.BlockSpec(memory_space=pltpu.VMEM)
    return pl.pallas_call(
        _s5_param_kernel,
        in_specs=[vm] * 5,
        out_specs=(vm,) * 6,
        out_shape=(jax.ShapeDtypeStruct((S5_GRP, ns), F32), jax.ShapeDtypeStruct((S5_GRP, ns), F32),
                   jax.ShapeDtypeStruct((len(S5_SHIFTS), S5_GRP, ns), F32),
                   jax.ShapeDtypeStruct((len(S5_SHIFTS), S5_GRP, ns), F32),
                   jax.ShapeDtypeStruct(bre_rep.shape, BF16), jax.ShapeDtypeStruct(bre_rep.shape, BF16)),
        compiler_params=pltpu.CompilerParams(vmem_limit_bytes=VMEM_LIMIT),
        name="s5_params",
    )(lr, li, ldt, bre_rep, bim_rep)


def _s5_kernel(u_ref, pwr_ref, pwi_ref, amr_ref, ami_ref, bbr_ref, bbi_ref, cre_ref, cim_ref, d_ref, wg_ref, x0r_ref, x0i_ref,
               o_ref, xr_out, xi_out, sr_scr, si_scr, bur, bui, xra, xia, *, tb_len, n_pad):
    tb = pl.program_id(1)
    nt = pl.num_programs(1)

    @pl.when(tb == 0)
    def _():
        sr_scr[...] = x0r_ref[0]
        si_scr[...] = x0i_ref[0]

    u = u_ref[...]
    ub = u.astype(BF16)
    bur[...] = _dot(ub, bbr_ref[...])
    bui[...] = _dot(ub, bbi_ref[...])
    def body(grp, carry):
        xr_prev, xi_prev = carry
        base = pl.multiple_of(grp * S5_GRP, S5_GRP)
        xr = bur[pl.ds(base, S5_GRP), :]
        xi = bui[pl.ds(base, S5_GRP), :]
        for k, d in enumerate(S5_SHIFTS):
            sr = pltpu.roll(xr, d, 0)
            si = pltpu.roll(xi, d, 0)
            ar, ai = amr_ref[k], ami_ref[k]
            xr, xi = xr + ar * sr - ai * si, xi + ar * si + ai * sr
        pr, pi = pwr_ref[...], pwi_ref[...]
        xr, xi = xr + pr * xr_prev - pi * xi_prev, xi + pr * xi_prev + pi * xr_prev
        xra[pl.ds(base, S5_GRP), :] = xr
        xia[pl.ds(base, S5_GRP), :] = xi
        return xr[S5_GRP - 1:S5_GRP], xi[S5_GRP - 1:S5_GRP]

    xr, xi = lax.fori_loop(0, tb_len // S5_GRP, body, (sr_scr[...], si_scr[...]))
    sr_scr[...] = xr
    si_scr[...] = xi

    y = _dot(xra[...].astype(BF16), cre_ref[...]) - _dot(xia[...].astype(BF16), cim_ref[...]) + d_ref[...] * u
    yg = 0.5 * y * (1.0 + jnp.tanh(math.sqrt(2.0 / math.pi) * (y + 0.044715 * (y * y * y))))
    oc = yg * _sigmoid(_dot(yg.astype(BF16), wg_ref[...]))
    rows = tb * tb_len + lax.broadcasted_iota(I32, (tb_len, 1), 0)
    o_ref[...] = jnp.where(rows >= n_pad, oc, 0.0)

    @pl.when(tb == nt - 1)
    def _():
        xr_out[0] = xr
        xi_out[0] = xi


def s5(proj, pwr, pwi, amr, ami, bbr, bbi, cre, cim, dskip, wglu, x0r, x0i, *, bsz, t, tb_len, n_pad):
    nt = t // tb_len
    ns = pwr.shape[1]
    fix2 = lambda b, i: (0, 0)
    st = pl.BlockSpec((1, 1, ns), lambda b, i: (b, 0, 0))
    return pl.pallas_call(
        functools.partial(_s5_kernel, tb_len=tb_len, n_pad=n_pad),
        grid=(bsz, nt),
        in_specs=[pl.BlockSpec((tb_len, HALF), lambda b, i: (b * nt + i, 0)),
                  pl.BlockSpec((S5_GRP, ns), fix2), pl.BlockSpec((S5_GRP, ns), fix2),
                  pl.BlockSpec(amr.shape, lambda b, i: (0, 0, 0)), pl.BlockSpec(amr.shape, lambda b, i: (0, 0, 0)),
                  pl.BlockSpec((HALF, ns), fix2), pl.BlockSpec((HALF, ns), fix2),
                  pl.BlockSpec((ns, HALF), fix2), pl.BlockSpec((ns, HALF), fix2),
                  pl.BlockSpec((1, HALF), fix2), pl.BlockSpec((HALF, HALF), fix2), st, st],
        out_specs=(pl.BlockSpec((tb_len, HALF), lambda b, i: (b * nt + i, 0)), st, st),
        out_shape=(jax.ShapeDtypeStruct((bsz * t, HALF), F32),
                   jax.ShapeDtypeStruct((bsz, 1, ns), F32), jax.ShapeDtypeStruct((bsz, 1, ns), F32)),
        scratch_shapes=[pltpu.VMEM((1, ns), F32), pltpu.VMEM((1, ns), F32),
                        pltpu.VMEM((tb_len, ns), F32), pltpu.VMEM((tb_len, ns), F32),
                        pltpu.VMEM((tb_len, ns), F32), pltpu.VMEM((tb_len, ns), F32)],
        compiler_params=_cparams(("parallel", "arbitrary")),
        name="s5",
    )(proj, pwr, pwi, amr, ami, bbr, bbi, cre, cim, dskip, wglu, x0r, x0i)


def _rwkv_kernel(r_ref, k_ref, v_ref, lo_ref, sh0_ref, s0_ref,
                 mu_ref, w0_ref, ww2_ref, a0_ref, wa2_ref, wg2_ref, kk_ref, ka_ref, rk_ref, lng_ref, lnb_ref,
                 o_ref, s_out,
                 xbuf, s_scr, w_scr, nkk_scr, b_scr, k_scr, r_scr, v_scr, c_scr, be_scr, ga_scr, y_scr,
                 *, tb_len, n_pad):
    tb = pl.program_id(1)
    nt = pl.num_programs(1)
    npair = D_HEADS // 2
    ncol = xbuf.shape[1]

    @pl.when(tb == 0)
    def _():
        xbuf[7:8, :] = sh0_ref[0]
        s_scr[...] = s0_ref[0]

    xbuf[8:8 + tb_len, 0:D_WIDTH] = r_ref[...]
    xbuf[8:8 + tb_len, D_WIDTH:2 * D_WIDTH] = k_ref[...]
    xbuf[8:8 + tb_len, 2 * D_WIDTH:3 * D_WIDTH] = v_ref[...]
    xbuf[8:8 + tb_len, 3 * D_WIDTH:ncol] = lo_ref[...]
    cur = xbuf[8:8 + tb_len, :]
    prev = xbuf[7:7 + tb_len, :]
    xbuf[7:8, :] = xbuf[7 + tb_len:8 + tb_len, :]
    xm = cur + mu_ref[...] * (prev - cur)
    r = xm[:, 0:D_WIDTH]
    k = xm[:, D_WIDTH:2 * D_WIDTH]
    v = xm[:, 2 * D_WIDTH:3 * D_WIDTH]
    c0 = 3 * D_WIDTH
    wlo = xm[:, c0:c0 + 64]
    alo = xm[:, c0 + 64:c0 + 128]
    glo = xm[:, c0 + 128:c0 + 256]

    w_raw = w0_ref[...] + _dot(jnp.tanh(wlo).astype(BF16), ww2_ref[...])
    decay = jnp.exp(-jnp.exp(-_softplus(-w_raw) - 0.5))
    a = _sigmoid(a0_ref[...] + _dot(alo.astype(BF16), wa2_ref[...]))
    g = _dot(_sigmoid(glo).astype(BF16), wg2_ref[...])

    ones_h = _block_ones(2 * LANES, D_HEAD)
    kk = k * kk_ref[...]
    kk = kk / jnp.maximum(jnp.sqrt(_segsum(kk * kk, ones_h)), 1e-12)
    k2 = k * (1.0 + (a - 1.0) * ka_ref[...])

    nkk = -kk
    bb = kk * a
    w_scr[...] = decay
    nkk_scr[...] = nkk
    b_scr[...] = bb
    k_scr[...] = k2
    r_scr[...] = r
    v_scr[...] = v
    c_scr[...] = pltpu.roll(decay, 1, 0) * nkk
    be_scr[...] = _segsum(pltpu.roll(bb, 1, 0) * nkk, ones_h)
    ga_scr[...] = _segsum(pltpu.roll(k2, 1, 0) * nkk, ones_h)

    eye2 = jnp.where((lax.broadcasted_iota(I32, (D_HEAD, LANES), 1) % D_HEAD)
                     == lax.broadcasted_iota(I32, (D_HEAD, LANES), 0), 1.0, 0.0)

    sub = lax.broadcasted_iota(I32, (RWKV_GRP, LANES), 0)
    pairs = range(npair)
    nrow = npair * D_HEAD
    ri = lax.broadcasted_iota(I32, (2 * LANES, 2 * LANES), 0)
    ci = lax.broadcasted_iota(I32, (2 * LANES, 2 * LANES), 1)
    same_head = (ri % LANES) // D_HEAD == (ci % LANES) // D_HEAD
    ones_s = jnp.where(jnp.logical_and(same_head, ri // LANES >= ci // LANES), 1.0, 0.0).astype(BF16)

    def hilo(parts):
        x = jnp.concatenate(parts, axis=0)
        hi = x.astype(BF16)
        return jnp.concatenate([hi, (x - hi.astype(F32)).astype(BF16)], axis=1)

    def per_pair(res):
        return [res[p * D_HEAD:(p + 1) * D_HEAD] for p in pairs]

    def y_rows(yt, ycol, i):
        return [jnp.where(sub == i, jnp.sum(ycol[p] * eye2, axis=0, keepdims=True), yt[p]) for p in pairs]

    def body(grp, state):
        base = pl.multiple_of(grp * RWKV_GRP, RWKV_GRP)
        tile = lambda scr: [scr[pl.ds(base, RWKV_GRP), p * LANES:(p + 1) * LANES] for p in pairs]
        nkk_t, c_t, be_t, ga_t = tile(nkk_scr), tile(c_scr), tile(be_scr), tile(ga_scr)
        w_t, b_t, k_t, r_t, v_t = tile(w_scr), tile(b_scr), tile(k_scr), tile(r_scr), tile(v_scr)
        vh = [x.astype(BF16).astype(F32) for x in v_t]
        vl = [x - h for x, h in zip(v_t, vh)]

        def vcol_lhs(i):
            return jnp.concatenate([jnp.concatenate([vh[p][i:i + 1] * eye2 for p in pairs], axis=0).astype(BF16),
                                    jnp.concatenate([vl[p][i:i + 1] * eye2 for p in pairs], axis=0).astype(BF16)],
                                   axis=1)

        def segsums(lhs):
            res = _dot(jnp.concatenate(lhs, axis=0), ones_s)
            return [res[i * nrow:(i + 1) * nrow] for i in range(len(lhs))]

        def read_out(yt, res, i):
            second = res[:, LANES:2 * LANES]
            yt = y_rows(yt, per_pair(res[:, 0:LANES] - second), i)
            return y_rows(yt, per_pair(second), i + 1)

        yt = [jnp.zeros((RWKV_GRP, LANES), F32) for _ in pairs]
        sp = list(state)
        vc = [per_pair(r[:, 0:LANES]) for r in segsums([vcol_lhs(0), vcol_lhs(1)])]
        yprods = None
        for t0 in range(0, RWKV_GRP, 2):
            t1 = t0 + 1
            r0, r1 = slice(t0, t0 + 1), slice(t1, t1 + 1)
            more = t0 + 2 < RWKV_GRP
            lhs = [hilo([sp[p] * nkk_t[p][r0] for p in pairs]), hilo([sp[p] * c_t[p][r1] for p in pairs])]
            if more:
                lhs += [vcol_lhs(t0 + 2), vcol_lhs(t0 + 3)]
            if yprods is not None:
                lhs.append(yprods)
            res = segsums(lhs)
            sa0, tmp = per_pair(res[0][:, 0:LANES]), per_pair(res[1][:, 0:LANES])
            if yprods is not None:
                yt = read_out(yt, res[-1], t0 - 2)
            s0 = [sp[p] * w_t[p][r0] + sa0[p] * b_t[p][r0] + vc[0][p] * k_t[p][r0] for p in pairs]
            sa1 = [tmp[p] + sa0[p] * be_t[p][r1] + vc[0][p] * ga_t[p][r1] for p in pairs]
            sp = [s0[p] * w_t[p][r1] + sa1[p] * b_t[p][r1] + vc[1][p] * k_t[p][r1] for p in pairs]
            yprods = jnp.concatenate(
                [jnp.concatenate([s0[p] * r_t[p][r0] for p in pairs], axis=0).astype(BF16),
                 jnp.concatenate([sp[p] * r_t[p][r1] for p in pairs], axis=0).astype(BF16)], axis=1)
            if more:
                vc = [per_pair(res[2][:, 0:LANES]), per_pair(res[3][:, 0:LANES])]
        yt = read_out(yt, segsums([yprods])[0], RWKV_GRP - 2)
        for p in pairs:
            y_scr[pl.ds(base, RWKV_GRP), p * LANES:(p + 1) * LANES] = yt[p]
        return tuple(sp)

    state = lax.fori_loop(0, tb_len // RWKV_GRP, body, tuple(s_scr[p] for p in pairs))
    for p in pairs:
        s_scr[p] = state[p]

    y = y_scr[...]
    inv = 1.0 / D_HEAD
    mean = _segsum(y, ones_h) * inv
    yc = y - mean
    var = _segsum(yc * yc, ones_h) * inv
    y = yc * lax.rsqrt(var + D_GN_EPS) * lng_ref[...] + lnb_ref[...]
    y = y + _segsum(r * k2 * rk_ref[...], ones_h) * v
    rows = tb * tb_len + lax.broadcasted_iota(I32, (tb_len, 1), 0)
    o_ref[...] = jnp.where(rows >= n_pad, y * g, 0.0)

    @pl.when(tb == nt - 1)
    def _():
        s_out[0] = s_scr[...]


def rwkv(proj, sh0, s0, mu, w0, ww2, a0, wa2, wg2, kkp, kap, rkp, lng, lnb, *, bsz, t, tb_len, n_pad):
    nt = t // tb_len
    npair = D_HEADS // 2
    ncol = mu.shape[1]
    nlo = ncol - 3 * D_WIDTH
    row = lambda c: (lambda b, i: (b * nt + i, c))
    fix2 = lambda b, i: (0, 0)
    vec = pl.BlockSpec((1, D_WIDTH), fix2)
    st = pl.BlockSpec((1, npair, D_HEAD, LANES), lambda b, i: (b, 0, 0, 0))
    big = lambda: pltpu.VMEM((tb_len, D_WIDTH), F32)
    return pl.pallas_call(
        functools.partial(_rwkv_kernel, tb_len=tb_len, n_pad=n_pad),
        grid=(bsz, nt),
        in_specs=[pl.BlockSpec((tb_len, D_WIDTH), row(1)), pl.BlockSpec((tb_len, D_WIDTH), row(2)),
                  pl.BlockSpec((tb_len, D_WIDTH), row(3)),
                  pl.BlockSpec((tb_len, nlo), row(4 * D_WIDTH // nlo)),
                  pl.BlockSpec((1, 1, ncol), lambda b, i: (b, 0, 0)), st,
                  pl.BlockSpec((1, ncol), fix2), vec,
                  pl.BlockSpec(ww2.shape, fix2), vec, pl.BlockSpec(wa2.shape, fix2), pl.BlockSpec(wg2.shape, fix2),
                  vec, vec, vec, vec, vec],
        out_specs=(pl.BlockSpec((tb_len, D_WIDTH), lambda b, i: (b * nt + i, 0)), st),
        out_shape=(jax.ShapeDtypeStruct((bsz * t, D_WIDTH), F32),
                   jax.ShapeDtypeStruct((bsz, npair, D_HEAD, LANES), F32)),
        scratch_shapes=[pltpu.VMEM((tb_len + 8, ncol), F32), pltpu.VMEM((npair, D_HEAD, LANES), F32),
                        big(), big(), big(), big(), big(), big(), big(), big(), big(), big()],
        compiler_params=_cparams(("parallel", "arbitrary")),
        name="rwkv7",
    )(proj, proj, proj, proj, sh0, s0, mu, w0, ww2, a0, wa2, wg2, kkp, kap, rkp, lng, lnb)


def _kv_pack_kernel(*refs, n_layers, lead_rows):
    e = pl.program_id(0)
    ins, (ko_ref, vo_ref) = refs[:4 * n_layers], refs[4 * n_layers:]
    tm = ins[0].shape[0]
    off = lead_rows % tm
    for layer in range(n_layers):
        @pl.when(e == layer)
        def _(layer=layer):
            for (a_ref, b_ref), o_ref in ((ins[4 * layer:4 * layer + 2], ko_ref), (ins[4 * layer + 2:4 * layer + 4], vo_ref)):
                for h in range(A_HEADS):
                    cols = slice(h * A_DV, (h + 1) * A_DV)
                    o_ref[0, 0, 0:tm - off, h, :] = a_ref[off:tm, cols]
                    o_ref[0, 0, tm - off:tm, h, :] = b_ref[0:off, cols]


def kv_pack(projs, n_rows, lead_rows, tm=ROW_TILE):
    n_layers = len(projs)
    nb_in = projs[0].shape[0] // tm
    first = lead_rows // tm
    assert lead_rows % 8 == 0
    nb = -(-n_rows // tm)
    in_specs, args = [], []
    for layer, pr in enumerate(projs):
        for col in (1, 2):
            cur = lambda e, b, layer=layer, col=col: (jnp.where(e == layer, first + b, 0), col)
            nxt = lambda e, b, layer=layer, col=col: (jnp.where(e == layer, jnp.minimum(first + b + 1, nb_in - 1), 0), col)
            in_specs += [pl.BlockSpec((tm, HALF), cur), pl.BlockSpec((tm, HALF), nxt)]
            args += [pr, pr]
    out_spec = pl.BlockSpec((1, 1, tm, A_HEADS, A_DV), lambda e, b: (e, 0, b, 0, 0))
    shape = jax.ShapeDtypeStruct((n_layers, 1, n_rows, A_HEADS, A_DV), F32)
    return pl.pallas_call(
        functools.partial(_kv_pack_kernel, n_layers=n_layers, lead_rows=lead_rows),
        grid=(n_layers, nb),
        in_specs=in_specs,
        out_specs=(out_spec, out_spec),
        out_shape=(shape, shape),
        compiler_params=_cparams(("arbitrary", "arbitrary")),
        name="kv_pack",
    )(*args)


def _pad_cols(w, mult=LANES):
    n = w.shape[1]
    return jnp.pad(w, ((0, 0), (0, (-n) % mult)))


def _pairs_from_heads(s):
    b = s.shape[0]
    return s.reshape(b, D_HEADS // 2, 2, D_HEAD, D_HEAD).transpose(0, 1, 3, 2, 4).reshape(b, D_HEADS // 2, D_HEAD, LANES)


def _heads_from_pairs(s):
    b = s.shape[0]
    return s.reshape(b, D_HEADS // 2, D_HEAD, 2, D_HEAD).transpose(0, 1, 3, 2, 4).reshape(b, D_HEADS, D_HEAD, D_HEAD)


def _trunk(x, p, *, bsz, t, seq_blk, n_pad, attn_fn, st, drop_rows=0):
    depth = p['norm_mix'].shape[0]
    new = {k: [] for k in ('a_k', 'a_v', 'b_c', 'b_n', 'b_m', 'b_conv', 'c_re', 'c_im', 'd_s', 'd_shift', 'proj_even')}
    nt = t // seq_blk
    y = None
    for layer in range(depth):
        g_mix = p['norm_mix'][layer][None]
        if layer % 2 == 0:
            e = layer // 2
            proj = rms_matmul(x, g_mix, p['ev_w_in'][e])
            lam_init = 0.8 - 0.6 * math.exp(-0.3 * layer)
            oa = attn_fn(proj, e, lam_init)
            gcols = proj[:, 7 * HALF:7 * HALF + 2 * B_HEADS]
            gates_r = gcols.reshape(bsz * nt, seq_blk, 2 * B_HEADS).transpose(0, 2, 1)
            hb, bc, bn, bm = mlstm(proj, gates_r, st['b_c'][e], st['b_n'][e][:, :, None, :], st['b_m'][e][:, None, :],
                                   st['b_conv'][e], p['b_conv_w'][e], p['b_conv_b'][e][None],
                                   p['b_ig_bias'][e], p['b_fg_bias'][e], p['b_norm'][e][None],
                                   bsz=bsz, t=t, ln=seq_blk, n_pad=n_pad)
            p3 = proj.reshape(bsz, t, -1)
            new['a_k'].append(p3[:, :, HALF:2 * HALF])
            new['a_v'].append(p3[:, :, 2 * HALF:3 * HALF])
            new['proj_even'].append(proj)
            new['b_c'].append(bc)
            new['b_n'].append(bn[:, :, 0, :])
            new['b_m'].append(bm[:, 0, :])
            new['b_conv'].append(p3[:, t - (B_CONV - 1):, 3 * HALF:5 * HALF])
            x = outproj(x, oa, hb, p['ev_w_out'][e][:HALF], p['ev_w_out'][e][HALF:])
        else:
            o = layer // 2
            proj = rms_matmul(x, g_mix, p['od_w_in'][o])
            sp = p['s5'][o]
            oc, cre, cim = s5(proj, sp['pwr'], sp['pwi'], sp['amr'], sp['ami'], sp['bbr'], sp['bbi'], sp['cre'], sp['cim'],
                              p['c_d'][o][None], p['c_w_glu'][o],
                              st['c_re'][o].reshape(bsz, 1, -1), st['c_im'][o].reshape(bsz, 1, -1),
                              bsz=bsz, t=t, tb_len=seq_blk, n_pad=n_pad)
            od, ds = rwkv(proj, st['d_shift'][o], _pairs_from_heads(st['d_s'][o]),
                          p['d_mu'][o][None], p['d_w0'][o][None], p['d_w_w2'][o], p['d_a0'][o][None],
                          p['d_w_a2'][o], p['d_w_g2'][o], p['d_k_k'][o][None], p['d_k_a'][o][None],
                          p['d_r_k'][o][None], p['d_ln_g'][o][None], p['d_ln_b'][o][None],
                          bsz=bsz, t=t, tb_len=seq_blk, n_pad=n_pad)
            new['c_re'].append(cre.reshape(bsz, C_GROUPS, C_STATE))
            new['c_im'].append(cim.reshape(bsz, C_GROUPS, C_STATE))
            new['d_s'].append(_heads_from_pairs(ds))
            new['d_shift'].append(proj.reshape(bsz, t, -1)[:, -1:, HALF:])
            x = outproj(x, oc, od, p['od_w_out'][o][:HALF], p['od_w_out'][o][HALF:])
        last = layer == depth - 1
        x = ffn(x, p['norm_ffn'][layer][None], p['ffn_w1'][layer], p['ffn_w3'][layer], p['ffn_w2'][layer],
                gf=p['norm_final'][None] if last else None, drop_rows=drop_rows if last else 0)
    return x, new


def kernel(x_prompt, x_sample, cache_a_k, cache_a_v, state_b_c, state_b_n, state_b_m, state_b_conv, state_c_re, state_c_im, state_d_s, state_d_shift, meta_tokens, rel_bias, norm_mix, norm_ffn, norm_final, ev_w_in, ev_w_out, a_lq1, a_lk1, a_lq2, a_lk2, a_subln, b_conv_w, b_conv_b, b_ig_bias, b_fg_bias, b_norm, od_w_in, od_w_out, c_lam_re, c_lam_im, c_log_dt, c_b_re, c_b_im, c_c_re, c_c_im, c_d, c_w_glu, d_mu, d_w0, d_w_w2, d_a0, d_w_a2, d_w_g2, d_k_k, d_k_a, d_r_k, d_ln_g, d_ln_b, ffn_w1, ffn_w3, ffn_w2):
    bp, sp_len, dm = x_prompt.shape
    bs, ts, _ = x_sample.shape
    n_even, n_odd = ev_w_in.shape[0], od_w_in.shape[0]
    assert bp == 1 and sp_len % CHUNK == 0 and ts % RWKV_GRP == 0 and ts >= B_CONV - 1
    dt = x_prompt.dtype

    ns = C_GROUPS * C_STATE
    s5p = []
    for o in range(n_odd):
        bre = jnp.tile(c_b_re[o].transpose(2, 0, 1).reshape(C_GROUP, ns), (C_GROUPS, 1))
        bim = jnp.tile(c_b_im[o].transpose(2, 0, 1).reshape(C_GROUP, ns), (C_GROUPS, 1))
        pwr, pwi, amr, ami, bbr, bbi = s5_params(c_lam_re[o].reshape(1, ns), c_lam_im[o].reshape(1, ns),
                                     jnp.repeat(c_log_dt[o], C_STATE)[None], bre, bim)
        eye = jnp.eye(C_GROUPS, dtype=F32)
        blk = lambda c: (eye[:, None, :, None] * c.transpose(0, 2, 1)[:, :, None, :]).reshape(ns, C_WIDTH).astype(BF16)
        s5p.append(dict(pwr=pwr, pwi=pwi, amr=amr, ami=ami, bbr=bbr, bbi=bbi, cre=blk(c_c_re[o]), cim=blk(c_c_im[o])))
    p = dict(norm_mix=norm_mix, norm_ffn=norm_ffn, norm_final=norm_final,
             ev_w_in=[_pad_cols(ev_w_in[e]).astype(BF16) for e in range(n_even)],
             ev_w_out=ev_w_out.astype(BF16),
             od_w_in=[od_w_in[o].astype(BF16) for o in range(n_odd)], od_w_out=od_w_out.astype(BF16),
             b_conv_w=b_conv_w, b_conv_b=b_conv_b, b_ig_bias=b_ig_bias, b_fg_bias=b_fg_bias, b_norm=b_norm,
             s5=s5p, c_d=c_d, c_w_glu=c_w_glu.astype(BF16),
             d_mu=d_mu, d_w0=d_w0, d_w_w2=d_w_w2.astype(BF16), d_a0=d_a0, d_w_a2=d_w_a2.astype(BF16),
             d_w_g2=d_w_g2.astype(BF16), d_k_k=d_k_k, d_k_a=d_k_a, d_r_k=d_r_k, d_ln_g=d_ln_g, d_ln_b=d_ln_b,
             ffn_w1=ffn_w1.astype(BF16), ffn_w3=ffn_w3.astype(BF16), ffn_w2=ffn_w2.astype(BF16))
    lam_vecs = lambda e: (a_lq1[e][None], a_lk1[e][None], a_lq2[e][None], a_lk2[e][None], a_subln[e][None])

    tp = -(-(sp_len + CHUNK) // ATT_BLK) * ATT_BLK
    n_pad = tp - sp_len - N_META
    xp = jnp.concatenate([jnp.zeros((n_pad, dm), dt), meta_tokens.astype(dt), x_prompt[0]], axis=0)
    bias2 = jnp.stack([bias_tile(rel_bias, ATT_BLK, ATT_BLK, 0, causal=True, scale=LOG2E),
                       bias_tile(rel_bias, ATT_BLK, ATT_BLK, -ATT_BLK, scale=LOG2E)], axis=0)

    def attn_p(proj, e, lam_init):
        return attn_prompt(proj, bias2, rel_bias, *lam_vecs(e), n_pad=n_pad, lam_init=lam_init)

    zeros = lambda *s: jnp.zeros(s, F32)
    st_p = dict(b_c=zeros(n_even, bp, B_HEADS, B_DK, B_DK), b_n=zeros(n_even, bp, B_HEADS, B_DK),
                b_m=zeros(n_even, bp, B_HEADS), b_conv=zeros(n_even, bp, B_CONV - 1, 2 * HALF),
                c_re=zeros(n_odd, bp, C_GROUPS, C_STATE), c_im=zeros(n_odd, bp, C_GROUPS, C_STATE),
                d_s=zeros(n_odd, bp, D_HEADS, D_HEAD, D_HEAD), d_shift=zeros(n_odd, bp, 1, d_mu.shape[1]))
    lead = n_pad + N_META
    drop = lead if lead % ROW_TILE == 0 else 0
    y_p, new_p = _trunk(xp, p, bsz=bp, t=tp, seq_blk=SEQ_BLK, n_pad=n_pad, attn_fn=attn_p, st=st_p, drop_rows=drop)

    tk = cache_a_k.shape[2]
    past_len = tk - N_META
    cid = lambda pos: np.where(pos < N_META, 0, 1 + (pos - N_META) // CHUNK)
    q_pos = N_META + past_len + np.arange(ts)
    k_pos = np.arange(tk + ts)
    assert (cid(k_pos)[None, :] <= cid(q_pos)[:, None]).all()
    bias_past = bias_tile(rel_bias, ts, tk, -(N_META + past_len))
    bias_new = bias_tile(rel_bias, ts, ts, 0)

    def attn_s(proj, e, lam_init):
        return attn_sample(proj, cache_a_k[e].reshape(bs, tk, HALF), cache_a_v[e].reshape(bs, tk, HALF),
                           bias_past, bias_new, *lam_vecs(e), ts=ts, lam_init=lam_init)

    st_s = dict(b_c=state_b_c, b_n=state_b_n, b_m=state_b_m, b_conv=state_b_conv, c_re=state_c_re,
                c_im=state_c_im, d_s=state_d_s, d_shift=state_d_shift)
    y_s, new_s = _trunk(x_sample.reshape(bs * ts, dm), p, bsz=bs, t=ts, seq_blk=ts, n_pad=0, attn_fn=attn_s, st=st_s)

    def pack(new, bsz, t, drop):
        if drop:
            ak, av = kv_pack(new['proj_even'], t - drop, drop)
        else:
            kv = lambda a: a.reshape(bsz, t, A_HEADS, A_DV)
            ak, av = jnp.stack([kv(a) for a in new['a_k']]), jnp.stack([kv(a) for a in new['a_v']])
        return (ak, av, jnp.stack(new['b_c']), jnp.stack(new['b_n']), jnp.stack(new['b_m']), jnp.stack(new['b_conv']),
                jnp.stack(new['c_re']), jnp.stack(new['c_im']), jnp.stack(new['d_s']), jnp.stack(new['d_shift']))

    out_p = pack(new_p, bp, tp, n_pad)
    out_s = pack(new_s, bs, ts, 0)
    return (y_p[lead - drop:][None], y_s.reshape(bs, ts, dm)) + out_p + out_s
```

```python
import functools
import math

import numpy as np
import jax
import jax.numpy as jnp
from jax import lax
from jax.experimental import pallas as pl
from jax.experimental.pallas import tpu as pltpu

F32 = jnp.float32
BF16 = jnp.bfloat16
I32 = jnp.int32

CHUNK = 64
N_META = 16
EPS = 1e-6
A_HEADS = 4
A_DH = 64
A_DV = 128
N_BUCKETS = 32
MAX_DIST = 128
B_HEADS = 4
B_DK = 128
B_CONV = 4
C_GROUP = 16
C_GROUPS = 32
C_STATE = 64
C_WIDTH = C_GROUP * C_GROUPS
D_HEAD = 64
D_HEADS = 8
D_WIDTH = D_HEAD * D_HEADS
D_GN_EPS = 64e-5
HALF = 512

NEG = -1e30
LOG2E = math.log2(math.e)
LANES = 128
VMEM_LIMIT = 56 * 1024 * 1024

ROW_TILE = 512
ATT_BLK = 512
ATT_STRIP = 64
SEQ_BLK = 128
S5_UT = 128
S5_YT = 256
S5_GRP = 8
S5_SHIFTS = (1, 2, 4)
RWKV_GRP = 32
FFN_TH = 1408


def _cparams(sem):
    return pltpu.CompilerParams(dimension_semantics=sem, vmem_limit_bytes=VMEM_LIMIT)


def _dot(a, b):
    return jnp.dot(a, b, preferred_element_type=F32)


def _dot_nt(a, b):
    return lax.dot_general(a, b, (((1,), (1,)), ((), ())), preferred_element_type=F32)


def _dot_tn(a, b):
    return lax.dot_general(a, b, (((0,), (0,)), ((), ())), preferred_element_type=F32)


def _sigmoid(x):
    return 1.0 / (1.0 + jnp.exp(-x))


def _softplus(x):
    return jnp.maximum(x, 0.0) + jnp.log1p(jnp.exp(-jnp.abs(x)))


def _rms(x, g):
    return x * lax.rsqrt(jnp.mean(x * x, axis=-1, keepdims=True) + EPS) * g


def _segsum(x, ones):
    n, w = x.shape[0], ones.shape[0]
    hi = x.astype(BF16)
    lo = (x - hi.astype(F32)).astype(BF16)
    outs = []
    for c in range(x.shape[1] // w):
        res = _dot(jnp.concatenate([hi[:, c * w:(c + 1) * w], lo[:, c * w:(c + 1) * w]], axis=0), ones)
        outs.append(res[:n] + res[n:])
    return jnp.concatenate(outs, axis=1)


def _block_ones(n, seg):
    r = lax.broadcasted_iota(I32, (n, n), 0) // seg
    c = lax.broadcasted_iota(I32, (n, n), 1) // seg
    return jnp.where(r == c, 1.0, 0.0).astype(BF16)


def _rms_mm_kernel(x_ref, g_ref, w_ref, o_ref):
    h = _rms(x_ref[...], g_ref[...])
    o_ref[...] = _dot(h.astype(BF16), w_ref[...])


def rms_matmul(x, g, w, tm=ROW_TILE):
    m, d = x.shape
    tm = min(tm, m)
    n = w.shape[1]
    return pl.pallas_call(
        _rms_mm_kernel,
        grid=(m // tm,),
        in_specs=[pl.BlockSpec((tm, d), lambda i: (i, 0)),
                  pl.BlockSpec((1, d), lambda i: (0, 0)),
                  pl.BlockSpec((d, n), lambda i: (0, 0))],
        out_specs=pl.BlockSpec((tm, n), lambda i: (i, 0)),
        out_shape=jax.ShapeDtypeStruct((m, n), F32),
        compiler_params=_cparams(("parallel",)),
        name="rms_matmul",
    )(x, g, w)


def _ffn_kernel(x_ref, a_ref, b_ref, wa_ref, wb_ref, g_ref, gf_ref, w1_ref, w3_ref, w2_ref, o_ref, h_scr,
                *, final_norm):
    j = pl.program_id(1)

    @pl.when(j == 0)
    def _():
        mix = _dot(a_ref[...].astype(BF16), wa_ref[...]) + _dot(b_ref[...].astype(BF16), wb_ref[...])
        x = x_ref[...] + mix
        h_scr[...] = _rms(x, g_ref[...]).astype(BF16)
        o_ref[...] = x

    h = h_scr[...]
    a = _dot(h, w1_ref[...])
    b = _dot(h, w3_ref[...])
    u = (a * _sigmoid(a)) * b
    o_ref[...] += _dot(u.astype(BF16), w2_ref[...])

    if final_norm:
        @pl.when(j == pl.num_programs(1) - 1)
        def _():
            o_ref[...] = _rms(o_ref[...], gf_ref[...])


def ffn(x, a, b, wa, wb, g, w1, w3, w2, gf=None, drop_rows=0, tm=ROW_TILE, th=FFN_TH):
    m, d = x.shape
    k = a.shape[1]
    tm = min(tm, m)
    hid = w1.shape[1]
    final_norm = gf is not None
    if gf is None:
        gf = g
    skip, rem = divmod(drop_rows, tm)
    assert rem == 0
    return pl.pallas_call(
        functools.partial(_ffn_kernel, final_norm=final_norm),
        grid=(m // tm, hid // th),
        in_specs=[pl.BlockSpec((tm, d), lambda i, j: (i, 0)),
                  pl.BlockSpec((tm, k), lambda i, j: (i, 0)), pl.BlockSpec((tm, k), lambda i, j: (i, 0)),
                  pl.BlockSpec((k, d), lambda i, j: (0, 0)), pl.BlockSpec((k, d), lambda i, j: (0, 0)),
                  pl.BlockSpec((1, d), lambda i, j: (0, 0)),
                  pl.BlockSpec((1, d), lambda i, j: (0, 0)),
                  pl.BlockSpec((d, th), lambda i, j: (0, j)),
                  pl.BlockSpec((d, th), lambda i, j: (0, j)),
                  pl.BlockSpec((th, d), lambda i, j: (j, 0))],
        out_specs=pl.BlockSpec((tm, d), lambda i, j: (jnp.maximum(i - skip, 0), 0)),
        out_shape=jax.ShapeDtypeStruct((m - drop_rows, d), F32),
        scratch_shapes=[pltpu.VMEM((tm, d), BF16)],
        compiler_params=_cparams(("arbitrary" if skip else "parallel", "arbitrary")),
        name="ffn",
    )(x, a, b, wa, wb, g, gf, w1, w3, w2)


def _bias_kernel(rb_ref, o_ref, *, rel0, causal, scale):
    nq, nk = o_ref.shape[1], o_ref.shape[2]
    a = lax.broadcasted_iota(I32, (nq, nk), 0)
    b = lax.broadcasted_iota(I32, (nq, nk), 1)
    rel = b - a + rel0
    nb = N_BUCKETS // 2
    max_exact = nb // 2
    ret = jnp.where(rel > 0, nb, 0)
    n = jnp.abs(rel)
    nf = jnp.maximum(n, 1).astype(F32)
    large = max_exact + (jnp.log(nf / max_exact) / math.log(MAX_DIST / max_exact) * (nb - max_exact)).astype(I32)
    large = jnp.minimum(large, nb - 1)
    bucket = ret + jnp.where(n < max_exact, n, large)
    for h in range(A_HEADS):
        acc = jnp.zeros((nq, nk), F32)
        for bk in range(N_BUCKETS):
            acc = jnp.where(bucket == bk, rb_ref[bk, h] * scale, acc)
        if causal:
            acc = jnp.where((b // CHUNK) <= (a // CHUNK), acc, NEG)
        o_ref[h] = acc


def bias_tile(rel_bias, nq, nk, rel0, causal=False, scale=1.0):
    return pl.pallas_call(
        functools.partial(_bias_kernel, rel0=rel0, causal=causal, scale=scale),
        in_specs=[pl.BlockSpec(memory_space=pltpu.SMEM)],
        out_specs=pl.BlockSpec(memory_space=pltpu.VMEM),
        out_shape=jax.ShapeDtypeStruct((A_HEADS, nq, nk), F32),
        compiler_params=pltpu.CompilerParams(vmem_limit_bytes=VMEM_LIMIT),
        name="bias_tile",
    )(rel_bias)


def _lambda(lq1_ref, lk1_ref, lq2_ref, lk2_ref, lam_init):
    s1 = jnp.sum(lq1_ref[...] * lk1_ref[...], axis=-1, keepdims=True)
    s2 = jnp.sum(lq2_ref[...] * lk2_ref[...], axis=-1, keepdims=True)
    return jnp.exp(s1) - jnp.exp(s2) + lam_init


def _attn_finish(acc0, l0, acc1, l1, lam, g, lam_init):
    o = acc0 / l0 - lam * (acc1 / l1)
    return _rms(o, g) * (1.0 - lam_init)


def _attn_prompt_kernel(qi_ref, kj_ref, rb_ref, q_ref, k_ref, v_ref, bias_ref,
                        lq1_ref, lk1_ref, lq2_ref, lk2_ref, g_ref, o_ref,
                        m_scr, l_scr, acc_scr, s_scr, p_scr, al_scr, *, blk, n_pad, lam_init):
    s = pl.program_id(0)
    i = qi_ref[s]
    j = kj_ref[s]
    d = i - j
    nct = blk // LANES
    nstrip = blk // ATT_STRIP

    @pl.when(j == 0)
    def _():
        m_scr[...] = jnp.full(m_scr.shape, NEG, F32)
        l_scr[...] = jnp.zeros(l_scr.shape, F32)
        acc_scr[...] = jnp.zeros(acc_scr.shape, F32)

    def update(general):
        for h in range(A_HEADS):
            far = rb_ref[N_BUCKETS // 2 - 1, h] * LOG2E
            for mm in range(2):
                c0 = h * 2 * A_DH + mm * A_DH
                qb = (q_ref[:, c0:c0 + A_DH] * (A_DH ** -0.5 * LOG2E)).astype(BF16)
                s_scr[2 * h + mm] = _dot_nt(qb, k_ref[:, c0:c0 + A_DH].astype(BF16))
            for mm in range(2):
                idx = 2 * h + mm
                for r in range(nstrip):
                    rows = slice(r * ATT_STRIP, (r + 1) * ATT_STRIP)
                    tiles = []
                    for c in range(nct):
                        cols = slice(c * LANES, (c + 1) * LANES)
                        t = s_scr[idx, rows, cols]
                        if general:
                            kpos = j * blk + c * LANES + lax.broadcasted_iota(I32, (1, LANES), 1)
                            near = bias_ref[jnp.minimum(d, 1), h, rows, cols]
                            t = t + jnp.where(d < 2, near, far) + jnp.where(kpos < n_pad, NEG, 0.0)
                        tiles.append(t)
                    mx = functools.reduce(jnp.maximum, tiles)
                    m_cur = jnp.broadcast_to(jnp.max(mx, axis=-1, keepdims=True), (ATT_STRIP, LANES))
                    if not general:
                        m_cur = m_cur + far
                    m_old = m_scr[idx, rows, :]
                    m_new = jnp.maximum(m_old, m_cur)
                    alpha = jnp.exp2(m_old - m_new)
                    m_sub = m_new if general else m_new - far
                    ps = [jnp.exp2(t - m_sub) for t in tiles]
                    l_scr[idx, rows, :] = alpha * l_scr[idx, rows, :] + functools.reduce(jnp.add, ps)
                    m_scr[idx, rows, :] = m_new
                    al_scr[idx, rows, :] = alpha
                    for c in range(nct):
                        p_scr[h, mm * blk + r * ATT_STRIP:mm * blk + (r + 1) * ATT_STRIP,
                              c * LANES:(c + 1) * LANES] = ps[c].astype(BF16)
            pv = _dot(p_scr[h], v_ref[:, h * A_DV:(h + 1) * A_DV].astype(BF16))
            for mm in range(2):
                idx = 2 * h + mm
                acc_scr[idx] = al_scr[idx] * acc_scr[idx] + pv[mm * blk:(mm + 1) * blk]

    is_far = jnp.logical_and(d >= 2, j > 0)

    @pl.when(is_far)
    def _():
        update(False)

    @pl.when(jnp.logical_not(is_far))
    def _():
        update(True)

    @pl.when(d == 0)
    def _():
        lam = _lambda(lq1_ref, lk1_ref, lq2_ref, lk2_ref, lam_init)
        rows = i * blk + lax.broadcasted_iota(I32, (blk, 1), 0)
        valid = rows >= n_pad
        for h in range(A_HEADS):
            l0 = jnp.sum(l_scr[2 * h], axis=-1, keepdims=True)
            l1 = jnp.sum(l_scr[2 * h + 1], axis=-1, keepdims=True)
            y = _attn_finish(acc_scr[2 * h], l0, acc_scr[2 * h + 1], l1, lam, g_ref[...], lam_init)
            o_ref[:, h * A_DV:(h + 1) * A_DV] = jnp.where(valid, y, 0.0)


def attn_prompt(proj, bias2, rel_bias, lq1, lk1, lq2, lk2, g, *, n_pad, lam_init, blk=ATT_BLK):
    t = proj.shape[0]
    nb = t // blk
    qi = np.array([i for i in range(nb) for _ in range(i + 1)], np.int32)
    kj = np.array([j for i in range(nb) for j in range(i + 1)], np.int32)
    vec = lambda n: pl.BlockSpec((1, n), lambda s, qi, kj: (0, 0))
    grid_spec = pltpu.PrefetchScalarGridSpec(
        num_scalar_prefetch=2,
        grid=(len(qi),),
        in_specs=[pl.BlockSpec(memory_space=pltpu.SMEM),
                  pl.BlockSpec((blk, HALF), lambda s, qi, kj: (qi[s], 0)),
                  pl.BlockSpec((blk, HALF), lambda s, qi, kj: (kj[s], 1)),
                  pl.BlockSpec((blk, HALF), lambda s, qi, kj: (kj[s], 2)),
                  pl.BlockSpec((2, A_HEADS, blk, blk), lambda s, qi, kj: (0, 0, 0, 0)),
                  vec(A_DH), vec(A_DH), vec(A_DH), vec(A_DH), vec(A_DV)],
        out_specs=pl.BlockSpec((blk, HALF), lambda s, qi, kj: (qi[s], 0)),
        scratch_shapes=[pltpu.VMEM((2 * A_HEADS, blk, LANES), F32),
                        pltpu.VMEM((2 * A_HEADS, blk, LANES), F32),
                        pltpu.VMEM((2 * A_HEADS, blk, A_DV), F32),
                        pltpu.VMEM((2 * A_HEADS, blk, blk), F32),
                        pltpu.VMEM((A_HEADS, 2 * blk, blk), BF16),
                        pltpu.VMEM((2 * A_HEADS, blk, LANES), F32)],
    )
    assert n_pad <= blk and blk >= MAX_DIST
    return pl.pallas_call(
        functools.partial(_attn_prompt_kernel, blk=blk, n_pad=n_pad, lam_init=lam_init),
        grid_spec=grid_spec,
        out_shape=jax.ShapeDtypeStruct((t, HALF), F32),
        compiler_params=_cparams(("arbitrary",)),
        name="attn_prompt",
    )(jnp.asarray(qi), jnp.asarray(kj), rel_bias, proj, proj, proj, bias2, lq1, lk1, lq2, lk2, g)


def _attn_sample_kernel(q_ref, kn_ref, vn_ref, kc_ref, vc_ref, bp_ref, bn_ref,
                        lq1_ref, lk1_ref, lq2_ref, lk2_ref, g_ref, o_ref, *, lam_init):
    lam = _lambda(lq1_ref, lk1_ref, lq2_ref, lk2_ref, lam_init)
    k_all = pltpu.einshape("khd->hkd", kc_ref[0, 0])
    v_all = pltpu.einshape("khd->hkd", vc_ref[0, 0])
    for h in range(A_HEADS):
        vsl = slice(h * A_DV, (h + 1) * A_DV)
        vp = v_all[h].astype(BF16)
        kp = k_all[h].astype(BF16)
        vn = vn_ref[:, vsl].astype(BF16)
        outs = []
        for mm in range(2):
            c0 = h * 2 * A_DH + mm * A_DH
            qb = (q_ref[:, c0:c0 + A_DH] * (A_DH ** -0.5)).astype(BF16)
            sp = _dot_nt(qb, kp[:, mm * A_DH:(mm + 1) * A_DH]) + bp_ref[h]
            sn = _dot_nt(qb, kn_ref[:, c0:c0 + A_DH].astype(BF16)) + bn_ref[h]
            mx = jnp.maximum(jnp.max(sp, axis=-1, keepdims=True), jnp.max(sn, axis=-1, keepdims=True))
            pp = jnp.exp(sp - mx)
            pn = jnp.exp(sn - mx)
            l = jnp.sum(pp, axis=-1, keepdims=True) + jnp.sum(pn, axis=-1, keepdims=True)
            acc = _dot(pp.astype(BF16), vp) + _dot(pn.astype(BF16), vn)
            outs.append((acc, l))
        o_ref[:, vsl] = _attn_finish(outs[0][0], outs[0][1], outs[1][0], outs[1][1], lam, g_ref[...], lam_init)


def attn_sample(proj, kc, vc, e, bias_past, bias_new, lq1, lk1, lq2, lk2, g, *, ts, lam_init):
    _, bsz, tk, nh, dv = kc.shape
    slab = pl.BlockSpec((1, 1, tk, nh, dv), lambda b: (e, b, 0, 0, 0))
    vec = lambda n: pl.BlockSpec((1, n), lambda b: (0, 0))
    return pl.pallas_call(
        functools.partial(_attn_sample_kernel, lam_init=lam_init),
        grid=(bsz,),
        in_specs=[pl.BlockSpec((ts, HALF), lambda b: (b, 0)),
                  pl.BlockSpec((ts, HALF), lambda b: (b, 1)),
                  pl.BlockSpec((ts, HALF), lambda b: (b, 2)),
                  slab, slab,
                  pl.BlockSpec((A_HEADS, ts, tk), lambda b: (0, 0, 0)),
                  pl.BlockSpec((A_HEADS, ts, ts), lambda b: (0, 0, 0)),
                  vec(A_DH), vec(A_DH), vec(A_DH), vec(A_DH), vec(A_DV)],
        out_specs=pl.BlockSpec((ts, HALF), lambda b: (b, 0)),
        out_shape=jax.ShapeDtypeStruct((bsz * ts, HALF), F32),
        compiler_params=_cparams(("parallel",)),
        name="attn_sample",
    )(proj, proj, proj, kc, vc, bias_past, bias_new, lq1, lk1, lq2, lk2, g)


def _mlstm_kernel(bq_ref, bk_ref, bv_ref, bo_ref, gc_ref, gr_ref, c0_ref, n0_ref, m0_ref, conv0_ref,
                  cw_ref, cb_ref, gbc_ref, gbr_ref, bn_ref,
                  h_ref, c_out, n_out, m_out,
                  xbuf, c_scr, n_scr, m_scr, *, ln, n_pad):
    tb = pl.program_id(1)
    nt = pl.num_programs(1)
    halo = B_CONV - 1
    base = 8 - halo

    @pl.when(tb == 0)
    def _():
        xbuf[base:8, :] = conv0_ref[0]
        c_scr[...] = c0_ref[0]
        n_scr[...] = n0_ref[0]
        for h in range(B_HEADS):
            m_scr[h] = m0_ref[0, :, h:h + 1]

    xbuf[8:8 + ln, 0:HALF] = bq_ref[...]
    xbuf[8:8 + ln, HALF:2 * HALF] = bk_ref[...]
    conv = cb_ref[...] + cw_ref[0:1, :] * xbuf[base:base + ln, :]
    for jj in range(1, B_CONV):
        conv = conv + cw_ref[jj:jj + 1, :] * xbuf[base + jj:base + jj + ln, :]
    xbuf[base:8, :] = xbuf[8 + ln - halo:8 + ln, :]
    conv = conv * _sigmoid(conv)
    q_all = conv[:, 0:HALF] * (B_DK ** -0.5)
    k_all = conv[:, HALF:2 * HALF]

    rows = tb * ln + lax.broadcasted_iota(I32, (ln, 1), 0)
    cols = tb * ln + lax.broadcasted_iota(I32, (1, ln), 1)
    valid_c = rows >= n_pad
    valid_r = cols >= n_pad

    gc = gc_ref[...] + gbc_ref[...]
    gr = gr_ref[0] + gbr_ref[...]
    li_c = jnp.where(valid_c, gc, NEG)
    li_r = jnp.where(valid_r, gr, NEG)
    lf_c = jnp.where(valid_c, -_softplus(-gc), 0.0)
    lf_r = jnp.where(valid_r, -_softplus(-gr), 0.0)

    ri = lax.broadcasted_iota(I32, (ln, ln), 0)
    ci = lax.broadcasted_iota(I32, (ln, ln), 1)
    tril = ri >= ci
    tril_f = jnp.where(tril, 1.0, 0.0)
    triu_f = jnp.where(ri <= ci, 1.0, 0.0)
    b_c = jnp.dot(tril_f, lf_c, preferred_element_type=F32, precision=lax.Precision.HIGHEST)
    b_r = jnp.dot(lf_r, triu_f, preferred_element_type=F32, precision=lax.Precision.HIGHEST)

    for h in range(B_HEADS):
        sl = slice(h * B_DK, (h + 1) * B_DK)
        qh = q_all[:, sl].astype(BF16)
        kh = k_all[:, sl]
        vh = bv_ref[:, sl].astype(BF16)
        c = c_scr[h]
        n = n_scr[h]
        m = m_scr[h]
        bc = b_c[:, B_HEADS + h:B_HEADS + h + 1]
        br = b_r[B_HEADS + h:B_HEADS + h + 1, :]
        inter = bc + m
        dmat = jnp.where(tril, bc - br + li_r[h:h + 1, :], NEG)
        mt = jnp.maximum(inter, jnp.max(dmat, axis=-1, keepdims=True))
        w_inter = jnp.exp(inter - mt)
        s = _dot_nt(qh, kh.astype(BF16)) * jnp.exp(dmat - mt)
        num = w_inter * _dot(qh, c.astype(BF16)) + _dot(s.astype(BF16), vh)
        qn = jnp.sum(qh.astype(F32) * n, axis=-1, keepdims=True)
        den = w_inter * qn + jnp.sum(s, axis=-1, keepdims=True)
        hh = num / jnp.maximum(jnp.abs(den), jnp.exp(-mt))
        b_last = bc[ln - 1:ln, :]
        g = b_last - bc + li_c[:, h:h + 1]
        m_new = jnp.maximum(b_last + m, jnp.max(g, axis=0, keepdims=True))
        decay = jnp.exp(b_last + m - m_new)
        wk = (jnp.exp(g - m_new) * kh)
        c_scr[h] = decay * c + _dot_tn(wk.astype(BF16), vh)
        n_scr[h] = decay * n + jnp.sum(wk, axis=0, keepdims=True)
        m_scr[h] = m_new
        hn = _rms(hh, bn_ref[:, sl]) * _sigmoid(bo_ref[:, sl])
        h_ref[:, sl] = jnp.where(valid_c, hn, 0.0)

    @pl.when(tb == nt - 1)
    def _():
        c_out[0] = c_scr[...]
        n_out[0] = n_scr[...]
        lane = lax.broadcasted_iota(I32, (1, B_HEADS), 1)
        mrow = jnp.zeros((1, B_HEADS), F32)
        for h in range(B_HEADS):
            mrow = jnp.where(lane == h, m_scr[h], mrow)
        m_out[0] = mrow


def mlstm(proj, gates_r, c0, n0, m0, conv0, cw, cb, igb, fgb, bnorm, *, bsz, t, ln, n_pad):
    nt = t // ln
    gb = jnp.concatenate([igb, fgb])
    gw = proj.shape[1] // LANES - 1
    row = lambda c: (lambda b, i: (b * nt + i, c))
    fix2 = lambda b, i: (0, 0)
    out_shapes = (jax.ShapeDtypeStruct((bsz * t, HALF), F32),
                  jax.ShapeDtypeStruct((bsz, B_HEADS, B_DK, B_DK), F32),
                  jax.ShapeDtypeStruct((bsz, B_HEADS, 1, B_DK), F32),
                  jax.ShapeDtypeStruct((bsz, 1, B_HEADS), F32))
    return pl.pallas_call(
        functools.partial(_mlstm_kernel, ln=ln, n_pad=n_pad),
        grid=(bsz, nt),
        in_specs=[pl.BlockSpec((ln, HALF), row(3)), pl.BlockSpec((ln, HALF), row(4)),
                  pl.BlockSpec((ln, HALF), row(5)), pl.BlockSpec((ln, HALF), row(6)),
                  pl.BlockSpec((ln, LANES), row(gw)),
                  pl.BlockSpec((1, 2 * B_HEADS, ln), lambda b, i: (b * nt + i, 0, 0)),
                  pl.BlockSpec((1, B_HEADS, B_DK, B_DK), lambda b, i: (b, 0, 0, 0)),
                  pl.BlockSpec((1, B_HEADS, 1, B_DK), lambda b, i: (b, 0, 0, 0)),
                  pl.BlockSpec((1, 1, B_HEADS), lambda b, i: (b, 0, 0)),
                  pl.BlockSpec((1, B_CONV - 1, 2 * HALF), lambda b, i: (b, 0, 0)),
                  pl.BlockSpec((B_CONV, 2 * HALF), fix2), pl.BlockSpec((1, 2 * HALF), fix2),
                  pl.BlockSpec((1, LANES), fix2), pl.BlockSpec((2 * B_HEADS, 1), fix2),
                  pl.BlockSpec((1, HALF), fix2)],
        out_specs=(pl.BlockSpec((ln, HALF), lambda b, i: (b * nt + i, 0)),
                   pl.BlockSpec((1, B_HEADS, B_DK, B_DK), lambda b, i: (b, 0, 0, 0)),
                   pl.BlockSpec((1, B_HEADS, 1, B_DK), lambda b, i: (b, 0, 0, 0)),
                   pl.BlockSpec((1, 1, B_HEADS), lambda b, i: (b, 0, 0))),
        out_shape=out_shapes,
        scratch_shapes=[pltpu.VMEM((ln + 8, 2 * HALF), F32),
                        pltpu.VMEM((B_HEADS, B_DK, B_DK), F32),
                        pltpu.VMEM((B_HEADS, 1, B_DK), F32),
                        pltpu.VMEM((B_HEADS, 1, 1), F32)],
        compiler_params=_cparams(("parallel", "arbitrary")),
        name="mlstm",
    )(proj, proj, proj, proj, proj, gates_r, c0, n0, m0, conv0, cw, cb,
      jnp.pad(gb, (0, LANES - 2 * B_HEADS))[None], gb[:, None], bnorm)


def _s5_param_kernel(lr_ref, li_ref, ldt_ref, bre_ref, bim_ref, pwr_ref, pwi_ref, amr_ref, ami_ref, bbr_ref, bbi_ref):
    lr = lr_ref[...]
    li = li_ref[...]
    dt = jnp.exp(ldt_ref[...])
    mag = jnp.exp(lr * dt)
    ar = mag * jnp.cos(li * dt)
    ai = mag * jnp.sin(li * dt)
    den = lr * lr + li * li
    cr = ((ar - 1.0) * lr + ai * li) / den
    ci = (ai * lr - (ar - 1.0) * li) / den
    sub = lax.broadcasted_iota(I32, pwr_ref.shape, 0)
    pr, pi = ar, ai
    pwr = jnp.zeros(pwr_ref.shape, F32)
    pwi = jnp.zeros(pwr_ref.shape, F32)
    for t in range(S5_GRP):
        pwr = jnp.where(sub == t, pr, pwr)
        pwi = jnp.where(sub == t, pi, pwi)
        if t + 1 in S5_SHIFTS:
            k = S5_SHIFTS.index(t + 1)
            amr_ref[k] = jnp.where(sub >= t + 1, pr, 0.0)
            ami_ref[k] = jnp.where(sub >= t + 1, pi, 0.0)
        pr, pi = pr * ar - pi * ai, pr * ai + pi * ar
    pwr_ref[...] = pwr
    pwi_ref[...] = pwi
    nr, nc = bre_ref.shape
    same = (lax.broadcasted_iota(I32, (nr, nc), 0) // C_GROUP) == (lax.broadcasted_iota(I32, (nr, nc), 1) // C_STATE)
    br = bre_ref[...]
    bi = bim_ref[...]
    bbr_ref[...] = jnp.where(same, cr * br - ci * bi, 0.0).astype(BF16)
    bbi_ref[...] = jnp.where(same, cr * bi + ci * br, 0.0).astype(BF16)


def s5_params(lr, li, ldt, bre_rep, bim_rep):
    ns = lr.shape[1]
    vm = pl.BlockSpec(memory_space=pltpu.VMEM)
    return pl.pallas_call(
        _s5_param_kernel,
        in_specs=[vm] * 5,
        out_specs=(vm,) * 6,
        out_shape=(jax.ShapeDtypeStruct((S5_GRP, ns), F32), jax.ShapeDtypeStruct((S5_GRP, ns), F32),
                   jax.ShapeDtypeStruct((len(S5_SHIFTS), S5_GRP, ns), F32),
                   jax.ShapeDtypeStruct((len(S5_SHIFTS), S5_GRP, ns), F32),
                   jax.ShapeDtypeStruct(bre_rep.shape, BF16), jax.ShapeDtypeStruct(bre_rep.shape, BF16)),
        compiler_params=pltpu.CompilerParams(vmem_limit_bytes=VMEM_LIMIT),
        name="s5_params",
    )(lr, li, ldt, bre_rep, bim_rep)


def _s5_kernel(u_ref, pwr_ref, pwi_ref, amr_ref, ami_ref, bbr_ref, bbi_ref, cre_ref, cim_ref, d_ref, wg_ref, x0r_ref, x0i_ref,
               o_ref, xr_out, xi_out, sr_scr, si_scr, bur, bui, xra, xia, *, tb_len, n_pad):
    tb = pl.program_id(1)
    nt = pl.num_programs(1)

    @pl.when(tb == 0)
    def _():
        sr_scr[...] = x0r_ref[0]
        si_scr[...] = x0i_ref[0]

    u = u_ref[...]
    ub = u.astype(BF16)
    st_w = S5_UT * C_STATE // C_GROUP
    for c in range(u.shape[1] // S5_UT):
        uc, sc = slice(c * S5_UT, (c + 1) * S5_UT), slice(c * st_w, (c + 1) * st_w)
        bur[:, sc] = _dot(ub[:, uc], bbr_ref[uc, sc])
        bui[:, sc] = _dot(ub[:, uc], bbi_ref[uc, sc])
    def body(grp, carry):
        xr_prev, xi_prev = carry
        base = pl.multiple_of(grp * S5_GRP, S5_GRP)
        xr = bur[pl.ds(base, S5_GRP), :]
        xi = bui[pl.ds(base, S5_GRP), :]
        for k, d in enumerate(S5_SHIFTS):
            sr = pltpu.roll(xr, d, 0)
            si = pltpu.roll(xi, d, 0)
            ar, ai = amr_ref[k], ami_ref[k]
            xr, xi = xr + ar * sr - ai * si, xi + ar * si + ai * sr
        pr, pi = pwr_ref[...], pwi_ref[...]
        xr, xi = xr + pr * xr_prev - pi * xi_prev, xi + pr * xi_prev + pi * xr_prev
        xra[pl.ds(base, S5_GRP), :] = xr
        xia[pl.ds(base, S5_GRP), :] = xi
        return xr[S5_GRP - 1:S5_GRP], xi[S5_GRP - 1:S5_GRP]

    xr, xi = lax.fori_loop(0, tb_len // S5_GRP, body, (sr_scr[...], si_scr[...]))
    sr_scr[...] = xr
    si_scr[...] = xi

    ys = []
    for c in range(u.shape[1] // S5_YT):
        yc = slice(c * S5_YT, (c + 1) * S5_YT)
        sc = slice(c * S5_YT * C_STATE // C_GROUP, (c + 1) * S5_YT * C_STATE // C_GROUP)
        ys.append(_dot(xra[:, sc].astype(BF16), cre_ref[sc, yc]) - _dot(xia[:, sc].astype(BF16), cim_ref[sc, yc]))
    y = jnp.concatenate(ys, axis=1) + d_ref[...] * u
    yg = 0.5 * y * (1.0 + jnp.tanh(math.sqrt(2.0 / math.pi) * (y + 0.044715 * (y * y * y))))
    oc = yg * _sigmoid(_dot(yg.astype(BF16), wg_ref[...]))
    rows = tb * tb_len + lax.broadcasted_iota(I32, (tb_len, 1), 0)
    o_ref[...] = jnp.where(rows >= n_pad, oc, 0.0)

    @pl.when(tb == nt - 1)
    def _():
        xr_out[0] = xr
        xi_out[0] = xi


def s5(proj, pwr, pwi, amr, ami, bbr, bbi, cre, cim, dskip, wglu, x0r, x0i, *, bsz, t, tb_len, n_pad):
    nt = t // tb_len
    ns = pwr.shape[1]
    fix2 = lambda b, i: (0, 0)
    st = pl.BlockSpec((1, 1, ns), lambda b, i: (b, 0, 0))
    return pl.pallas_call(
        functools.partial(_s5_kernel, tb_len=tb_len, n_pad=n_pad),
        grid=(bsz, nt),
        in_specs=[pl.BlockSpec((tb_len, HALF), lambda b, i: (b * nt + i, 0)),
                  pl.BlockSpec((S5_GRP, ns), fix2), pl.BlockSpec((S5_GRP, ns), fix2),
                  pl.BlockSpec(amr.shape, lambda b, i: (0, 0, 0)), pl.BlockSpec(amr.shape, lambda b, i: (0, 0, 0)),
                  pl.BlockSpec((HALF, ns), fix2), pl.BlockSpec((HALF, ns), fix2),
                  pl.BlockSpec((ns, HALF), fix2), pl.BlockSpec((ns, HALF), fix2),
                  pl.BlockSpec((1, HALF), fix2), pl.BlockSpec((HALF, HALF), fix2), st, st],
        out_specs=(pl.BlockSpec((tb_len, HALF), lambda b, i: (b * nt + i, 0)), st, st),
        out_shape=(jax.ShapeDtypeStruct((bsz * t, HALF), F32),
                   jax.ShapeDtypeStruct((bsz, 1, ns), F32), jax.ShapeDtypeStruct((bsz, 1, ns), F32)),
        scratch_shapes=[pltpu.VMEM((1, ns), F32), pltpu.VMEM((1, ns), F32),
                        pltpu.VMEM((tb_len, ns), F32), pltpu.VMEM((tb_len, ns), F32),
                        pltpu.VMEM((tb_len, ns), F32), pltpu.VMEM((tb_len, ns), F32)],
        compiler_params=_cparams(("parallel", "arbitrary")),
        name="s5",
    )(proj, pwr, pwi, amr, ami, bbr, bbi, cre, cim, dskip, wglu, x0r, x0i)


def _rwkv_kernel(r_ref, k_ref, v_ref, lo_ref, sh0_ref, s0_ref,
                 mu_ref, w0_ref, ww2_ref, a0_ref, wa2_ref, wg2_ref, kk_ref, ka_ref, rk_ref, lng_ref, lnb_ref,
                 o_ref, s_out,
                 xbuf, s_scr, w_scr, nkk_scr, b_scr, k_scr, r_scr, v_scr, c_scr, be_scr, ga_scr, y_scr,
                 *, tb_len, n_pad):
    tb = pl.program_id(1)
    nt = pl.num_programs(1)
    npair = D_HEADS // 2
    ncol = xbuf.shape[1]

    @pl.when(tb == 0)
    def _():
        xbuf[7:8, :] = sh0_ref[0]
        s_scr[...] = s0_ref[0]

    xbuf[8:8 + tb_len, 0:D_WIDTH] = r_ref[...]
    xbuf[8:8 + tb_len, D_WIDTH:2 * D_WIDTH] = k_ref[...]
    xbuf[8:8 + tb_len, 2 * D_WIDTH:3 * D_WIDTH] = v_ref[...]
    xbuf[8:8 + tb_len, 3 * D_WIDTH:ncol] = lo_ref[...]
    cur = xbuf[8:8 + tb_len, :]
    prev = xbuf[7:7 + tb_len, :]
    xbuf[7:8, :] = xbuf[7 + tb_len:8 + tb_len, :]
    xm = cur + mu_ref[...] * (prev - cur)
    r = xm[:, 0:D_WIDTH]
    k = xm[:, D_WIDTH:2 * D_WIDTH]
    v = xm[:, 2 * D_WIDTH:3 * D_WIDTH]
    c0 = 3 * D_WIDTH
    wlo = xm[:, c0:c0 + 64]
    alo = xm[:, c0 + 64:c0 + 128]
    glo = xm[:, c0 + 128:c0 + 256]

    w_raw = w0_ref[...] + _dot(jnp.tanh(wlo).astype(BF16), ww2_ref[...])
    decay = jnp.exp(-jnp.exp(-_softplus(-w_raw) - 0.5))
    a = _sigmoid(a0_ref[...] + _dot(alo.astype(BF16), wa2_ref[...]))
    g = _dot(_sigmoid(glo).astype(BF16), wg2_ref[...])

    ones_h = _block_ones(2 * LANES, D_HEAD)
    kk = k * kk_ref[...]
    kk = kk / jnp.maximum(jnp.sqrt(_segsum(kk * kk, ones_h)), 1e-12)
    k2 = k * (1.0 + (a - 1.0) * ka_ref[...])

    nkk = -kk
    bb = kk * a
    w_scr[...] = decay
    nkk_scr[...] = nkk
    b_scr[...] = bb
    k_scr[...] = k2
    r_scr[...] = r
    v_scr[...] = v
    c_scr[...] = pltpu.roll(decay, 1, 0) * nkk
    be_scr[...] = _segsum(pltpu.roll(bb, 1, 0) * nkk, ones_h)
    ga_scr[...] = _segsum(pltpu.roll(k2, 1, 0) * nkk, ones_h)

    eye2 = jnp.where((lax.broadcasted_iota(I32, (D_HEAD, LANES), 1) % D_HEAD)
                     == lax.broadcasted_iota(I32, (D_HEAD, LANES), 0), 1.0, 0.0)

    sub = lax.broadcasted_iota(I32, (RWKV_GRP, LANES), 0)
    pairs = range(npair)
    nrow = npair * D_HEAD
    ri = lax.broadcasted_iota(I32, (2 * LANES, 2 * LANES), 0)
    ci = lax.broadcasted_iota(I32, (2 * LANES, 2 * LANES), 1)
    same_head = (ri % LANES) // D_HEAD == (ci % LANES) // D_HEAD
    ones_s = jnp.where(jnp.logical_and(same_head, ri // LANES >= ci // LANES), 1.0, 0.0).astype(BF16)

    def hilo(parts):
        x = jnp.concatenate(parts, axis=0)
        hi = x.astype(BF16)
        return jnp.concatenate([hi, (x - hi.astype(F32)).astype(BF16)], axis=1)

    def per_pair(res):
        return [res[p * D_HEAD:(p + 1) * D_HEAD] for p in pairs]

    def y_rows(yt, ycol, i):
        return [jnp.where(sub == i, jnp.sum(ycol[p] * eye2, axis=0, keepdims=True), yt[p]) for p in pairs]

    def body(grp, state):
        base = pl.multiple_of(grp * RWKV_GRP, RWKV_GRP)
        tile = lambda scr: [scr[pl.ds(base, RWKV_GRP), p * LANES:(p + 1) * LANES] for p in pairs]
        nkk_t, c_t, be_t, ga_t = tile(nkk_scr), tile(c_scr), tile(be_scr), tile(ga_scr)
        w_t, b_t, k_t, r_t, v_t = tile(w_scr), tile(b_scr), tile(k_scr), tile(r_scr), tile(v_scr)
        vh = [x.astype(BF16).astype(F32) for x in v_t]
        vl = [x - h for x, h in zip(v_t, vh)]

        def vcol_lhs(i):
            return jnp.concatenate([jnp.concatenate([vh[p][i:i + 1] * eye2 for p in pairs], axis=0).astype(BF16),
                                    jnp.concatenate([vl[p][i:i + 1] * eye2 for p in pairs], axis=0).astype(BF16)],
                                   axis=1)

        def segsums(lhs):
            res = _dot(jnp.concatenate(lhs, axis=0), ones_s)
            return [res[i * nrow:(i + 1) * nrow] for i in range(len(lhs))]

        def read_out(yt, res, i):
            second = res[:, LANES:2 * LANES]
            yt = y_rows(yt, per_pair(res[:, 0:LANES] - second), i)
            return y_rows(yt, per_pair(second), i + 1)

        yt = [jnp.zeros((RWKV_GRP, LANES), F32) for _ in pairs]
        sp = list(state)
        vc = [per_pair(r[:, 0:LANES]) for r in segsums([vcol_lhs(0), vcol_lhs(1)])]
        yprods = None
        for t0 in range(0, RWKV_GRP, 2):
            t1 = t0 + 1
            r0, r1 = slice(t0, t0 + 1), slice(t1, t1 + 1)
            more = t0 + 2 < RWKV_GRP
            lhs = [hilo([sp[p] * nkk_t[p][r0] for p in pairs]), hilo([sp[p] * c_t[p][r1] for p in pairs])]
            if more:
                lhs += [vcol_lhs(t0 + 2), vcol_lhs(t0 + 3)]
            if yprods is not None:
                lhs.append(yprods)
            res = segsums(lhs)
            sa0, tmp = per_pair(res[0][:, 0:LANES]), per_pair(res[1][:, 0:LANES])
            if yprods is not None:
                yt = read_out(yt, res[-1], t0 - 2)
            s0 = [sp[p] * w_t[p][r0] + sa0[p] * b_t[p][r0] + vc[0][p] * k_t[p][r0] for p in pairs]
            sa1 = [tmp[p] + sa0[p] * be_t[p][r1] + vc[0][p] * ga_t[p][r1] for p in pairs]
            sp = [s0[p] * w_t[p][r1] + sa1[p] * b_t[p][r1] + vc[1][p] * k_t[p][r1] for p in pairs]
            yprods = jnp.concatenate(
                [jnp.concatenate([s0[p] * r_t[p][r0] for p in pairs], axis=0).astype(BF16),
                 jnp.concatenate([sp[p] * r_t[p][r1] for p in pairs], axis=0).astype(BF16)], axis=1)
            if more:
                vc = [per_pair(res[2][:, 0:LANES]), per_pair(res[3][:, 0:LANES])]
        yt = read_out(yt, segsums([yprods])[0], RWKV_GRP - 2)
        for p in pairs:
            y_scr[pl.ds(base, RWKV_GRP), p * LANES:(p + 1) * LANES] = yt[p]
        return tuple(sp)

    state = lax.fori_loop(0, tb_len // RWKV_GRP, body, tuple(s_scr[p] for p in pairs))
    for p in pairs:
        s_scr[p] = state[p]

    y = y_scr[...]
    inv = 1.0 / D_HEAD
    mean = _segsum(y, ones_h) * inv
    yc = y - mean
    var = _segsum(yc * yc, ones_h) * inv
    y = yc * lax.rsqrt(var + D_GN_EPS) * lng_ref[...] + lnb_ref[...]
    y = y + _segsum(r * k2 * rk_ref[...], ones_h) * v
    rows = tb * tb_len + lax.broadcasted_iota(I32, (tb_len, 1), 0)
    o_ref[...] = jnp.where(rows >= n_pad, y * g, 0.0)

    @pl.when(tb == nt - 1)
    def _():
        s_out[0] = s_scr[...]


def rwkv(proj, sh0, s0, mu, w0, ww2, a0, wa2, wg2, kkp, kap, rkp, lng, lnb, *, bsz, t, tb_len, n_pad):
    nt = t // tb_len
    npair = D_HEADS // 2
    ncol = mu.shape[1]
    nlo = ncol - 3 * D_WIDTH
    row = lambda c: (lambda b, i: (b * nt + i, c))
    fix2 = lambda b, i: (0, 0)
    vec = pl.BlockSpec((1, D_WIDTH), fix2)
    st = pl.BlockSpec((1, npair, D_HEAD, LANES), lambda b, i: (b, 0, 0, 0))
    big = lambda: pltpu.VMEM((tb_len, D_WIDTH), F32)
    return pl.pallas_call(
        functools.partial(_rwkv_kernel, tb_len=tb_len, n_pad=n_pad),
        grid=(bsz, nt),
        in_specs=[pl.BlockSpec((tb_len, D_WIDTH), row(1)), pl.BlockSpec((tb_len, D_WIDTH), row(2)),
                  pl.BlockSpec((tb_len, D_WIDTH), row(3)),
                  pl.BlockSpec((tb_len, nlo), row(4 * D_WIDTH // nlo)),
                  pl.BlockSpec((1, 1, ncol), lambda b, i: (b, 0, 0)), st,
                  pl.BlockSpec((1, ncol), fix2), vec,
                  pl.BlockSpec(ww2.shape, fix2), vec, pl.BlockSpec(wa2.shape, fix2), pl.BlockSpec(wg2.shape, fix2),
                  vec, vec, vec, vec, vec],
        out_specs=(pl.BlockSpec((tb_len, D_WIDTH), lambda b, i: (b * nt + i, 0)), st),
        out_shape=(jax.ShapeDtypeStruct((bsz * t, D_WIDTH), F32),
                   jax.ShapeDtypeStruct((bsz, npair, D_HEAD, LANES), F32)),
        scratch_shapes=[pltpu.VMEM((tb_len + 8, ncol), F32), pltpu.VMEM((npair, D_HEAD, LANES), F32),
                        big(), big(), big(), big(), big(), big(), big(), big(), big(), big()],
        compiler_params=_cparams(("parallel", "arbitrary")),
        name="rwkv7",
    )(proj, proj, proj, proj, sh0, s0, mu, w0, ww2, a0, wa2, wg2, kkp, kap, rkp, lng, lnb)


def _kv_pack_kernel(*refs, n_layers, lead_rows):
    e = pl.program_id(0)
    ins, (ko_ref, vo_ref) = refs[:4 * n_layers], refs[4 * n_layers:]
    tm = ins[0].shape[0]
    off = lead_rows % tm
    for layer in range(n_layers):
        @pl.when(e == layer)
        def _(layer=layer):
            for (a_ref, b_ref), o_ref in ((ins[4 * layer:4 * layer + 2], ko_ref), (ins[4 * layer + 2:4 * layer + 4], vo_ref)):
                for h in range(A_HEADS):
                    cols = slice(h * A_DV, (h + 1) * A_DV)
                    o_ref[0, 0, 0:tm - off, h, :] = a_ref[off:tm, cols]
                    o_ref[0, 0, tm - off:tm, h, :] = b_ref[0:off, cols]


def kv_pack(projs, n_rows, lead_rows, tm=ROW_TILE):
    n_layers = len(projs)
    nb_in = projs[0].shape[0] // tm
    first = lead_rows // tm
    assert lead_rows % 8 == 0
    nb = -(-n_rows // tm)
    in_specs, args = [], []
    for layer, pr in enumerate(projs):
        for col in (1, 2):
            cur = lambda e, b, layer=layer, col=col: (jnp.where(e == layer, first + b, 0), col)
            nxt = lambda e, b, layer=layer, col=col: (jnp.where(e == layer, jnp.minimum(first + b + 1, nb_in - 1), 0), col)
            in_specs += [pl.BlockSpec((tm, HALF), cur), pl.BlockSpec((tm, HALF), nxt)]
            args += [pr, pr]
    out_spec = pl.BlockSpec((1, 1, tm, A_HEADS, A_DV), lambda e, b: (e, 0, b, 0, 0))
    shape = jax.ShapeDtypeStruct((n_layers, 1, n_rows, A_HEADS, A_DV), F32)
    return pl.pallas_call(
        functools.partial(_kv_pack_kernel, n_layers=n_layers, lead_rows=lead_rows),
        grid=(n_layers, nb),
        in_specs=in_specs,
        out_specs=(out_spec, out_spec),
        out_shape=(shape, shape),
        compiler_params=_cparams(("arbitrary", "arbitrary")),
        name="kv_pack",
    )(*args)


def _pad_cols(w, mult=LANES):
    n = w.shape[1]
    return jnp.pad(w, ((0, 0), (0, (-n) % mult)))


def _pairs_from_heads(s):
    b = s.shape[0]
    return s.reshape(b, D_HEADS // 2, 2, D_HEAD, D_HEAD).transpose(0, 1, 3, 2, 4).reshape(b, D_HEADS // 2, D_HEAD, LANES)


def _heads_from_pairs(s):
    b = s.shape[0]
    return s.reshape(b, D_HEADS // 2, D_HEAD, 2, D_HEAD).transpose(0, 1, 3, 2, 4).reshape(b, D_HEADS, D_HEAD, D_HEAD)


def _trunk(x, p, *, bsz, t, seq_blk, n_pad, attn_fn, st, drop_rows=0):
    depth = p['norm_mix'].shape[0]
    new = {k: [] for k in ('a_k', 'a_v', 'b_c', 'b_n', 'b_m', 'b_conv', 'c_re', 'c_im', 'd_s', 'd_shift', 'proj_even')}
    nt = t // seq_blk
    y = None
    for layer in range(depth):
        g_mix = p['norm_mix'][layer][None]
        if layer % 2 == 0:
            e = layer // 2
            proj = rms_matmul(x, g_mix, p['ev_w_in'][e])
            lam_init = 0.8 - 0.6 * math.exp(-0.3 * layer)
            oa = attn_fn(proj, e, lam_init)
            gcols = proj[:, 7 * HALF:7 * HALF + 2 * B_HEADS]
            gates_r = gcols.reshape(bsz * nt, seq_blk, 2 * B_HEADS).transpose(0, 2, 1)
            hb, bc, bn, bm = mlstm(proj, gates_r, st['b_c'][e], st['b_n'][e][:, :, None, :], st['b_m'][e][:, None, :],
                                   st['b_conv'][e], p['b_conv_w'][e], p['b_conv_b'][e][None],
                                   p['b_ig_bias'][e], p['b_fg_bias'][e], p['b_norm'][e][None],
                                   bsz=bsz, t=t, ln=seq_blk, n_pad=n_pad)
            p3 = proj.reshape(bsz, t, -1)
            new['a_k'].append(p3[:, :, HALF:2 * HALF])
            new['a_v'].append(p3[:, :, 2 * HALF:3 * HALF])
            new['proj_even'].append(proj)
            new['b_c'].append(bc)
            new['b_n'].append(bn[:, :, 0, :])
            new['b_m'].append(bm[:, 0, :])
            new['b_conv'].append(p3[:, t - (B_CONV - 1):, 3 * HALF:5 * HALF])
            mix_in = (oa, hb, p['ev_w_out'][e][:HALF], p['ev_w_out'][e][HALF:])
        else:
            o = layer // 2
            proj = rms_matmul(x, g_mix, p['od_w_in'][o])
            sp = p['s5'][o]
            oc, cre, cim = s5(proj, sp['pwr'], sp['pwi'], sp['amr'], sp['ami'], sp['bbr'], sp['bbi'], sp['cre'], sp['cim'],
                              p['c_d'][o][None], p['c_w_glu'][o],
                              st['c_re'][o].reshape(bsz, 1, -1), st['c_im'][o].reshape(bsz, 1, -1),
                              bsz=bsz, t=t, tb_len=seq_blk, n_pad=n_pad)
            od, ds = rwkv(proj, st['d_shift'][o], _pairs_from_heads(st['d_s'][o]),
                          p['d_mu'][o][None], p['d_w0'][o][None], p['d_w_w2'][o], p['d_a0'][o][None],
                          p['d_w_a2'][o], p['d_w_g2'][o], p['d_k_k'][o][None], p['d_k_a'][o][None],
                          p['d_r_k'][o][None], p['d_ln_g'][o][None], p['d_ln_b'][o][None],
                          bsz=bsz, t=t, tb_len=seq_blk, n_pad=n_pad)
            new['c_re'].append(cre.reshape(bsz, C_GROUPS, C_STATE))
            new['c_im'].append(cim.reshape(bsz, C_GROUPS, C_STATE))
            new['d_s'].append(_heads_from_pairs(ds))
            new['d_shift'].append(proj.reshape(bsz, t, -1)[:, -1:, HALF:])
            mix_in = (oc, od, p['od_w_out'][o][:HALF], p['od_w_out'][o][HALF:])
        last = layer == depth - 1
        x = ffn(x, *mix_in, p['norm_ffn'][layer][None], p['ffn_w1'][layer], p['ffn_w3'][layer], p['ffn_w2'][layer],
                gf=p['norm_final'][None] if last else None, drop_rows=drop_rows if last else 0)
    return x, new


def kernel(x_prompt, x_sample, cache_a_k, cache_a_v, state_b_c, state_b_n, state_b_m, state_b_conv, state_c_re, state_c_im, state_d_s, state_d_shift, meta_tokens, rel_bias, norm_mix, norm_ffn, norm_final, ev_w_in, ev_w_out, a_lq1, a_lk1, a_lq2, a_lk2, a_subln, b_conv_w, b_conv_b, b_ig_bias, b_fg_bias, b_norm, od_w_in, od_w_out, c_lam_re, c_lam_im, c_log_dt, c_b_re, c_b_im, c_c_re, c_c_im, c_d, c_w_glu, d_mu, d_w0, d_w_w2, d_a0, d_w_a2, d_w_g2, d_k_k, d_k_a, d_r_k, d_ln_g, d_ln_b, ffn_w1, ffn_w3, ffn_w2):
    bp, sp_len, dm = x_prompt.shape
    bs, ts, _ = x_sample.shape
    n_even, n_odd = ev_w_in.shape[0], od_w_in.shape[0]
    assert bp == 1 and sp_len % CHUNK == 0 and ts % RWKV_GRP == 0 and ts >= B_CONV - 1
    dt = x_prompt.dtype

    ns = C_GROUPS * C_STATE
    s5p = []
    for o in range(n_odd):
        bre = jnp.tile(c_b_re[o].transpose(2, 0, 1).reshape(C_GROUP, ns), (C_GROUPS, 1))
        bim = jnp.tile(c_b_im[o].transpose(2, 0, 1).reshape(C_GROUP, ns), (C_GROUPS, 1))
        pwr, pwi, amr, ami, bbr, bbi = s5_params(c_lam_re[o].reshape(1, ns), c_lam_im[o].reshape(1, ns),
                                     jnp.repeat(c_log_dt[o], C_STATE)[None], bre, bim)
        eye = jnp.eye(C_GROUPS, dtype=F32)
        blk = lambda c: (eye[:, None, :, None] * c.transpose(0, 2, 1)[:, :, None, :]).reshape(ns, C_WIDTH).astype(BF16)
        s5p.append(dict(pwr=pwr, pwi=pwi, amr=amr, ami=ami, bbr=bbr, bbi=bbi, cre=blk(c_c_re[o]), cim=blk(c_c_im[o])))
    p = dict(norm_mix=norm_mix, norm_ffn=norm_ffn, norm_final=norm_final,
             ev_w_in=[_pad_cols(ev_w_in[e]).astype(BF16) for e in range(n_even)],
             ev_w_out=ev_w_out.astype(BF16),
             od_w_in=[od_w_in[o].astype(BF16) for o in range(n_odd)], od_w_out=od_w_out.astype(BF16),
             b_conv_w=b_conv_w, b_conv_b=b_conv_b, b_ig_bias=b_ig_bias, b_fg_bias=b_fg_bias, b_norm=b_norm,
             s5=s5p, c_d=c_d, c_w_glu=c_w_glu.astype(BF16),
             d_mu=d_mu, d_w0=d_w0, d_w_w2=d_w_w2.astype(BF16), d_a0=d_a0, d_w_a2=d_w_a2.astype(BF16),
             d_w_g2=d_w_g2.astype(BF16), d_k_k=d_k_k, d_k_a=d_k_a, d_r_k=d_r_k, d_ln_g=d_ln_g, d_ln_b=d_ln_b,
             ffn_w1=ffn_w1.astype(BF16), ffn_w3=ffn_w3.astype(BF16), ffn_w2=ffn_w2.astype(BF16))
    lam_vecs = lambda e: (a_lq1[e][None], a_lk1[e][None], a_lq2[e][None], a_lk2[e][None], a_subln[e][None])

    tp = -(-(sp_len + CHUNK) // ATT_BLK) * ATT_BLK
    n_pad = tp - sp_len - N_META
    xp = jnp.concatenate([jnp.zeros((n_pad, dm), dt), meta_tokens.astype(dt), x_prompt[0]], axis=0)
    bias2 = jnp.stack([bias_tile(rel_bias, ATT_BLK, ATT_BLK, 0, causal=True, scale=LOG2E),
                       bias_tile(rel_bias, ATT_BLK, ATT_BLK, -ATT_BLK, scale=LOG2E)], axis=0)

    def attn_p(proj, e, lam_init):
        return attn_prompt(proj, bias2, rel_bias, *lam_vecs(e), n_pad=n_pad, lam_init=lam_init)

    zeros = lambda *s: jnp.zeros(s, F32)
    st_p = dict(b_c=zeros(n_even, bp, B_HEADS, B_DK, B_DK), b_n=zeros(n_even, bp, B_HEADS, B_DK),
                b_m=zeros(n_even, bp, B_HEADS), b_conv=zeros(n_even, bp, B_CONV - 1, 2 * HALF),
                c_re=zeros(n_odd, bp, C_GROUPS, C_STATE), c_im=zeros(n_odd, bp, C_GROUPS, C_STATE),
                d_s=zeros(n_odd, bp, D_HEADS, D_HEAD, D_HEAD), d_shift=zeros(n_odd, bp, 1, d_mu.shape[1]))
    lead = n_pad + N_META
    drop = lead if lead % ROW_TILE == 0 else 0
    y_p, new_p = _trunk(xp, p, bsz=bp, t=tp, seq_blk=SEQ_BLK, n_pad=n_pad, attn_fn=attn_p, st=st_p, drop_rows=drop)

    tk = cache_a_k.shape[2]
    past_len = tk - N_META
    cid = lambda pos: np.where(pos < N_META, 0, 1 + (pos - N_META) // CHUNK)
    q_pos = N_META + past_len + np.arange(ts)
    k_pos = np.arange(tk + ts)
    assert (cid(k_pos)[None, :] <= cid(q_pos)[:, None]).all()
    bias_past = bias_tile(rel_bias, ts, tk, -(N_META + past_len))
    bias_new = bias_tile(rel_bias, ts, ts, 0)

    def attn_s(proj, e, lam_init):
        return attn_sample(proj, cache_a_k, cache_a_v, e, bias_past, bias_new, *lam_vecs(e), ts=ts, lam_init=lam_init)

    st_s = dict(b_c=state_b_c, b_n=state_b_n, b_m=state_b_m, b_conv=state_b_conv, c_re=state_c_re,
                c_im=state_c_im, d_s=state_d_s, d_shift=state_d_shift)
    y_s, new_s = _trunk(x_sample.reshape(bs * ts, dm), p, bsz=bs, t=ts, seq_blk=ts, n_pad=0, attn_fn=attn_s, st=st_s)

    def pack(new, bsz, t, drop):
        if drop:
            ak, av = kv_pack(new['proj_even'], t - drop, drop)
        else:
            kv = lambda a: a.reshape(bsz, t, A_HEADS, A_DV)
            ak, av = jnp.stack([kv(a) for a in new['a_k']]), jnp.stack([kv(a) for a in new['a_v']])
        return (ak, av, jnp.stack(new['b_c']), jnp.stack(new['b_n']), jnp.stack(new['b_m']), jnp.stack(new['b_conv']),
                jnp.stack(new['c_re']), jnp.stack(new['c_im']), jnp.stack(new['d_s']), jnp.stack(new['d_shift']))

    out_p = pack(new_p, bp, tp, n_pad)
    out_s = pack(new_s, bs, ts, 0)
    return (y_p[lead - drop:][None], y_s.reshape(bs, ts, dm)) + out_p + out_s
```

```python
import functools
import math

import numpy as np
import jax
import jax.numpy as jnp
from jax import lax
from jax.experimental import pallas as pl
from jax.experimental.pallas import tpu as pltpu

F32 = jnp.float32
BF16 = jnp.bfloat16
I32 = jnp.int32

CHUNK = 64
N_META = 16
EPS = 1e-6
A_HEADS = 4
A_DH = 64
A_DV = 128
N_BUCKETS = 32
MAX_DIST = 128
B_HEADS = 4
B_DK = 128
B_CONV = 4
C_GROUP = 16
C_GROUPS = 32
C_STATE = 64
C_WIDTH = C_GROUP * C_GROUPS
D_HEAD = 64
D_HEADS = 8
D_WIDTH = D_HEAD * D_HEADS
D_GN_EPS = 64e-5
HALF = 512

NEG = -1e30
LOG2E = math.log2(math.e)
LANES = 128
VMEM_LIMIT = 56 * 1024 * 1024

ROW_TILE = 512
ATT_BLK = 512
ATT_STRIP = 64
SEQ_BLK = 128
S5_UT = 128
S5_YT = 256
S5_GRP = 8
S5_SHIFTS = (1, 2, 4)
GRP_MAX = 64
FFN_TH = 1408


def _cparams(sem):
    return pltpu.CompilerParams(dimension_semantics=sem, vmem_limit_bytes=VMEM_LIMIT)


def _dot(a, b):
    return jnp.dot(a, b, preferred_element_type=F32)


def _dot_nt(a, b):
    return lax.dot_general(a, b, (((1,), (1,)), ((), ())), preferred_element_type=F32)


def _dot_tn(a, b):
    return lax.dot_general(a, b, (((0,), (0,)), ((), ())), preferred_element_type=F32)


def _sigmoid(x):
    return 1.0 / (1.0 + jnp.exp(-x))


def _softplus(x):
    return jnp.maximum(x, 0.0) + jnp.log1p(jnp.exp(-jnp.abs(x)))


def _rms(x, g):
    return x * lax.rsqrt(jnp.mean(x * x, axis=-1, keepdims=True) + EPS) * g


def _segsum(x, ones):
    n, w = x.shape[0], ones.shape[0]
    hi = x.astype(BF16)
    lo = (x - hi.astype(F32)).astype(BF16)
    outs = []
    for c in range(x.shape[1] // w):
        res = _dot(jnp.concatenate([hi[:, c * w:(c + 1) * w], lo[:, c * w:(c + 1) * w]], axis=0), ones)
        outs.append(res[:n] + res[n:])
    return jnp.concatenate(outs, axis=1)


def _block_ones(n, seg):
    r = lax.broadcasted_iota(I32, (n, n), 0) // seg
    c = lax.broadcasted_iota(I32, (n, n), 1) // seg
    return jnp.where(r == c, 1.0, 0.0).astype(BF16)


def _rms_mm_kernel(x_ref, g_ref, w_ref, o_ref):
    h = _rms(x_ref[...], g_ref[...])
    o_ref[...] = _dot(h.astype(BF16), w_ref[...])


def rms_matmul(x, g, w, tm=ROW_TILE):
    m, d = x.shape
    tm = min(tm, m)
    n = w.shape[1]
    return pl.pallas_call(
        _rms_mm_kernel,
        grid=(m // tm,),
        in_specs=[pl.BlockSpec((tm, d), lambda i: (i, 0)),
                  pl.BlockSpec((1, d), lambda i: (0, 0)),
                  pl.BlockSpec((d, n), lambda i: (0, 0))],
        out_specs=pl.BlockSpec((tm, n), lambda i: (i, 0)),
        out_shape=jax.ShapeDtypeStruct((m, n), F32),
        compiler_params=_cparams(("parallel",)),
        name="rms_matmul",
    )(x, g, w)


def _ffn_kernel(x_ref, a_ref, b_ref, wa_ref, wb_ref, g_ref, gf_ref, w1_ref, w3_ref, w2_ref, o_ref, h_scr,
                *, final_norm):
    j = pl.program_id(1)

    @pl.when(j == 0)
    def _():
        mix = _dot(a_ref[...].astype(BF16), wa_ref[...]) + _dot(b_ref[...].astype(BF16), wb_ref[...])
        x = x_ref[...] + mix
        h_scr[...] = _rms(x, g_ref[...]).astype(BF16)
        o_ref[...] = x

    h = h_scr[...]
    a = _dot(h, w1_ref[...])
    b = _dot(h, w3_ref[...])
    u = (a * _sigmoid(a)) * b
    o_ref[...] += _dot(u.astype(BF16), w2_ref[...])

    if final_norm:
        @pl.when(j == pl.num_programs(1) - 1)
        def _():
            o_ref[...] = _rms(o_ref[...], gf_ref[...])


def ffn(x, a, b, wa, wb, g, w1, w3, w2, gf=None, drop_rows=0, tm=ROW_TILE, th=FFN_TH):
    m, d = x.shape
    k = a.shape[1]
    tm = min(tm, m)
    hid = w1.shape[1]
    final_norm = gf is not None
    if gf is None:
        gf = g
    skip, rem = divmod(drop_rows, tm)
    assert rem == 0
    return pl.pallas_call(
        functools.partial(_ffn_kernel, final_norm=final_norm),
        grid=(m // tm, hid // th),
        in_specs=[pl.BlockSpec((tm, d), lambda i, j: (i, 0)),
                  pl.BlockSpec((tm, k), lambda i, j: (i, 0)), pl.BlockSpec((tm, k), lambda i, j: (i, 0)),
                  pl.BlockSpec((k, d), lambda i, j: (0, 0)), pl.BlockSpec((k, d), lambda i, j: (0, 0)),
                  pl.BlockSpec((1, d), lambda i, j: (0, 0)),
                  pl.BlockSpec((1, d), lambda i, j: (0, 0)),
                  pl.BlockSpec((d, th), lambda i, j: (0, j)),
                  pl.BlockSpec((d, th), lambda i, j: (0, j)),
                  pl.BlockSpec((th, d), lambda i, j: (j, 0))],
        out_specs=pl.BlockSpec((tm, d), lambda i, j: (jnp.maximum(i - skip, 0), 0)),
        out_shape=jax.ShapeDtypeStruct((m - drop_rows, d), F32),
        scratch_shapes=[pltpu.VMEM((tm, d), BF16)],
        compiler_params=_cparams(("arbitrary" if skip else "parallel", "arbitrary")),
        name="ffn",
    )(x, a, b, wa, wb, g, gf, w1, w3, w2)


def _bias_kernel(rb_ref, o_ref, *, rel0, causal, scale):
    nq, nk = o_ref.shape[1], o_ref.shape[2]
    a = lax.broadcasted_iota(I32, (nq, nk), 0)
    b = lax.broadcasted_iota(I32, (nq, nk), 1)
    rel = b - a + rel0
    nb = N_BUCKETS // 2
    max_exact = nb // 2
    ret = jnp.where(rel > 0, nb, 0)
    n = jnp.abs(rel)
    nf = jnp.maximum(n, 1).astype(F32)
    large = max_exact + (jnp.log(nf / max_exact) / math.log(MAX_DIST / max_exact) * (nb - max_exact)).astype(I32)
    large = jnp.minimum(large, nb - 1)
    bucket = ret + jnp.where(n < max_exact, n, large)
    for h in range(A_HEADS):
        acc = jnp.zeros((nq, nk), F32)
        for bk in range(N_BUCKETS):
            acc = jnp.where(bucket == bk, rb_ref[bk, h] * scale, acc)
        if causal:
            acc = jnp.where((b // CHUNK) <= (a // CHUNK), acc, NEG)
        o_ref[h] = acc


def bias_tile(rel_bias, nq, nk, rel0, causal=False, scale=1.0):
    return pl.pallas_call(
        functools.partial(_bias_kernel, rel0=rel0, causal=causal, scale=scale),
        in_specs=[pl.BlockSpec(memory_space=pltpu.SMEM)],
        out_specs=pl.BlockSpec(memory_space=pltpu.VMEM),
        out_shape=jax.ShapeDtypeStruct((A_HEADS, nq, nk), F32),
        compiler_params=pltpu.CompilerParams(vmem_limit_bytes=VMEM_LIMIT),
        name="bias_tile",
    )(rel_bias)


def _lambda(lq1_ref, lk1_ref, lq2_ref, lk2_ref, lam_init):
    s1 = jnp.sum(lq1_ref[...] * lk1_ref[...], axis=-1, keepdims=True)
    s2 = jnp.sum(lq2_ref[...] * lk2_ref[...], axis=-1, keepdims=True)
    return jnp.exp(s1) - jnp.exp(s2) + lam_init


def _attn_finish(acc0, l0, acc1, l1, lam, g, lam_init):
    o = acc0 / l0 - lam * (acc1 / l1)
    return _rms(o, g) * (1.0 - lam_init)


def _attn_prompt_kernel(qi_ref, kj_ref, rb_ref, q_ref, k_ref, v_ref, bias_ref,
                        lq1_ref, lk1_ref, lq2_ref, lk2_ref, g_ref, o_ref,
                        m_scr, l_scr, acc_scr, s_scr, p_scr, al_scr, *, blk, n_pad, lam_init):
    s = pl.program_id(0)
    i = qi_ref[s]
    j = kj_ref[s]
    d = i - j
    nct = blk // LANES
    nstrip = blk // ATT_STRIP

    @pl.when(j == 0)
    def _():
        m_scr[...] = jnp.full(m_scr.shape, NEG, F32)
        l_scr[...] = jnp.zeros(l_scr.shape, F32)
        acc_scr[...] = jnp.zeros(acc_scr.shape, F32)

    def update(general):
        for h in range(A_HEADS):
            far = rb_ref[N_BUCKETS // 2 - 1, h] * LOG2E
            for mm in range(2):
                c0 = h * 2 * A_DH + mm * A_DH
                qb = (q_ref[:, c0:c0 + A_DH] * (A_DH ** -0.5 * LOG2E)).astype(BF16)
                s_scr[2 * h + mm] = _dot_nt(qb, k_ref[:, c0:c0 + A_DH].astype(BF16))
            for mm in range(2):
                idx = 2 * h + mm
                for r in range(nstrip):
                    rows = slice(r * ATT_STRIP, (r + 1) * ATT_STRIP)
                    tiles = []
                    for c in range(nct):
                        cols = slice(c * LANES, (c + 1) * LANES)
                        t = s_scr[idx, rows, cols]
                        if general:
                            kpos = j * blk + c * LANES + lax.broadcasted_iota(I32, (1, LANES), 1)
                            near = bias_ref[jnp.minimum(d, 1), h, rows, cols]
                            t = t + jnp.where(d < 2, near, far) + jnp.where(kpos < n_pad, NEG, 0.0)
                        tiles.append(t)
                    mx = functools.reduce(jnp.maximum, tiles)
                    m_cur = jnp.broadcast_to(jnp.max(mx, axis=-1, keepdims=True), (ATT_STRIP, LANES))
                    if not general:
                        m_cur = m_cur + far
                    m_old = m_scr[idx, rows, :]
                    m_new = jnp.maximum(m_old, m_cur)
                    alpha = jnp.exp2(m_old - m_new)
                    m_sub = m_new if general else m_new - far
                    ps = [jnp.exp2(t - m_sub) for t in tiles]
                    l_scr[idx, rows, :] = alpha * l_scr[idx, rows, :] + functools.reduce(jnp.add, ps)
                    m_scr[idx, rows, :] = m_new
                    al_scr[idx, rows, :] = alpha
                    for c in range(nct):
                        p_scr[h, mm * blk + r * ATT_STRIP:mm * blk + (r + 1) * ATT_STRIP,
                              c * LANES:(c + 1) * LANES] = ps[c].astype(BF16)
            pv = _dot(p_scr[h], v_ref[:, h * A_DV:(h + 1) * A_DV].astype(BF16))
            for mm in range(2):
                idx = 2 * h + mm
                acc_scr[idx] = al_scr[idx] * acc_scr[idx] + pv[mm * blk:(mm + 1) * blk]

    is_far = jnp.logical_and(d >= 2, j > 0)

    @pl.when(is_far)
    def _():
        update(False)

    @pl.when(jnp.logical_not(is_far))
    def _():
        update(True)

    @pl.when(d == 0)
    def _():
        lam = _lambda(lq1_ref, lk1_ref, lq2_ref, lk2_ref, lam_init)
        rows = i * blk + lax.broadcasted_iota(I32, (blk, 1), 0)
        valid = rows >= n_pad
        for h in range(A_HEADS):
            l0 = jnp.sum(l_scr[2 * h], axis=-1, keepdims=True)
            l1 = jnp.sum(l_scr[2 * h + 1], axis=-1, keepdims=True)
            y = _attn_finish(acc_scr[2 * h], l0, acc_scr[2 * h + 1], l1, lam, g_ref[...], lam_init)
            o_ref[:, h * A_DV:(h + 1) * A_DV] = jnp.where(valid, y, 0.0)


def attn_prompt(proj, bias2, rel_bias, lq1, lk1, lq2, lk2, g, *, n_pad, lam_init, blk=ATT_BLK):
    t = proj.shape[0]
    nb = t // blk
    qi = np.array([i for i in range(nb) for _ in range(i + 1)], np.int32)
    kj = np.array([j for i in range(nb) for j in range(i + 1)], np.int32)
    vec = lambda n: pl.BlockSpec((1, n), lambda s, qi, kj: (0, 0))
    grid_spec = pltpu.PrefetchScalarGridSpec(
        num_scalar_prefetch=2,
        grid=(len(qi),),
        in_specs=[pl.BlockSpec(memory_space=pltpu.SMEM),
                  pl.BlockSpec((blk, HALF), lambda s, qi, kj: (qi[s], 0)),
                  pl.BlockSpec((blk, HALF), lambda s, qi, kj: (kj[s], 1)),
                  pl.BlockSpec((blk, HALF), lambda s, qi, kj: (kj[s], 2)),
                  pl.BlockSpec((2, A_HEADS, blk, blk), lambda s, qi, kj: (0, 0, 0, 0)),
                  vec(A_DH), vec(A_DH), vec(A_DH), vec(A_DH), vec(A_DV)],
        out_specs=pl.BlockSpec((blk, HALF), lambda s, qi, kj: (qi[s], 0)),
        scratch_shapes=[pltpu.VMEM((2 * A_HEADS, blk, LANES), F32),
                        pltpu.VMEM((2 * A_HEADS, blk, LANES), F32),
                        pltpu.VMEM((2 * A_HEADS, blk, A_DV), F32),
                        pltpu.VMEM((2 * A_HEADS, blk, blk), F32),
                        pltpu.VMEM((A_HEADS, 2 * blk, blk), BF16),
                        pltpu.VMEM((2 * A_HEADS, blk, LANES), F32)],
    )
    assert n_pad <= blk and blk >= MAX_DIST
    return pl.pallas_call(
        functools.partial(_attn_prompt_kernel, blk=blk, n_pad=n_pad, lam_init=lam_init),
        grid_spec=grid_spec,
        out_shape=jax.ShapeDtypeStruct((t, HALF), F32),
        compiler_params=_cparams(("arbitrary",)),
        name="attn_prompt",
    )(jnp.asarray(qi), jnp.asarray(kj), rel_bias, proj, proj, proj, bias2, lq1, lk1, lq2, lk2, g)


def _attn_sample_kernel(q_ref, kn_ref, vn_ref, kc_ref, vc_ref, bp_ref, bn_ref,
                        lq1_ref, lk1_ref, lq2_ref, lk2_ref, g_ref, o_ref, *, lam_init):
    lam = _lambda(lq1_ref, lk1_ref, lq2_ref, lk2_ref, lam_init)
    k_all = pltpu.einshape("khd->hkd", kc_ref[0, 0])
    v_all = pltpu.einshape("khd->hkd", vc_ref[0, 0])
    for h in range(A_HEADS):
        vsl = slice(h * A_DV, (h + 1) * A_DV)
        vp = v_all[h].astype(BF16)
        kp = k_all[h].astype(BF16)
        vn = vn_ref[:, vsl].astype(BF16)
        outs = []
        for mm in range(2):
            c0 = h * 2 * A_DH + mm * A_DH
            qb = (q_ref[:, c0:c0 + A_DH] * (A_DH ** -0.5)).astype(BF16)
            sp = _dot_nt(qb, kp[:, mm * A_DH:(mm + 1) * A_DH]) + bp_ref[h]
            sn = _dot_nt(qb, kn_ref[:, c0:c0 + A_DH].astype(BF16)) + bn_ref[h]
            mx = jnp.maximum(jnp.max(sp, axis=-1, keepdims=True), jnp.max(sn, axis=-1, keepdims=True))
            pp = jnp.exp(sp - mx)
            pn = jnp.exp(sn - mx)
            l = jnp.sum(pp, axis=-1, keepdims=True) + jnp.sum(pn, axis=-1, keepdims=True)
            acc = _dot(pp.astype(BF16), vp) + _dot(pn.astype(BF16), vn)
            outs.append((acc, l))
        o_ref[:, vsl] = _attn_finish(outs[0][0], outs[0][1], outs[1][0], outs[1][1], lam, g_ref[...], lam_init)


def attn_sample(proj, kc, vc, e, bias_past, bias_new, lq1, lk1, lq2, lk2, g, *, ts, lam_init):
    _, bsz, tk, nh, dv = kc.shape
    slab = pl.BlockSpec((1, 1, tk, nh, dv), lambda b: (e, b, 0, 0, 0))
    vec = lambda n: pl.BlockSpec((1, n), lambda b: (0, 0))
    return pl.pallas_call(
        functools.partial(_attn_sample_kernel, lam_init=lam_init),
        grid=(bsz,),
        in_specs=[pl.BlockSpec((ts, HALF), lambda b: (b, 0)),
                  pl.BlockSpec((ts, HALF), lambda b: (b, 1)),
                  pl.BlockSpec((ts, HALF), lambda b: (b, 2)),
                  slab, slab,
                  pl.BlockSpec((A_HEADS, ts, tk), lambda b: (0, 0, 0)),
                  pl.BlockSpec((A_HEADS, ts, ts), lambda b: (0, 0, 0)),
                  vec(A_DH), vec(A_DH), vec(A_DH), vec(A_DH), vec(A_DV)],
        out_specs=pl.BlockSpec((ts, HALF), lambda b: (b, 0)),
        out_shape=jax.ShapeDtypeStruct((bsz * ts, HALF), F32),
        compiler_params=_cparams(("parallel",)),
        name="attn_sample",
    )(proj, proj, proj, kc, vc, bias_past, bias_new, lq1, lk1, lq2, lk2, g)


def _mlstm_kernel(bq_ref, bk_ref, bv_ref, bo_ref, gc_ref, gr_ref, c0_ref, n0_ref, m0_ref, conv0_ref,
                  cw_ref, cb_ref, gbc_ref, gbr_ref, bn_ref,
                  h_ref, c_out, n_out, m_out,
                  xbuf, c_scr, n_scr, m_scr, *, ln, n_pad):
    tb = pl.program_id(1)
    nt = pl.num_programs(1)
    halo = B_CONV - 1
    base = 8 - halo

    @pl.when(tb == 0)
    def _():
        xbuf[base:8, :] = conv0_ref[0]
        c_scr[...] = c0_ref[0]
        n_scr[...] = n0_ref[0]
        for h in range(B_HEADS):
            m_scr[h] = m0_ref[0, :, h:h + 1]

    xbuf[8:8 + ln, 0:HALF] = bq_ref[...]
    xbuf[8:8 + ln, HALF:2 * HALF] = bk_ref[...]
    conv = cb_ref[...] + cw_ref[0:1, :] * xbuf[base:base + ln, :]
    for jj in range(1, B_CONV):
        conv = conv + cw_ref[jj:jj + 1, :] * xbuf[base + jj:base + jj + ln, :]
    xbuf[base:8, :] = xbuf[8 + ln - halo:8 + ln, :]
    conv = conv * _sigmoid(conv)
    q_all = conv[:, 0:HALF] * (B_DK ** -0.5)
    k_all = conv[:, HALF:2 * HALF]

    rows = tb * ln + lax.broadcasted_iota(I32, (ln, 1), 0)
    cols = tb * ln + lax.broadcasted_iota(I32, (1, ln), 1)
    valid_c = rows >= n_pad
    valid_r = cols >= n_pad

    gc = gc_ref[...] + gbc_ref[...]
    gr = gr_ref[0] + gbr_ref[...]
    li_c = jnp.where(valid_c, gc, NEG)
    li_r = jnp.where(valid_r, gr, NEG)
    lf_c = jnp.where(valid_c, -_softplus(-gc), 0.0)
    lf_r = jnp.where(valid_r, -_softplus(-gr), 0.0)

    ri = lax.broadcasted_iota(I32, (ln, ln), 0)
    ci = lax.broadcasted_iota(I32, (ln, ln), 1)
    tril = ri >= ci
    tril_f = jnp.where(tril, 1.0, 0.0)
    triu_f = jnp.where(ri <= ci, 1.0, 0.0)
    b_c = jnp.dot(tril_f, lf_c, preferred_element_type=F32, precision=lax.Precision.HIGHEST)
    b_r = jnp.dot(lf_r, triu_f, preferred_element_type=F32, precision=lax.Precision.HIGHEST)

    for h in range(B_HEADS):
        sl = slice(h * B_DK, (h + 1) * B_DK)
        qh = q_all[:, sl].astype(BF16)
        kh = k_all[:, sl]
        vh = bv_ref[:, sl].astype(BF16)
        c = c_scr[h]
        n = n_scr[h]
        m = m_scr[h]
        bc = b_c[:, B_HEADS + h:B_HEADS + h + 1]
        br = b_r[B_HEADS + h:B_HEADS + h + 1, :]
        inter = bc + m
        dmat = jnp.where(tril, bc - br + li_r[h:h + 1, :], NEG)
        mt = jnp.maximum(inter, jnp.max(dmat, axis=-1, keepdims=True))
        w_inter = jnp.exp(inter - mt)
        s = _dot_nt(qh, kh.astype(BF16)) * jnp.exp(dmat - mt)
        num = w_inter * _dot(qh, c.astype(BF16)) + _dot(s.astype(BF16), vh)
        qn = jnp.sum(qh.astype(F32) * n, axis=-1, keepdims=True)
        den = w_inter * qn + jnp.sum(s, axis=-1, keepdims=True)
        hh = num / jnp.maximum(jnp.abs(den), jnp.exp(-mt))
        b_last = bc[ln - 1:ln, :]
        g = b_last - bc + li_c[:, h:h + 1]
        m_new = jnp.maximum(b_last + m, jnp.max(g, axis=0, keepdims=True))
        decay = jnp.exp(b_last + m - m_new)
        wk = (jnp.exp(g - m_new) * kh)
        c_scr[h] = decay * c + _dot_tn(wk.astype(BF16), vh)
        n_scr[h] = decay * n + jnp.sum(wk, axis=0, keepdims=True)
        m_scr[h] = m_new
        hn = _rms(hh, bn_ref[:, sl]) * _sigmoid(bo_ref[:, sl])
        h_ref[:, sl] = jnp.where(valid_c, hn, 0.0)

    @pl.when(tb == nt - 1)
    def _():
        c_out[0] = c_scr[...]
        n_out[0] = n_scr[...]
        lane = lax.broadcasted_iota(I32, (1, B_HEADS), 1)
        mrow = jnp.zeros((1, B_HEADS), F32)
        for h in range(B_HEADS):
            mrow = jnp.where(lane == h, m_scr[h], mrow)
        m_out[0] = mrow


def mlstm(proj, gates_r, c0, n0, m0, conv0, cw, cb, igb, fgb, bnorm, *, bsz, t, ln, n_pad):
    nt = t // ln
    gb = jnp.concatenate([igb, fgb])
    gw = proj.shape[1] // LANES - 1
    row = lambda c: (lambda b, i: (b * nt + i, c))
    fix2 = lambda b, i: (0, 0)
    out_shapes = (jax.ShapeDtypeStruct((bsz * t, HALF), F32),
                  jax.ShapeDtypeStruct((bsz, B_HEADS, B_DK, B_DK), F32),
                  jax.ShapeDtypeStruct((bsz, B_HEADS, 1, B_DK), F32),
                  jax.ShapeDtypeStruct((bsz, 1, B_HEADS), F32))
    return pl.pallas_call(
        functools.partial(_mlstm_kernel, ln=ln, n_pad=n_pad),
        grid=(bsz, nt),
        in_specs=[pl.BlockSpec((ln, HALF), row(3)), pl.BlockSpec((ln, HALF), row(4)),
                  pl.BlockSpec((ln, HALF), row(5)), pl.BlockSpec((ln, HALF), row(6)),
                  pl.BlockSpec((ln, LANES), row(gw)),
                  pl.BlockSpec((1, 2 * B_HEADS, ln), lambda b, i: (b * nt + i, 0, 0)),
                  pl.BlockSpec((1, B_HEADS, B_DK, B_DK), lambda b, i: (b, 0, 0, 0)),
                  pl.BlockSpec((1, B_HEADS, 1, B_DK), lambda b, i: (b, 0, 0, 0)),
                  pl.BlockSpec((1, 1, B_HEADS), lambda b, i: (b, 0, 0)),
                  pl.BlockSpec((1, B_CONV - 1, 2 * HALF), lambda b, i: (b, 0, 0)),
                  pl.BlockSpec((B_CONV, 2 * HALF), fix2), pl.BlockSpec((1, 2 * HALF), fix2),
                  pl.BlockSpec((1, LANES), fix2), pl.BlockSpec((2 * B_HEADS, 1), fix2),
                  pl.BlockSpec((1, HALF), fix2)],
        out_specs=(pl.BlockSpec((ln, HALF), lambda b, i: (b * nt + i, 0)),
                   pl.BlockSpec((1, B_HEADS, B_DK, B_DK), lambda b, i: (b, 0, 0, 0)),
                   pl.BlockSpec((1, B_HEADS, 1, B_DK), lambda b, i: (b, 0, 0, 0)),
                   pl.BlockSpec((1, 1, B_HEADS), lambda b, i: (b, 0, 0))),
        out_shape=out_shapes,
        scratch_shapes=[pltpu.VMEM((ln + 8, 2 * HALF), F32),
                        pltpu.VMEM((B_HEADS, B_DK, B_DK), F32),
                        pltpu.VMEM((B_HEADS, 1, B_DK), F32),
                        pltpu.VMEM((B_HEADS, 1, 1), F32)],
        compiler_params=_cparams(("parallel", "arbitrary")),
        name="mlstm",
    )(proj, proj, proj, proj, proj, gates_r, c0, n0, m0, conv0, cw, cb,
      jnp.pad(gb, (0, LANES - 2 * B_HEADS))[None], gb[:, None], bnorm)


def _s5_param_kernel(lr_ref, li_ref, ldt_ref, bre_ref, bim_ref, pwr_ref, pwi_ref, amr_ref, ami_ref, bbr_ref, bbi_ref):
    lr = lr_ref[...]
    li = li_ref[...]
    dt = jnp.exp(ldt_ref[...])
    mag = jnp.exp(lr * dt)
    ar = mag * jnp.cos(li * dt)
    ai = mag * jnp.sin(li * dt)
    den = lr * lr + li * li
    cr = ((ar - 1.0) * lr + ai * li) / den
    ci = (ai * lr - (ar - 1.0) * li) / den
    sub = lax.broadcasted_iota(I32, pwr_ref.shape, 0)
    pr, pi = ar, ai
    pwr = jnp.zeros(pwr_ref.shape, F32)
    pwi = jnp.zeros(pwr_ref.shape, F32)
    for t in range(S5_GRP):
        pwr = jnp.where(sub == t, pr, pwr)
        pwi = jnp.where(sub == t, pi, pwi)
        if t + 1 in S5_SHIFTS:
            k = S5_SHIFTS.index(t + 1)
            amr_ref[k] = jnp.where(sub >= t + 1, pr, 0.0)
            ami_ref[k] = jnp.where(sub >= t + 1, pi, 0.0)
        pr, pi = pr * ar - pi * ai, pr * ai + pi * ar
    pwr_ref[...] = pwr
    pwi_ref[...] = pwi
    nr, nc = bre_ref.shape
    same = (lax.broadcasted_iota(I32, (nr, nc), 0) // C_GROUP) == (lax.broadcasted_iota(I32, (nr, nc), 1) // C_STATE)
    br = bre_ref[...]
    bi = bim_ref[...]
    bbr_ref[...] = jnp.where(same, cr * br - ci * bi, 0.0).astype(BF16)
    bbi_ref[...] = jnp.where(same, cr * bi + ci * br, 0.0).astype(BF16)


def s5_params(lr, li, ldt, bre_rep, bim_rep):
    ns = lr.shape[1]
    vm = pl.BlockSpec(memory_space=pltpu.VMEM)
    return pl.pallas_call(
        _s5_param_kernel,
        in_specs=[vm] * 5,
        out_specs=(vm,) * 6,
        out_shape=(jax.ShapeDtypeStruct((S5_GRP, ns), F32), jax.ShapeDtypeStruct((S5_GRP, ns), F32),
                   jax.ShapeDtypeStruct((len(S5_SHIFTS), S5_GRP, ns), F32),
                   jax.ShapeDtypeStruct((len(S5_SHIFTS), S5_GRP, ns), F32),
                   jax.ShapeDtypeStruct(bre_rep.shape, BF16), jax.ShapeDtypeStruct(bre_rep.shape, BF16)),
        compiler_params=pltpu.CompilerParams(vmem_limit_bytes=VMEM_LIMIT),
        name="s5_params",
    )(lr, li, ldt, bre_rep, bim_rep)


def _s5_kernel(u_ref, pwr_ref, pwi_ref, amr_ref, ami_ref, bbr_ref, bbi_ref, cre_ref, cim_ref, d_ref, wg_ref, x0r_ref, x0i_ref,
               o_ref, xr_out, xi_out, sr_scr, si_scr, bur, bui, xra, xia, *, tb_len, n_pad):
    tb = pl.program_id(1)
    nt = pl.num_programs(1)

    @pl.when(tb == 0)
    def _():
        sr_scr[...] = x0r_ref[0]
        si_scr[...] = x0i_ref[0]

    u = u_ref[...]
    ub = u.astype(BF16)
    st_w = S5_UT * C_STATE // C_GROUP
    for c in range(u.shape[1] // S5_UT):
        uc, sc = slice(c * S5_UT, (c + 1) * S5_UT), slice(c * st_w, (c + 1) * st_w)
        bur[:, sc] = _dot(ub[:, uc], bbr_ref[uc, sc])
        bui[:, sc] = _dot(ub[:, uc], bbi_ref[uc, sc])
    def body(grp, carry):
        xr_prev, xi_prev = carry
        base = pl.multiple_of(grp * S5_GRP, S5_GRP)
        xr = bur[pl.ds(base, S5_GRP), :]
        xi = bui[pl.ds(base, S5_GRP), :]
        for k, d in enumerate(S5_SHIFTS):
            sr = pltpu.roll(xr, d, 0)
            si = pltpu.roll(xi, d, 0)
            ar, ai = amr_ref[k], ami_ref[k]
            xr, xi = xr + ar * sr - ai * si, xi + ar * si + ai * sr
        pr, pi = pwr_ref[...], pwi_ref[...]
        xr, xi = xr + pr * xr_prev - pi * xi_prev, xi + pr * xi_prev + pi * xr_prev
        xra[pl.ds(base, S5_GRP), :] = xr
        xia[pl.ds(base, S5_GRP), :] = xi
        return xr[S5_GRP - 1:S5_GRP], xi[S5_GRP - 1:S5_GRP]

    xr, xi = lax.fori_loop(0, tb_len // S5_GRP, body, (sr_scr[...], si_scr[...]))
    sr_scr[...] = xr
    si_scr[...] = xi

    ys = []
    for c in range(u.shape[1] // S5_YT):
        yc = slice(c * S5_YT, (c + 1) * S5_YT)
        sc = slice(c * S5_YT * C_STATE // C_GROUP, (c + 1) * S5_YT * C_STATE // C_GROUP)
        ys.append(_dot(xra[:, sc].astype(BF16), cre_ref[sc, yc]) - _dot(xia[:, sc].astype(BF16), cim_ref[sc, yc]))
    y = jnp.concatenate(ys, axis=1) + d_ref[...] * u
    yg = 0.5 * y * (1.0 + jnp.tanh(math.sqrt(2.0 / math.pi) * (y + 0.044715 * (y * y * y))))
    oc = yg * _sigmoid(_dot(yg.astype(BF16), wg_ref[...]))
    rows = tb * tb_len + lax.broadcasted_iota(I32, (tb_len, 1), 0)
    o_ref[...] = jnp.where(rows >= n_pad, oc, 0.0)

    @pl.when(tb == nt - 1)
    def _():
        xr_out[0] = xr
        xi_out[0] = xi


def s5(proj, pwr, pwi, amr, ami, bbr, bbi, cre, cim, dskip, wglu, x0r, x0i, *, bsz, t, tb_len, n_pad):
    nt = t // tb_len
    ns = pwr.shape[1]
    fix2 = lambda b, i: (0, 0)
    st = pl.BlockSpec((1, 1, ns), lambda b, i: (b, 0, 0))
    return pl.pallas_call(
        functools.partial(_s5_kernel, tb_len=tb_len, n_pad=n_pad),
        grid=(bsz, nt),
        in_specs=[pl.BlockSpec((tb_len, HALF), lambda b, i: (b * nt + i, 0)),
                  pl.BlockSpec((S5_GRP, ns), fix2), pl.BlockSpec((S5_GRP, ns), fix2),
                  pl.BlockSpec(amr.shape, lambda b, i: (0, 0, 0)), pl.BlockSpec(amr.shape, lambda b, i: (0, 0, 0)),
                  pl.BlockSpec((HALF, ns), fix2), pl.BlockSpec((HALF, ns), fix2),
                  pl.BlockSpec((ns, HALF), fix2), pl.BlockSpec((ns, HALF), fix2),
                  pl.BlockSpec((1, HALF), fix2), pl.BlockSpec((HALF, HALF), fix2), st, st],
        out_specs=(pl.BlockSpec((tb_len, HALF), lambda b, i: (b * nt + i, 0)), st, st),
        out_shape=(jax.ShapeDtypeStruct((bsz * t, HALF), F32),
                   jax.ShapeDtypeStruct((bsz, 1, ns), F32), jax.ShapeDtypeStruct((bsz, 1, ns), F32)),
        scratch_shapes=[pltpu.VMEM((1, ns), F32), pltpu.VMEM((1, ns), F32),
                        pltpu.VMEM((tb_len, ns), F32), pltpu.VMEM((tb_len, ns), F32),
                        pltpu.VMEM((tb_len, ns), F32), pltpu.VMEM((tb_len, ns), F32)],
        compiler_params=_cparams(("parallel", "arbitrary")),
        name="s5",
    )(proj, pwr, pwi, amr, ami, bbr, bbi, cre, cim, dskip, wglu, x0r, x0i)


def _rwkv_kernel(r_ref, k_ref, v_ref, lo_ref, sh0_ref, s0_ref,
                 mu_ref, w0_ref, ww2_ref, a0_ref, wa2_ref, wg2_ref, kk_ref, ka_ref, rk_ref, lng_ref, lnb_ref,
                 o_ref, s_out,
                 xbuf, s_scr, w_scr, nkk_scr, b_scr, k_scr, r_scr, v_scr, c_scr, be_scr, ga_scr, y_scr,
                 *, tb_len, n_pad):
    tb = pl.program_id(1)
    nt = pl.num_programs(1)
    npair = D_HEADS // 2
    ncol = xbuf.shape[1]

    @pl.when(tb == 0)
    def _():
        xbuf[7:8, :] = sh0_ref[0]
        s_scr[...] = s0_ref[0]

    xbuf[8:8 + tb_len, 0:D_WIDTH] = r_ref[...]
    xbuf[8:8 + tb_len, D_WIDTH:2 * D_WIDTH] = k_ref[...]
    xbuf[8:8 + tb_len, 2 * D_WIDTH:3 * D_WIDTH] = v_ref[...]
    xbuf[8:8 + tb_len, 3 * D_WIDTH:ncol] = lo_ref[...]
    cur = xbuf[8:8 + tb_len, :]
    prev = xbuf[7:7 + tb_len, :]
    xbuf[7:8, :] = xbuf[7 + tb_len:8 + tb_len, :]
    xm = cur + mu_ref[...] * (prev - cur)
    r = xm[:, 0:D_WIDTH]
    k = xm[:, D_WIDTH:2 * D_WIDTH]
    v = xm[:, 2 * D_WIDTH:3 * D_WIDTH]
    c0 = 3 * D_WIDTH
    wlo = xm[:, c0:c0 + 64]
    alo = xm[:, c0 + 64:c0 + 128]
    glo = xm[:, c0 + 128:c0 + 256]

    w_raw = w0_ref[...] + _dot(jnp.tanh(wlo).astype(BF16), ww2_ref[...])
    decay = jnp.exp(-jnp.exp(-_softplus(-w_raw) - 0.5))
    a = _sigmoid(a0_ref[...] + _dot(alo.astype(BF16), wa2_ref[...]))
    g = _dot(_sigmoid(glo).astype(BF16), wg2_ref[...])

    ones_h = _block_ones(2 * LANES, D_HEAD)
    kk = k * kk_ref[...]
    kk = kk / jnp.maximum(jnp.sqrt(_segsum(kk * kk, ones_h)), 1e-12)
    k2 = k * (1.0 + (a - 1.0) * ka_ref[...])

    nkk = -kk
    bb = kk * a
    w_scr[...] = decay
    nkk_scr[...] = nkk
    b_scr[...] = bb
    k_scr[...] = k2
    r_scr[...] = r
    v_scr[...] = v
    c_scr[...] = pltpu.roll(decay, 1, 0) * nkk
    be_scr[...] = _segsum(pltpu.roll(bb, 1, 0) * nkk, ones_h)
    ga_scr[...] = _segsum(pltpu.roll(k2, 1, 0) * nkk, ones_h)

    eye2 = jnp.where((lax.broadcasted_iota(I32, (D_HEAD, LANES), 1) % D_HEAD)
                     == lax.broadcasted_iota(I32, (D_HEAD, LANES), 0), 1.0, 0.0)

    grp_len = min(GRP_MAX, tb_len)
    sub = lax.broadcasted_iota(I32, (grp_len, LANES), 0)
    pairs = range(npair)
    nrow = npair * D_HEAD
    ri = lax.broadcasted_iota(I32, (2 * LANES, 2 * LANES), 0)
    ci = lax.broadcasted_iota(I32, (2 * LANES, 2 * LANES), 1)
    same_head = (ri % LANES) // D_HEAD == (ci % LANES) // D_HEAD
    ones_s = jnp.where(jnp.logical_and(same_head, ri // LANES >= ci // LANES), 1.0, 0.0).astype(BF16)

    def hilo(parts):
        x = jnp.concatenate(parts, axis=0)
        hi = x.astype(BF16)
        return jnp.concatenate([hi, (x - hi.astype(F32)).astype(BF16)], axis=1)

    def per_pair(res):
        return [res[p * D_HEAD:(p + 1) * D_HEAD] for p in pairs]

    def y_rows(yt, ycol, i):
        return [jnp.where(sub == i, jnp.sum(ycol[p] * eye2, axis=0, keepdims=True), yt[p]) for p in pairs]

    def body(grp, state):
        base = pl.multiple_of(grp * grp_len, grp_len)
        tile = lambda scr: [scr[pl.ds(base, grp_len), p * LANES:(p + 1) * LANES] for p in pairs]
        nkk_t, c_t, be_t, ga_t = tile(nkk_scr), tile(c_scr), tile(be_scr), tile(ga_scr)
        w_t, b_t, k_t, r_t, v_t = tile(w_scr), tile(b_scr), tile(k_scr), tile(r_scr), tile(v_scr)
        vh = [x.astype(BF16).astype(F32) for x in v_t]
        vl = [x - h for x, h in zip(v_t, vh)]

        def vcol_lhs(i):
            return jnp.concatenate([jnp.concatenate([vh[p][i:i + 1] * eye2 for p in pairs], axis=0).astype(BF16),
                                    jnp.concatenate([vl[p][i:i + 1] * eye2 for p in pairs], axis=0).astype(BF16)],
                                   axis=1)

        def segsums(lhs):
            res = _dot(jnp.concatenate(lhs, axis=0), ones_s)
            return [res[i * nrow:(i + 1) * nrow] for i in range(len(lhs))]

        def read_out(yt, res, i):
            second = res[:, LANES:2 * LANES]
            yt = y_rows(yt, per_pair(res[:, 0:LANES] - second), i)
            return y_rows(yt, per_pair(second), i + 1)

        yt = [jnp.zeros((grp_len, LANES), F32) for _ in pairs]
        sp = list(state)
        vc = [per_pair(r[:, 0:LANES]) for r in segsums([vcol_lhs(0), vcol_lhs(1)])]
        yprods = None
        for t0 in range(0, grp_len, 2):
            t1 = t0 + 1
            r0, r1 = slice(t0, t0 + 1), slice(t1, t1 + 1)
            more = t0 + 2 < grp_len
            lhs = [hilo([sp[p] * nkk_t[p][r0] for p in pairs]), hilo([sp[p] * c_t[p][r1] for p in pairs])]
            if more:
                lhs += [vcol_lhs(t0 + 2), vcol_lhs(t0 + 3)]
            if yprods is not None:
                lhs.append(yprods)
            res = segsums(lhs)
            sa0, tmp = per_pair(res[0][:, 0:LANES]), per_pair(res[1][:, 0:LANES])
            if yprods is not None:
                yt = read_out(yt, res[-1], t0 - 2)
            s0 = [sp[p] * w_t[p][r0] + sa0[p] * b_t[p][r0] + vc[0][p] * k_t[p][r0] for p in pairs]
            sa1 = [tmp[p] + sa0[p] * be_t[p][r1] + vc[0][p] * ga_t[p][r1] for p in pairs]
            sp = [s0[p] * w_t[p][r1] + sa1[p] * b_t[p][r1] + vc[1][p] * k_t[p][r1] for p in pairs]
            yprods = jnp.concatenate(
                [jnp.concatenate([s0[p] * r_t[p][r0] for p in pairs], axis=0).astype(BF16),
                 jnp.concatenate([sp[p] * r_t[p][r1] for p in pairs], axis=0).astype(BF16)], axis=1)
            if more:
                vc = [per_pair(res[2][:, 0:LANES]), per_pair(res[3][:, 0:LANES])]
        yt = read_out(yt, segsums([yprods])[0], grp_len - 2)
        for p in pairs:
            y_scr[pl.ds(base, grp_len), p * LANES:(p + 1) * LANES] = yt[p]
        return tuple(sp)

    state = lax.fori_loop(0, tb_len // grp_len, body, tuple(s_scr[p] for p in pairs))
    for p in pairs:
        s_scr[p] = state[p]

    y = y_scr[...]
    inv = 1.0 / D_HEAD
    mean = _segsum(y, ones_h) * inv
    yc = y - mean
    var = _segsum(yc * yc, ones_h) * inv
    y = yc * lax.rsqrt(var + D_GN_EPS) * lng_ref[...] + lnb_ref[...]
    y = y + _segsum(r * k2 * rk_ref[...], ones_h) * v
    rows = tb * tb_len + lax.broadcasted_iota(I32, (tb_len, 1), 0)
    o_ref[...] = jnp.where(rows >= n_pad, y * g, 0.0)

    @pl.when(tb == nt - 1)
    def _():
        s_out[0] = s_scr[...]


def rwkv(proj, sh0, s0, mu, w0, ww2, a0, wa2, wg2, kkp, kap, rkp, lng, lnb, *, bsz, t, tb_len, n_pad):
    nt = t // tb_len
    npair = D_HEADS // 2
    ncol = mu.shape[1]
    nlo = ncol - 3 * D_WIDTH
    row = lambda c: (lambda b, i: (b * nt + i, c))
    fix2 = lambda b, i: (0, 0)
    vec = pl.BlockSpec((1, D_WIDTH), fix2)
    st = pl.BlockSpec((1, npair, D_HEAD, LANES), lambda b, i: (b, 0, 0, 0))
    big = lambda: pltpu.VMEM((tb_len, D_WIDTH), F32)
    return pl.pallas_call(
        functools.partial(_rwkv_kernel, tb_len=tb_len, n_pad=n_pad),
        grid=(bsz, nt),
        in_specs=[pl.BlockSpec((tb_len, D_WIDTH), row(1)), pl.BlockSpec((tb_len, D_WIDTH), row(2)),
                  pl.BlockSpec((tb_len, D_WIDTH), row(3)),
                  pl.BlockSpec((tb_len, nlo), row(4 * D_WIDTH // nlo)),
                  pl.BlockSpec((1, 1, ncol), lambda b, i: (b, 0, 0)), st,
                  pl.BlockSpec((1, ncol), fix2), vec,
                  pl.BlockSpec(ww2.shape, fix2), vec, pl.BlockSpec(wa2.shape, fix2), pl.BlockSpec(wg2.shape, fix2),
                  vec, vec, vec, vec, vec],
        out_specs=(pl.BlockSpec((tb_len, D_WIDTH), lambda b, i: (b * nt + i, 0)), st),
        out_shape=(jax.ShapeDtypeStruct((bsz * t, D_WIDTH), F32),
                   jax.ShapeDtypeStruct((bsz, npair, D_HEAD, LANES), F32)),
        scratch_shapes=[pltpu.VMEM((tb_len + 8, ncol), F32), pltpu.VMEM((npair, D_HEAD, LANES), F32),
                        big(), big(), big(), big(), big(), big(), big(), big(), big(), big()],
        compiler_params=_cparams(("parallel", "arbitrary")),
        name="rwkv7",
    )(proj, proj, proj, proj, sh0, s0, mu, w0, ww2, a0, wa2, wg2, kkp, kap, rkp, lng, lnb)


def _kv_pack_kernel(*refs, n_layers, lead_rows):
    e = pl.program_id(0)
    ins, (ko_ref, vo_ref) = refs[:4 * n_layers], refs[4 * n_layers:]
    tm = ins[0].shape[0]
    off = lead_rows % tm
    for layer in range(n_layers):
        @pl.when(e == layer)
        def _(layer=layer):
            for (a_ref, b_ref), o_ref in ((ins[4 * layer:4 * layer + 2], ko_ref), (ins[4 * layer + 2:4 * layer + 4], vo_ref)):
                rows = jnp.concatenate([a_ref[off:tm, :], b_ref[0:off, :]], axis=0)
                o_ref[0, 0] = pltpu.einshape("k(hd)->khd", rows, h=A_HEADS, d=A_DV)


def kv_pack(projs, n_rows, lead_rows, tm=ROW_TILE):
    n_layers = len(projs)
    nb_in = projs[0].shape[0] // tm
    first = lead_rows // tm
    assert lead_rows % 8 == 0
    nb = -(-n_rows // tm)
    in_specs, args = [], []
    for layer, pr in enumerate(projs):
        for col in (1, 2):
            cur = lambda e, b, layer=layer, col=col: (jnp.where(e == layer, first + b, 0), col)
            nxt = lambda e, b, layer=layer, col=col: (jnp.where(e == layer, jnp.minimum(first + b + 1, nb_in - 1), 0), col)
            in_specs += [pl.BlockSpec((tm, HALF), cur), pl.BlockSpec((tm, HALF), nxt)]
            args += [pr, pr]
    out_spec = pl.BlockSpec((1, 1, tm, A_HEADS, A_DV), lambda e, b: (e, 0, b, 0, 0))
    shape = jax.ShapeDtypeStruct((n_layers, 1, n_rows, A_HEADS, A_DV), F32)
    return pl.pallas_call(
        functools.partial(_kv_pack_kernel, n_layers=n_layers, lead_rows=lead_rows),
        grid=(n_layers, nb),
        in_specs=in_specs,
        out_specs=(out_spec, out_spec),
        out_shape=(shape, shape),
        compiler_params=_cparams(("arbitrary", "arbitrary")),
        name="kv_pack",
    )(*args)


def _pad_cols(w, mult=LANES):
    n = w.shape[1]
    return jnp.pad(w, ((0, 0), (0, (-n) % mult)))


def _pairs_from_heads(s):
    b = s.shape[0]
    return s.reshape(b, D_HEADS // 2, 2, D_HEAD, D_HEAD).transpose(0, 1, 3, 2, 4).reshape(b, D_HEADS // 2, D_HEAD, LANES)


def _heads_from_pairs(s):
    b = s.shape[0]
    return s.reshape(b, D_HEADS // 2, D_HEAD, 2, D_HEAD).transpose(0, 1, 3, 2, 4).reshape(b, D_HEADS, D_HEAD, D_HEAD)


def _trunk(x, p, *, bsz, t, seq_blk, n_pad, attn_fn, st, drop_rows=0):
    depth = p['norm_mix'].shape[0]
    new = {k: [] for k in ('a_k', 'a_v', 'b_c', 'b_n', 'b_m', 'b_conv', 'c_re', 'c_im', 'd_s', 'd_shift', 'proj_even')}
    nt = t // seq_blk
    y = None
    for layer in range(depth):
        g_mix = p['norm_mix'][layer][None]
        if layer % 2 == 0:
            e = layer // 2
            proj = rms_matmul(x, g_mix, p['ev_w_in'][e])
            lam_init = 0.8 - 0.6 * math.exp(-0.3 * layer)
            oa = attn_fn(proj, e, lam_init)
            gcols = proj[:, 7 * HALF:7 * HALF + 2 * B_HEADS]
            gates_r = gcols.reshape(bsz * nt, seq_blk, 2 * B_HEADS).transpose(0, 2, 1)
            hb, bc, bn, bm = mlstm(proj, gates_r, st['b_c'][e], st['b_n'][e][:, :, None, :], st['b_m'][e][:, None, :],
                                   st['b_conv'][e], p['b_conv_w'][e], p['b_conv_b'][e][None],
                                   p['b_ig_bias'][e], p['b_fg_bias'][e], p['b_norm'][e][None],
                                   bsz=bsz, t=t, ln=seq_blk, n_pad=n_pad)
            p3 = proj.reshape(bsz, t, -1)
            new['a_k'].append(p3[:, :, HALF:2 * HALF])
            new['a_v'].append(p3[:, :, 2 * HALF:3 * HALF])
            new['proj_even'].append(proj)
            new['b_c'].append(bc)
            new['b_n'].append(bn[:, :, 0, :])
            new['b_m'].append(bm[:, 0, :])
            new['b_conv'].append(p3[:, t - (B_CONV - 1):, 3 * HALF:5 * HALF])
            mix_in = (oa, hb, p['ev_w_out'][e][:HALF], p['ev_w_out'][e][HALF:])
        else:
            o = layer // 2
            proj = rms_matmul(x, g_mix, p['od_w_in'][o])
            sp = p['s5'][o]
            oc, cre, cim = s5(proj, sp['pwr'], sp['pwi'], sp['amr'], sp['ami'], sp['bbr'], sp['bbi'], sp['cre'], sp['cim'],
                              p['c_d'][o][None], p['c_w_glu'][o],
                              st['c_re'][o].reshape(bsz, 1, -1), st['c_im'][o].reshape(bsz, 1, -1),
                              bsz=bsz, t=t, tb_len=seq_blk, n_pad=n_pad)
            od, ds = rwkv(proj, st['d_shift'][o], _pairs_from_heads(st['d_s'][o]),
                          p['d_mu'][o][None], p['d_w0'][o][None], p['d_w_w2'][o], p['d_a0'][o][None],
                          p['d_w_a2'][o], p['d_w_g2'][o], p['d_k_k'][o][None], p['d_k_a'][o][None],
                          p['d_r_k'][o][None], p['d_ln_g'][o][None], p['d_ln_b'][o][None],
                          bsz=bsz, t=t, tb_len=seq_blk, n_pad=n_pad)
            new['c_re'].append(cre.reshape(bsz, C_GROUPS, C_STATE))
            new['c_im'].append(cim.reshape(bsz, C_GROUPS, C_STATE))
            new['d_s'].append(_heads_from_pairs(ds))
            new['d_shift'].append(proj.reshape(bsz, t, -1)[:, -1:, HALF:])
            mix_in = (oc, od, p['od_w_out'][o][:HALF], p['od_w_out'][o][HALF:])
        last = layer == depth - 1
        x = ffn(x, *mix_in, p['norm_ffn'][layer][None], p['ffn_w1'][layer], p['ffn_w3'][layer], p['ffn_w2'][layer],
                gf=p['norm_final'][None] if last else None, drop_rows=drop_rows if last else 0)
    return x, new


def kernel(x_prompt, x_sample, cache_a_k, cache_a_v, state_b_c, state_b_n, state_b_m, state_b_conv, state_c_re, state_c_im, state_d_s, state_d_shift, meta_tokens, rel_bias, norm_mix, norm_ffn, norm_final, ev_w_in, ev_w_out, a_lq1, a_lk1, a_lq2, a_lk2, a_subln, b_conv_w, b_conv_b, b_ig_bias, b_fg_bias, b_norm, od_w_in, od_w_out, c_lam_re, c_lam_im, c_log_dt, c_b_re, c_b_im, c_c_re, c_c_im, c_d, c_w_glu, d_mu, d_w0, d_w_w2, d_a0, d_w_a2, d_w_g2, d_k_k, d_k_a, d_r_k, d_ln_g, d_ln_b, ffn_w1, ffn_w3, ffn_w2):
    bp, sp_len, dm = x_prompt.shape
    bs, ts, _ = x_sample.shape
    n_even, n_odd = ev_w_in.shape[0], od_w_in.shape[0]
    assert bp == 1 and sp_len % CHUNK == 0 and ts % 2 == 0 and ts >= B_CONV - 1
    dt = x_prompt.dtype

    ns = C_GROUPS * C_STATE
    s5p = []
    for o in range(n_odd):
        bre = jnp.tile(c_b_re[o].transpose(2, 0, 1).reshape(C_GROUP, ns), (C_GROUPS, 1))
        bim = jnp.tile(c_b_im[o].transpose(2, 0, 1).reshape(C_GROUP, ns), (C_GROUPS, 1))
        pwr, pwi, amr, ami, bbr, bbi = s5_params(c_lam_re[o].reshape(1, ns), c_lam_im[o].reshape(1, ns),
                                     jnp.repeat(c_log_dt[o], C_STATE)[None], bre, bim)
        eye = jnp.eye(C_GROUPS, dtype=F32)
        blk = lambda c: (eye[:, None, :, None] * c.transpose(0, 2, 1)[:, :, None, :]).reshape(ns, C_WIDTH).astype(BF16)
        s5p.append(dict(pwr=pwr, pwi=pwi, amr=amr, ami=ami, bbr=bbr, bbi=bbi, cre=blk(c_c_re[o]), cim=blk(c_c_im[o])))
    p = dict(norm_mix=norm_mix, norm_ffn=norm_ffn, norm_final=norm_final,
             ev_w_in=[_pad_cols(ev_w_in[e]).astype(BF16) for e in range(n_even)],
             ev_w_out=ev_w_out.astype(BF16),
             od_w_in=[od_w_in[o].astype(BF16) for o in range(n_odd)], od_w_out=od_w_out.astype(BF16),
             b_conv_w=b_conv_w, b_conv_b=b_conv_b, b_ig_bias=b_ig_bias, b_fg_bias=b_fg_bias, b_norm=b_norm,
             s5=s5p, c_d=c_d, c_w_glu=c_w_glu.astype(BF16),
             d_mu=d_mu, d_w0=d_w0, d_w_w2=d_w_w2.astype(BF16), d_a0=d_a0, d_w_a2=d_w_a2.astype(BF16),
             d_w_g2=d_w_g2.astype(BF16), d_k_k=d_k_k, d_k_a=d_k_a, d_r_k=d_r_k, d_ln_g=d_ln_g, d_ln_b=d_ln_b,
             ffn_w1=ffn_w1.astype(BF16), ffn_w3=ffn_w3.astype(BF16), ffn_w2=ffn_w2.astype(BF16))
    lam_vecs = lambda e: (a_lq1[e][None], a_lk1[e][None], a_lq2[e][None], a_lk2[e][None], a_subln[e][None])

    tp = -(-(sp_len + CHUNK) // ATT_BLK) * ATT_BLK
    n_pad = tp - sp_len - N_META
    xp = jnp.concatenate([jnp.zeros((n_pad, dm), dt), meta_tokens.astype(dt), x_prompt[0]], axis=0)
    bias2 = jnp.stack([bias_tile(rel_bias, ATT_BLK, ATT_BLK, 0, causal=True, scale=LOG2E),
                       bias_tile(rel_bias, ATT_BLK, ATT_BLK, -ATT_BLK, scale=LOG2E)], axis=0)

    def attn_p(proj, e, lam_init):
        return attn_prompt(proj, bias2, rel_bias, *lam_vecs(e), n_pad=n_pad, lam_init=lam_init)

    zeros = lambda *s: jnp.zeros(s, F32)
    st_p = dict(b_c=zeros(n_even, bp, B_HEADS, B_DK, B_DK), b_n=zeros(n_even, bp, B_HEADS, B_DK),
                b_m=zeros(n_even, bp, B_HEADS), b_conv=zeros(n_even, bp, B_CONV - 1, 2 * HALF),
                c_re=zeros(n_odd, bp, C_GROUPS, C_STATE), c_im=zeros(n_odd, bp, C_GROUPS, C_STATE),
                d_s=zeros(n_odd, bp, D_HEADS, D_HEAD, D_HEAD), d_shift=zeros(n_odd, bp, 1, d_mu.shape[1]))
    lead = n_pad + N_META
    drop = lead if lead % ROW_TILE == 0 else 0
    y_p, new_p = _trunk(xp, p, bsz=bp, t=tp, seq_blk=SEQ_BLK, n_pad=n_pad, attn_fn=attn_p, st=st_p, drop_rows=drop)

    tk = cache_a_k.shape[2]
    past_len = tk - N_META
    cid = lambda pos: np.where(pos < N_META, 0, 1 + (pos - N_META) // CHUNK)
    q_pos = N_META + past_len + np.arange(ts)
    k_pos = np.arange(tk + ts)
    assert (cid(k_pos)[None, :] <= cid(q_pos)[:, None]).all()
    bias_past = bias_tile(rel_bias, ts, tk, -(N_META + past_len))
    bias_new = bias_tile(rel_bias, ts, ts, 0)

    def attn_s(proj, e, lam_init):
        return attn_sample(proj, cache_a_k, cache_a_v, e, bias_past, bias_new, *lam_vecs(e), ts=ts, lam_init=lam_init)

    st_s = dict(b_c=state_b_c, b_n=state_b_n, b_m=state_b_m, b_conv=state_b_conv, c_re=state_c_re,
                c_im=state_c_im, d_s=state_d_s, d_shift=state_d_shift)
    y_s, new_s = _trunk(x_sample.reshape(bs * ts, dm), p, bsz=bs, t=ts, seq_blk=ts, n_pad=0, attn_fn=attn_s, st=st_s)

    def pack(new, bsz, t, drop):
        if drop:
            ak, av = kv_pack(new['proj_even'], t - drop, drop)
        else:
            kv = lambda a: a.reshape(bsz, t, A_HEADS, A_DV)
            ak, av = jnp.stack([kv(a) for a in new['a_k']]), jnp.stack([kv(a) for a in new['a_v']])
        return (ak, av, jnp.stack(new['b_c']), jnp.stack(new['b_n']), jnp.stack(new['b_m']), jnp.stack(new['b_conv']),
                jnp.stack(new['c_re']), jnp.stack(new['c_im']), jnp.stack(new['d_s']), jnp.stack(new['d_shift']))

    out_p = pack(new_p, bp, tp, n_pad)
    out_s = pack(new_s, bs, ts, 0)
    return (y_p[lead - drop:][None], y_s.reshape(bs, ts, dm)) + out_p + out_s
```
